```python
import math
import jax, jax.numpy as jnp
from jax import lax
import numpy as np

D_MODEL = 1024
BATCH = 4
SEQ = 8192
DEPTH = 2
DEC_BATCH = 8
DEC_SEQ = 64
PAST_LEN = 2048

CHUNK = 64
EPS = 1e-6
D_FF = 2816
SGU_WIDTH = D_MODEL // 4
SGU_HEADS = 4
SGU_HEAD_DIM = SGU_WIDTH // SGU_HEADS
SGU_CHUNK = 128
POOL_WIDTH = D_MODEL // 4
POOL_WINDOWS = (2, 4, 8, 16)
POOL_GROUPS = len(POOL_WINDOWS)
POOL_GROUP_DIM = POOL_WIDTH // POOL_GROUPS
POOL_HIST = max(POOL_WINDOWS) - 1
GDN_WIDTH = D_MODEL // 2
GDN_HEADS = 4
GDN_DK = GDN_WIDTH // GDN_HEADS
GDN_DV = GDN_WIDTH // GDN_HEADS
GDN_KEY = GDN_HEADS * GDN_DK
GDN_QKV = 2 * GDN_KEY + GDN_WIDTH
GDN_CONV = 4
GDN_CHUNK = 64
MIX_WIDTH = SGU_WIDTH + POOL_WIDTH + GDN_WIDTH
MEM_LEN = 256
MEM_HEADS = 4
MEM_HEAD_DIM = 128
MEM_WIDTH = MEM_HEADS * MEM_HEAD_DIM
OFF_POOL = 2 * SGU_WIDTH
OFF_QKV = OFF_POOL + POOL_WIDTH
OFF_Z = OFF_QKV + GDN_QKV
OFF_BETA = OFF_Z + GDN_WIDTH
OFF_A = OFF_BETA + GDN_HEADS
N_IN = OFF_A + GDN_HEADS

kernel_name = 'hybrid_streaming_encoder_step'


def rms_norm(x, g):
    xf = x.astype(jnp.float32)
    y = xf * lax.rsqrt(jnp.mean(xf * xf, axis=-1, keepdims=True) + EPS)
    return (y * g.astype(jnp.float32)).astype(x.dtype)


def _head_layer_norm(v, g):
    vf = v.astype(jnp.float32)
    vc = vf - jnp.mean(vf, axis=-1, keepdims=True)
    y = vc * lax.rsqrt(jnp.mean(vc * vc, axis=-1, keepdims=True) + EPS)
    return (y * g.astype(jnp.float32)).astype(v.dtype)


def _l2norm(t):
    return t * lax.rsqrt(jnp.sum(t * t, axis=-1, keepdims=True) + 1e-6)


def swiglu(h, w_gate, w_up, w_down):
    return (jax.nn.silu(h @ w_gate) * (h @ w_up)) @ w_down


def sgu_mix(a, w_s, b_s, g_n):
    B, L, _ = a.shape
    C = min(SGU_CHUNK, L)
    u = a[..., :SGU_WIDTH]
    v = _head_layer_norm(a[..., SGU_WIDTH:].reshape(B, L, SGU_HEADS, SGU_HEAD_DIM),
                         g_n.reshape(SGU_HEADS, SGU_HEAD_DIM))
    blk = np.arange(C) // CHUNK
    mask = blk[:, None] >= blk[None, :]
    w = jnp.where(mask, w_s[:, :C, :C], 0.0).astype(v.dtype)
    s = jnp.einsum('hij,bnjhc->bnihc', w, v.reshape(B, L // C, C, SGU_HEADS, SGU_HEAD_DIM))
    s = s + b_s[:, :C].T[None, None, :, :, None].astype(s.dtype)
    return u * s.reshape(B, L, SGU_WIDTH), v.reshape(B, L, SGU_WIDTH)


def pool_mix(p, hist, pos, w_pool, scale):
    B, L, W = p.shape
    xp = jnp.concatenate([hist.astype(p.dtype), p], axis=1)
    cs = jnp.concatenate([jnp.zeros((B, 1, W), jnp.float32),
                          jnp.cumsum(xp.astype(jnp.float32), axis=1)], axis=1)
    end = cs[:, POOL_HIST + 1:]
    means = []
    for gi, w in enumerate(POOL_WINDOWS):
        sl = slice(gi * POOL_GROUP_DIM, (gi + 1) * POOL_GROUP_DIM)
        start = cs[:, POOL_HIST + 1 - w:POOL_HIST + 1 - w + L, sl]
        cnt = jnp.minimum(w, pos + 1).astype(jnp.float32)[None, :, None]
        means.append((end[..., sl] - start) / cnt)
    d = (jnp.concatenate(means, axis=-1) - p.astype(jnp.float32)).astype(p.dtype)
    y = jnp.einsum('blgc,gcd->blgd', d.reshape(B, L, POOL_GROUPS, POOL_GROUP_DIM), w_pool)
    return y.reshape(B, L, W) * scale, xp[:, -POOL_HIST:]


def causal_conv(x, hist, w):
    L = x.shape[1]
    xc = jnp.concatenate([hist.astype(x.dtype), x], axis=1)
    y = xc[:, 0:L] * w[0]
    for j in range(1, GDN_CONV):
        y = y + xc[:, j:j + L] * w[j]
    return y, xc[:, -(GDN_CONV - 1):]


def gated_delta_rule(q, k, v, g, beta, s0):
    B, L, H, Dk = q.shape
    C = min(GDN_CHUNK, L)
    N = L // C

    def chunks(t):
        return jnp.moveaxis(t.reshape((B, N, C, H) + t.shape[3:]), (1, 3), (0, 2))

    qc, kc, vc = chunks(q * (Dk ** -0.5)), chunks(k), chunks(v)
    gc = jnp.cumsum(chunks(g), axis=-1)
    bc = chunks(beta)
    kb = kc * bc[..., None]
    vb = vc * bc[..., None]
    incl = np.tril(np.ones((C, C), dtype=bool))
    strict = np.tril(np.ones((C, C), dtype=bool), -1)
    decay = jnp.exp(jnp.where(incl, gc[..., :, None] - gc[..., None, :], -jnp.inf))
    lmat = jnp.where(strict, jnp.einsum('nbhid,nbhjd->nbhij', kb, kc) * decay, 0.0)
    eye = jnp.eye(C, dtype=jnp.float32)
    t_inv = lax.linalg.triangular_solve(lmat + eye, jnp.broadcast_to(eye, lmat.shape),
                                        left_side=True, lower=True, unit_diagonal=True)
    u = t_inv @ vb
    w = t_inv @ (kb * jnp.exp(gc)[..., None])
    qk = jnp.einsum('nbhid,nbhjd->nbhij', qc, kc) * decay
    qg = qc * jnp.exp(gc)[..., None]
    kg = kc * jnp.exp(gc[..., -1:] - gc)[..., None]
    gl = jnp.exp(gc[..., -1])

    def step(S, xs):
        u_i, w_i, qg_i, qk_i, kg_i, gl_i = xs
        v_new = u_i - w_i @ S
        o_i = qg_i @ S + qk_i @ v_new
        S = S * gl_i[..., None, None] + jnp.einsum('bhcd,bhce->bhde', kg_i, v_new)
        return S, o_i

    S, o = lax.scan(step, s0, (u, w, qg, qk, kg, gl))
    o = jnp.moveaxis(o, (0, 2), (1, 3)).reshape(B, L, H, v.shape[-1])
    return o, S


def gdn_mix(proj, conv_hist, s0, conv_w, a_log, dt_bias, norm_g):
    B, L, _ = proj.shape
    qkv, conv_new = causal_conv(proj[..., OFF_QKV:OFF_Z], conv_hist, conv_w)
    qkv = jax.nn.silu(qkv.astype(jnp.float32))
    q = _l2norm(qkv[..., :GDN_KEY].reshape(B, L, GDN_HEADS, GDN_DK))
    k = _l2norm(qkv[..., GDN_KEY:2 * GDN_KEY].reshape(B, L, GDN_HEADS, GDN_DK))
    v = qkv[..., 2 * GDN_KEY:].reshape(B, L, GDN_HEADS, GDN_DV)
    beta = jax.nn.sigmoid(proj[..., OFF_BETA:OFF_A].astype(jnp.float32))
    g = -jnp.exp(a_log.astype(jnp.float32)) * jax.nn.softplus(
        proj[..., OFF_A:N_IN].astype(jnp.float32) + dt_bias.astype(jnp.float32))
    o, s_new = gated_delta_rule(q, k, v, g, beta, s0.astype(jnp.float32))
    z = proj[..., OFF_Z:OFF_BETA].reshape(B, L, GDN_HEADS, GDN_DV).astype(jnp.float32)
    o = rms_norm(o, norm_g) * jax.nn.silu(z)
    return o.reshape(B, L, GDN_WIDTH).astype(proj.dtype), conv_new, s_new


def mem_kv(mem, g, w_k, w_v):
    B, M, _ = mem.shape
    m = rms_norm(mem, g)
    return ((m @ w_k).reshape(B, M, MEM_HEADS, MEM_HEAD_DIM),
            (m @ w_v).reshape(B, M, MEM_HEADS, MEM_HEAD_DIM))


def mem_attend(h, mk, mv, w_q, w_o):
    B, L, _ = h.shape
    q = (h @ w_q).reshape(B, L, MEM_HEADS, MEM_HEAD_DIM)
    s = jnp.einsum('blhd,bmhd->bhlm', q, mk.astype(q.dtype)).astype(jnp.float32) * (MEM_HEAD_DIM ** -0.5)
    p = jax.nn.softmax(s, axis=-1).astype(q.dtype)
    o = jnp.einsum('bhlm,bmhd->blhd', p, mv.astype(q.dtype)).reshape(B, L, MEM_WIDTH)
    return o @ w_o


def layer(x, mk, mv, pool_hist, conv_hist, s0, pos,
          ffn1_norm, ffn1_w_gate, ffn1_w_up, ffn1_w_down,
          mix_norm, w_in, sgu_norm, sgu_w, sgu_b, pool_w, pool_scale,
          gdn_conv_w, gdn_a_log, gdn_dt_bias, gdn_out_norm, w_out,
          xattn_norm, w_mq, w_mo,
          ffn2_norm, ffn2_w_gate, ffn2_w_up, ffn2_w_down):
    h = x + 0.5 * swiglu(rms_norm(x, ffn1_norm), ffn1_w_gate, ffn1_w_up, ffn1_w_down)
    proj = rms_norm(h, mix_norm) @ w_in
    y_a, v_sgu = sgu_mix(jax.nn.gelu(proj[..., :OFF_POOL]), sgu_w, sgu_b, sgu_norm)
    y_b, pool_new = pool_mix(proj[..., OFF_POOL:OFF_QKV], pool_hist, pos, pool_w, pool_scale)
    y_c, conv_new, s_new = gdn_mix(proj, conv_hist, s0, gdn_conv_w, gdn_a_log, gdn_dt_bias, gdn_out_norm)
    h = h + jnp.concatenate([y_a, y_b, y_c], axis=-1) @ w_out
    h = h + mem_attend(rms_norm(h, xattn_norm), mk, mv, w_mq, w_mo)
    h = h + 0.5 * swiglu(rms_norm(h, ffn2_norm), ffn2_w_gate, ffn2_w_up, ffn2_w_down)
    return h, v_sgu, pool_new, conv_new, s_new


def setup_inputs(seed: int = 0) -> dict:
    key = jax.random.key(seed)
    ks = iter(jax.random.split(key, 48))
    f32 = jnp.float32
    L = DEPTH

    def nrm(shape, scale):
        return jax.random.normal(next(ks), shape, f32) * scale

    def gain(shape, s=0.02):
        return 1.0 + s * jax.random.normal(next(ks), shape, f32)

    inp = {}
    inp['x_prompt'] = nrm((BATCH, SEQ, D_MODEL), 1.0)
    inp['x_sample'] = nrm((DEC_BATCH, DEC_SEQ, D_MODEL), 1.0)
    inp['mem_prompt'] = nrm((BATCH, MEM_LEN, D_MODEL), 1.0)
    inp['cache_mem_k'] = nrm((L, DEC_BATCH, MEM_LEN, MEM_HEADS, MEM_HEAD_DIM), 1.0)
    inp['cache_mem_v'] = nrm((L, DEC_BATCH, MEM_LEN, MEM_HEADS, MEM_HEAD_DIM), 1.0)
    inp['state_pool'] = nrm((L, DEC_BATCH, POOL_HIST, POOL_WIDTH), 1.0)
    inp['state_conv'] = nrm((L, DEC_BATCH, GDN_CONV - 1, GDN_QKV), 1.0)
    inp['state_ssm'] = nrm((L, DEC_BATCH, GDN_HEADS, GDN_DK, GDN_DV), GDN_DK ** -0.5)
    inp['ffn1_norm'] = gain((L, D_MODEL))
    inp['ffn1_w_gate'] = nrm((L, D_MODEL, D_FF), D_MODEL ** -0.5)
    inp['ffn1_w_up'] = nrm((L, D_MODEL, D_FF), D_MODEL ** -0.5)
    inp['ffn1_w_down'] = nrm((L, D_FF, D_MODEL), D_FF ** -0.5)
    inp['mix_norm'] = gain((L, D_MODEL))
    inp['w_in'] = nrm((L, D_MODEL, N_IN), D_MODEL ** -0.5)
    inp['sgu_norm'] = gain((L, SGU_WIDTH))
    inp['sgu_w'] = nrm((L, SGU_HEADS, SGU_CHUNK, SGU_CHUNK), SGU_CHUNK ** -0.5)
    inp['sgu_b'] = gain((L, SGU_HEADS, SGU_CHUNK), 0.01)
    inp['pool_w'] = nrm((L, POOL_GROUPS, POOL_GROUP_DIM, POOL_GROUP_DIM), POOL_GROUP_DIM ** -0.5)
    inp['pool_scale'] = gain((L, POOL_WIDTH), 0.1)
    inp['gdn_conv_w'] = nrm((L, GDN_CONV, GDN_QKV), 0.5)
    inp['gdn_a_log'] = jnp.log(jax.random.uniform(next(ks), (L, GDN_HEADS), f32, 1.0, 16.0))
    dt = jnp.exp(jax.random.uniform(next(ks), (L, GDN_HEADS), f32, math.log(1e-3), math.log(1e-1)))
    inp['gdn_dt_bias'] = dt + jnp.log(-jnp.expm1(-dt))
    inp['gdn_out_norm'] = gain((L, GDN_DV))
    inp['w_out'] = nrm((L, MIX_WIDTH, D_MODEL), MIX_WIDTH ** -0.5)
    inp['xattn_norm'] = gain((L, D_MODEL))
    inp['mem_norm'] = gain((L, D_MODEL))
    inp['w_mq'] = nrm((L, D_MODEL, MEM_WIDTH), D_MODEL ** -0.5)
    inp['w_mk'] = nrm((L, D_MODEL, MEM_WIDTH), D_MODEL ** -0.5)
    inp['w_mv'] = nrm((L, D_MODEL, MEM_WIDTH), D_MODEL ** -0.5)
    inp['w_mo'] = nrm((L, MEM_WIDTH, D_MODEL), MEM_WIDTH ** -0.5)
    inp['ffn2_norm'] = gain((L, D_MODEL))
    inp['ffn2_w_gate'] = nrm((L, D_MODEL, D_FF), D_MODEL ** -0.5)
    inp['ffn2_w_up'] = nrm((L, D_MODEL, D_FF), D_MODEL ** -0.5)
    inp['ffn2_w_down'] = nrm((L, D_FF, D_MODEL), D_FF ** -0.5)
    inp['final_norm'] = gain((D_MODEL,))
    return inp


def reference(x_prompt, x_sample, mem_prompt, cache_mem_k, cache_mem_v, state_pool, state_conv, state_ssm,
              ffn1_norm, ffn1_w_gate, ffn1_w_up, ffn1_w_down,
              mix_norm, w_in, sgu_norm, sgu_w, sgu_b, pool_w, pool_scale,
              gdn_conv_w, gdn_a_log, gdn_dt_bias, gdn_out_norm, w_out,
              xattn_norm, mem_norm, w_mq, w_mk, w_mv, w_mo,
              ffn2_norm, ffn2_w_gate, ffn2_w_up, ffn2_w_down, final_norm):
    Bp, Lp, _ = x_prompt.shape
    Ls = x_sample.shape[1]
    pos_p = jnp.arange(Lp, dtype=jnp.int32)
    pos_s = PAST_LEN + jnp.arange(Ls, dtype=jnp.int32)
    pool0 = jnp.zeros((Bp, POOL_HIST, POOL_WIDTH), x_prompt.dtype)
    conv0 = jnp.zeros((Bp, GDN_CONV - 1, GDN_QKV), x_prompt.dtype)
    ssm0 = jnp.zeros((Bp, GDN_HEADS, GDN_DK, GDN_DV), jnp.float32)
    xp, xs = x_prompt, x_sample
    p_pool, p_conv, p_ssm, p_mk, p_mv = [], [], [], [], []
    s_pool, s_conv, s_ssm, s_v = [], [], [], []
    for l in range(DEPTH):
        lw = (ffn1_norm[l], ffn1_w_gate[l], ffn1_w_up[l], ffn1_w_down[l],
              mix_norm[l], w_in[l], sgu_norm[l], sgu_w[l], sgu_b[l], pool_w[l], pool_scale[l],
              gdn_conv_w[l], gdn_a_log[l], gdn_dt_bias[l], gdn_out_norm[l], w_out[l],
              xattn_norm[l], w_mq[l], w_mo[l],
              ffn2_norm[l], ffn2_w_gate[l], ffn2_w_up[l], ffn2_w_down[l])
        mk, mv = mem_kv(mem_prompt, mem_norm[l], w_mk[l], w_mv[l])
        xp, _, pp, pc, ps = layer(xp, mk, mv, pool0, conv0, ssm0, pos_p, *lw)
        xs, sv, sp, sc, ss = layer(xs, cache_mem_k[l], cache_mem_v[l], state_pool[l], state_conv[l],
                                   state_ssm[l], pos_s, *lw)
        p_pool.append(pp); p_conv.append(pc); p_ssm.append(ps); p_mk.append(mk); p_mv.append(mv)
        s_pool.append(sp); s_conv.append(sc); s_ssm.append(ss); s_v.append(sv)
    y_prompt = rms_norm(xp, final_norm)
    y_sample = rms_norm(xs, final_norm)
    prompt_state_pool = jnp.stack(p_pool)
    prompt_state_conv = jnp.stack(p_conv)
    prompt_state_ssm = jnp.stack(p_ssm)
    prompt_mem_k = jnp.stack(p_mk)
    prompt_mem_v = jnp.stack(p_mv)
    sample_state_pool = jnp.stack(s_pool)
    sample_state_conv = jnp.stack(s_conv)
    sample_state_ssm = jnp.stack(s_ssm)
    sample_sgu_v = jnp.stack(s_v)
    return (y_prompt, y_sample, prompt_state_pool, prompt_state_conv, prompt_state_ssm, prompt_mem_k, prompt_mem_v,
            sample_state_pool, sample_state_conv, sample_state_ssm, sample_sgu_v)
```

```python
import functools
import math

import jax
import jax.numpy as jnp
from jax import lax
from jax.experimental import pallas as pl
from jax.experimental.pallas import tpu as pltpu

F32 = jnp.float32
BF16 = jnp.bfloat16

EPS = 1e-6
L2_EPS = 1e-6
LANES = 128
MXU_COLS = 256
VMEM_LIMIT_BYTES = 56 * 1024 * 1024

SGU_HEADS = 4
POOL_WINDOWS = (2, 4, 8, 16)
POOL_HIST = max(POOL_WINDOWS) - 1
POOL_PAD = 16
GDN_HEADS = 4
GDN_CONV = 4
CONV_PAD = 8
GDN_CHUNK = 64
MEM_HEADS = 4
PAST_LEN = 2048
NEG_BIG = -1e30


def _dot(a, b):
    return jnp.dot(a, b, preferred_element_type=F32)


def _dot_nt(a, b):
    return lax.dot_general(a, b, (((1,), (1,)), ((), ())), preferred_element_type=F32)


def _rms(x, g):
    return x * lax.rsqrt(jnp.mean(x * x, axis=-1, keepdims=True) + EPS) * g


def _sigmoid(x):
    return 1.0 / (1.0 + jnp.exp(-x))


def _gelu_tanh(x):
    return 0.5 * x * (1.0 + jnp.tanh(math.sqrt(2.0 / math.pi) * (x + 0.044715 * (x * x * x))))


def _softplus(x):
    return jnp.maximum(x, 0.0) + jnp.log1p(jnp.exp(-jnp.abs(x)))


def _swiglu(nb_ref, wg_ref, wu_ref, wd_ref, act_ref):
    d_ff = wg_ref.shape[1]
    for c in range(d_ff // MXU_COLS):
        sl = slice(c * MXU_COLS, (c + 1) * MXU_COLS)
        g = _dot(nb_ref[...], wg_ref[:, sl])
        u = _dot(nb_ref[...], wu_ref[:, sl])
        act_ref[:, sl] = (g * _sigmoid(g) * u).astype(BF16)
    return _dot(act_ref[...], wd_ref[...])


def _ffn_in_kernel(x_ref, g1_ref, wg_ref, wu_ref, wd_ref, gm_ref, win_ref,
                   h_ref, a_ref, p_ref, qkv_ref, zg_ref, nb_ref, act_ref, *, splits):
    x = x_ref[...]
    nb_ref[...] = _rms(x, g1_ref[...]).astype(BF16)
    h = x + 0.5 * _swiglu(nb_ref, wg_ref, wu_ref, wd_ref, act_ref)
    h_ref[...] = h
    nb_ref[...] = _rms(h, gm_ref[...]).astype(BF16)
    o_pool, o_qkv, o_z, n_pad = splits
    a_ref[...] = _gelu_tanh(_dot(nb_ref[...], win_ref[:, 0:o_pool]))
    p_ref[...] = _dot(nb_ref[...], win_ref[:, o_pool:o_qkv])
    qkv_ref[...] = _dot(nb_ref[...], win_ref[:, o_qkv:o_z])
    zg_ref[...] = _dot(nb_ref[...], win_ref[:, o_z:n_pad])


def _resident(shape):
    nd = len(shape)
    return pl.BlockSpec(shape, lambda *_: (0,) * nd, pipeline_mode=pl.Buffered(1))


def _ffn_in(x2d, lw, tm):
    t, d = x2d.shape
    d_ff = lw["wg1"].shape[1]
    o_pool, o_qkv, o_z, n_pad = lw["splits"]
    row = lambda width: pl.BlockSpec((tm, width), lambda i: (i, 0))
    widths = (d, o_pool, o_qkv - o_pool, o_z - o_qkv, n_pad - o_z)
    return pl.pallas_call(
        functools.partial(_ffn_in_kernel, splits=lw["splits"]),
        grid=(t // tm,),
        in_specs=[row(d), _resident((1, d)), _resident((d, d_ff)), _resident((d, d_ff)),
                  _resident((d_ff, d)), _resident((1, d)), _resident((d, n_pad))],
        out_specs=[row(w) for w in widths],
        out_shape=[jax.ShapeDtypeStruct((t, w), F32) for w in widths],
        scratch_shapes=[pltpu.VMEM((tm, d), BF16), pltpu.VMEM((tm, d_ff), BF16)],
        compiler_params=pltpu.CompilerParams(dimension_semantics=("parallel",),
                                             vmem_limit_bytes=VMEM_LIMIT_BYTES),
        name="ffn_in",
    )(x2d, lw["g1"], lw["wg1"], lw["wu1"], lw["wd1"], lw["gmix"], lw["win"])


def _mixer_kernel(a_ref, p_ref, qkv_ref, zg_ref, h_ref, pool0_ref, conv0_ref, ssm0_ref,
                  sgw_ref, sgb_ref, sgn_ref, avg_ref, poolw_ref, pools_ref,
                  convw_ref, alog_ref, dtb_ref, onorm_ref, wout_ref,
                  h2_ref, sguv_ref, poolst_ref, convst_ref, ssm_ref,
                  pbuf, cbuf, q_s, k_s, v_s, beta_s, g_s, u_s, wq_s, ql_s, gl_s, o_s, y_s, s_s,
                  *, ts, c_sgu, pos0):
    j = pl.program_id(1)
    nj = pl.num_programs(1)
    sgu_w = sgn_ref.shape[1]
    pool_w = pools_ref.shape[1]
    gdn_w = onorm_ref.shape[1]
    dk = gdn_w // GDN_HEADS
    cc = GDN_CHUNK
    nch = ts // cc

    @pl.when(j == 0)
    def _():
        pbuf[0:POOL_PAD, :] = pool0_ref[...]
        cbuf[0:CONV_PAD, :] = conv0_ref[...]
        s_s[...] = ssm0_ref[...]

    a = a_ref[...]
    v = a[:, sgu_w:]
    avg = avg_ref[...]

    def seg_mean(x):
        hi = x.astype(BF16)
        lo = (x - hi.astype(F32)).astype(BF16)
        return _dot(hi, avg) + _dot(lo, avg)

    vc = v - seg_mean(v)
    vn = vc * lax.rsqrt(seg_mean(vc * vc) + EPS) * sgn_ref[...]
    sguv_ref[...] = vn
    hd = sgu_w // SGU_HEADS
    wi = lax.broadcasted_iota(jnp.int32, (c_sgu, SGU_HEADS * c_sgu), 0)
    wj = lax.broadcasted_iota(jnp.int32, (c_sgu, SGU_HEADS * c_sgu), 1) % c_sgu
    wmask = jnp.where(wi // 64 >= wj // 64, sgw_ref[...], 0.0).astype(BF16)
    lane_head = lax.broadcasted_iota(jnp.int32, (c_sgu, sgu_w), 1) // hd
    for c in range(ts // c_sgu):
        rows = slice(c * c_sgu, (c + 1) * c_sgu)
        vch = vn[rows]
        vstack = jnp.concatenate([jnp.where(lane_head == hh, vch, 0.0) for hh in range(SGU_HEADS)],
                                 axis=0).astype(BF16)
        s = _dot(wmask, vstack) + sgb_ref[...]
        y_s[rows, 0:sgu_w] = (a[rows, 0:sgu_w] * s).astype(BF16)

    p = p_ref[...]
    pbuf[POOL_PAD:POOL_PAD + ts, :] = p
    lane = lax.broadcasted_iota(jnp.int32, (ts, pool_w), 1)
    gdim = pool_w // len(POOL_WINDOWS)
    wlane = jnp.full((ts, pool_w), POOL_WINDOWS[-1], jnp.int32)
    for gi in range(len(POOL_WINDOWS) - 2, -1, -1):
        wlane = jnp.where(lane < (gi + 1) * gdim, POOL_WINDOWS[gi], wlane)
    acc = p
    for jj in range(1, POOL_HIST + 1):
        acc = acc + jnp.where(wlane > jj, pbuf[POOL_PAD - jj:POOL_PAD - jj + ts, :], 0.0)
    pos = pos0 + j * ts + lax.broadcasted_iota(jnp.int32, (ts, pool_w), 0)
    cnt = jnp.minimum(wlane, pos + 1).astype(F32)
    dlt = (acc / cnt - p).astype(BF16)
    y_s[:, sgu_w:sgu_w + pool_w] = (_dot(dlt, poolw_ref[...]) * pools_ref[...]).astype(BF16)
    tail = pbuf[ts:ts + POOL_PAD, :]
    poolst_ref[...] = tail
    pbuf[0:POOL_PAD, :] = tail

    cbuf[CONV_PAD:CONV_PAD + ts, :] = qkv_ref[...]
    qscale = dk ** -0.5
    for cb in range(3 * gdn_w // dk):
        cs = slice(cb * dk, (cb + 1) * dk)
        y = cbuf[CONV_PAD:CONV_PAD + ts, cs] * convw_ref[GDN_CONV - 1:GDN_CONV, cs]
        for t in range(1, GDN_CONV):
            y = y + cbuf[CONV_PAD - t:CONV_PAD - t + ts, cs] * convw_ref[GDN_CONV - 1 - t:GDN_CONV - t, cs]
        y = y * _sigmoid(y)
        which, hh = divmod(cb, GDN_HEADS)
        hs = slice(hh * dk, (hh + 1) * dk)
        if which == 0:
            q_s[:, hs] = y * (lax.rsqrt(jnp.sum(y * y, axis=-1, keepdims=True) + L2_EPS) * qscale)
        elif which == 1:
            k_s[:, hs] = y * lax.rsqrt(jnp.sum(y * y, axis=-1, keepdims=True) + L2_EPS)
        else:
            v_s[:, hs] = y
    ctail = cbuf[ts:ts + CONV_PAD, :]
    convst_ref[...] = ctail
    cbuf[0:CONV_PAD, :] = ctail

    gates = zg_ref[:, gdn_w:gdn_w + LANES]
    beta_s[...] = _sigmoid(gates)
    g_s[...] = -jnp.exp(alog_ref[...]) * _softplus(gates + dtb_ref[...])

    ri = lax.broadcasted_iota(jnp.int32, (cc, cc), 0)
    ci = lax.broadcasted_iota(jnp.int32, (cc, cc), 1)
    incl = ri >= ci
    strict = ri > ci
    pi = lax.broadcasted_iota(jnp.int32, (4 * cc, cc), 0)
    pj = lax.broadcasted_iota(jnp.int32, (4 * cc, cc), 1)
    pref = jnp.where(pi < cc, jnp.where(pj <= pi, 1.0, 0.0),
                     jnp.where(pi < 2 * cc, jnp.where(pj > pi - cc, 1.0, 0.0), 1.0))

    def chunk_a(n, carry):
        r0 = pl.multiple_of(n * cc, cc)
        rows = pl.ds(r0, cc)
        sums = jnp.dot(pref, g_s[rows, :], preferred_element_type=F32, precision=lax.Precision.HIGHEST)
        gc = sums[0:cc]
        gct = gc.T
        egc = jnp.exp(gc)
        erg = jnp.exp(sums[cc:2 * cc])
        gl_s[n] = jnp.exp(sums[2 * cc:4 * cc])
        beta = beta_s[rows, :]
        for hh in range(GDN_HEADS):
            hs = slice(hh * dk, (hh + 1) * dk)
            gl = GDN_HEADS + hh
            q = q_s[rows, hs]
            k = k_s[rows, hs]
            vv = v_s[rows, hs]
            b = beta[:, hh:hh + 1]
            eg = egc[:, gl:gl + 1]
            kb = k * b
            kbq = jnp.concatenate([kb, q], axis=0).astype(BF16)
            prod = _dot_nt(kbq, k.astype(BF16))
            decay = jnp.exp(jnp.where(incl, gc[:, gl:gl + 1] - gct[gl:gl + 1, :], NEG_BIG))
            nmat = jnp.where(strict, -(prod[0:cc] * decay), 0.0)
            tp = nmat
            m = nmat
            for _ in range(5):
                mb = m.astype(BF16)
                m = _dot(mb, mb)
                tp = tp + m + _dot(tp.astype(BF16), m.astype(BF16))
            rhs = jnp.concatenate([vv * b, kb * eg], axis=1)
            sol = rhs + _dot(tp.astype(BF16), rhs.astype(BF16))
            u_s[rows, hs] = sol[:, 0:dk]
            wq_s[n, hh, 0:cc, :] = sol[:, dk:2 * dk].astype(BF16)
            wq_s[n, hh, cc:2 * cc, :] = (q * eg).astype(BF16)
            ql_s[n, hh, 0:cc, :] = (prod[cc:2 * cc] * decay).astype(BF16)
            ql_s[n, hh, cc:cc + dk, :] = (k * erg[:, gl:gl + 1]).T.astype(BF16)
        return carry

    lax.fori_loop(0, nch, chunk_a, 0)

    def chunk_b(n, carry):
        r0 = pl.multiple_of(n * cc, cc)
        rows = pl.ds(r0, cc)
        glm = gl_s[n]
        for hh in range(GDN_HEADS):
            hs = slice(hh * dk, (hh + 1) * dk)
            gl = GDN_HEADS + hh
            st = s_s[hh]
            r1 = _dot(wq_s[n, hh], st.astype(BF16))
            vnew = u_s[rows, hs] - r1[0:cc]
            r2 = _dot(ql_s[n, hh], vnew.astype(BF16))
            o_s[rows, hs] = r1[cc:2 * cc] + r2[0:cc]
            s_s[hh] = st * glm[:, gl:gl + 1] + r2[cc:cc + dk]
        return carry

    lax.fori_loop(0, nch, chunk_b, 0)

    @pl.when(j == nj - 1)
    def _():
        ssm_ref[...] = s_s[...]

    for hh in range(GDN_HEADS):
        hs = slice(hh * dk, (hh + 1) * dk)
        o = o_s[:, hs]
        z = zg_ref[:, hs]
        on = o * lax.rsqrt(jnp.mean(o * o, axis=-1, keepdims=True) + EPS) * onorm_ref[:, hs]
        y_s[:, sgu_w + pool_w + hh * dk:sgu_w + pool_w + (hh + 1) * dk] = (on * (z * _sigmoid(z))).astype(BF16)
    h2_ref[...] = h_ref[...] + _dot(y_s[...], wout_ref[...])


def _mixer(h, a, p, qkv, zg, pool0, conv0, ssm0, lw, ts, c_sgu, pos0):
    b, l, d = h.shape
    sgu_w, pool_w, gdn_w = lw["sgn"].shape[1], lw["pools"].shape[1], lw["onorm"].shape[1]
    dk = gdn_w // GDN_HEADS
    nch = ts // GDN_CHUNK
    mix_w = sgu_w + pool_w + gdn_w
    tile = lambda width: pl.BlockSpec((None, ts, width), lambda i, j: (i, j, 0))
    per_b = lambda *shape: pl.BlockSpec((None,) + shape, lambda i, j: (i,) + (0,) * len(shape))
    res = lambda arr: _resident(arr.shape)
    weights = [lw["sgw"], lw["sgb"], lw["sgn"], lw["avg"], lw["poolw"], lw["pools"],
               lw["convw"], lw["alog"], lw["dtb"], lw["onorm"], lw["wout"]]
    return pl.pallas_call(
        functools.partial(_mixer_kernel, ts=ts, c_sgu=c_sgu, pos0=pos0),
        grid=(b, l // ts),
        in_specs=[tile(2 * sgu_w), tile(pool_w), tile(3 * gdn_w), tile(gdn_w + LANES), tile(d),
                  per_b(POOL_PAD, pool_w), per_b(CONV_PAD, 3 * gdn_w), per_b(GDN_HEADS, dk, dk)]
                 + [res(w) for w in weights],
        out_specs=[tile(d), tile(sgu_w), per_b(POOL_PAD, pool_w), per_b(CONV_PAD, 3 * gdn_w),
                   per_b(GDN_HEADS, dk, dk)],
        out_shape=[jax.ShapeDtypeStruct((b, l, d), F32), jax.ShapeDtypeStruct((b, l, sgu_w), F32),
                   jax.ShapeDtypeStruct((b, POOL_PAD, pool_w), F32),
                   jax.ShapeDtypeStruct((b, CONV_PAD, 3 * gdn_w), F32),
                   jax.ShapeDtypeStruct((b, GDN_HEADS, dk, dk), F32)],
        scratch_shapes=[
            pltpu.VMEM((POOL_PAD + ts, pool_w), F32),
            pltpu.VMEM((CONV_PAD + ts, 3 * gdn_w), F32),
            pltpu.VMEM((ts, gdn_w), F32),
            pltpu.VMEM((ts, gdn_w), F32),
            pltpu.VMEM((ts, gdn_w), F32),
            pltpu.VMEM((ts, LANES), F32),
            pltpu.VMEM((ts, LANES), F32),
            pltpu.VMEM((ts, gdn_w), F32),
            pltpu.VMEM((nch, GDN_HEADS, 2 * GDN_CHUNK, dk), BF16),
            pltpu.VMEM((nch, GDN_HEADS, GDN_CHUNK + dk, GDN_CHUNK), BF16),
            pltpu.VMEM((nch, dk, LANES), F32),
            pltpu.VMEM((ts, gdn_w), F32),
            pltpu.VMEM((ts, mix_w), BF16),
            pltpu.VMEM((GDN_HEADS, dk, dk), F32),
        ],
        compiler_params=pltpu.CompilerParams(dimension_semantics=("parallel", "arbitrary"),
                                             vmem_limit_bytes=VMEM_LIMIT_BYTES),
        name="mixer",
    )(a, p, qkv, zg, h, pool0, conv0, ssm0, *weights)


def _attn_ffn_kernel(h_ref, mk_ref, mv_ref, gx_ref, wq_ref, wo_ref, g2_ref, wg_ref, wu_ref, wd_ref, gf_ref,
                     out_ref, nb_ref, act_ref, ob_ref, *, final):
    h = h_ref[...]
    nb_ref[...] = _rms(h, gx_ref[...]).astype(BF16)
    q = _dot(nb_ref[...], wq_ref[...])
    dh = q.shape[1] // MEM_HEADS
    scale = dh ** -0.5
    for hh in range(MEM_HEADS):
        hs = slice(hh * dh, (hh + 1) * dh)
        s = _dot_nt(q[:, hs].astype(BF16), mk_ref[:, hs].astype(BF16)) * scale
        e = jnp.exp(s - jnp.max(s, axis=-1, keepdims=True))
        pr = e / jnp.sum(e, axis=-1, keepdims=True)
        ob_ref[:, hs] = _dot(pr.astype(BF16), mv_ref[:, hs].astype(BF16)).astype(BF16)
    h = h + _dot(ob_ref[...], wo_ref[...])
    nb_ref[...] = _rms(h, g2_ref[...]).astype(BF16)
    h = h + 0.5 * _swiglu(nb_ref, wg_ref, wu_ref, wd_ref, act_ref)
    if final:
        h = _rms(h, gf_ref[...])
    out_ref[...] = h


def _attn_ffn(h, mk, mv, lw, gf, tm, final):
    b, l, d = h.shape
    d_ff = lw["wg2"].shape[1]
    m, mw = mk.shape[1], mk.shape[2]
    tile = pl.BlockSpec((None, tm, d), lambda i, j: (i, j, 0))
    per_b = pl.BlockSpec((None, m, mw), lambda i, j: (i, 0, 0))
    return pl.pallas_call(
        functools.partial(_attn_ffn_kernel, final=final),
        grid=(b, l // tm),
        in_specs=[tile, per_b, per_b, _resident((1, d)), _resident((d, mw)), _resident((mw, d)),
                  _resident((1, d)), _resident((d, d_ff)), _resident((d, d_ff)), _resident((d_ff, d)),
                  _resident((1, d))],
        out_specs=tile,
        out_shape=jax.ShapeDtypeStruct((b, l, d), F32),
        scratch_shapes=[pltpu.VMEM((tm, d), BF16), pltpu.VMEM((tm, d_ff), BF16), pltpu.VMEM((tm, mw), BF16)],
        compiler_params=pltpu.CompilerParams(dimension_semantics=("parallel", "parallel"),
                                             vmem_limit_bytes=VMEM_LIMIT_BYTES),
        name="attn_ffn",
    )(h, mk, mv, lw["gx"], lw["wmq"], lw["wmo"], lw["g2"], lw["wg2"], lw["wu2"], lw["wd2"], gf)


def _mem_kv_kernel(mem_ref, g_ref, wk_ref, wv_ref, k_ref, v_ref):
    mb = _rms(mem_ref[...], g_ref[...]).astype(BF16)
    k_ref[...] = _dot(mb, wk_ref[...])
    v_ref[...] = _dot(mb, wv_ref[...])


def _mem_kv(mem, g, wk, wv):
    b, m, d = mem.shape
    mw = wk.shape[1]
    blk = lambda width: pl.BlockSpec((None, m, width), lambda i: (i, 0, 0))
    return pl.pallas_call(
        _mem_kv_kernel,
        grid=(b,),
        in_specs=[blk(d), _resident((1, d)), _resident((d, mw)), _resident((d, mw))],
        out_specs=[blk(mw), blk(mw)],
        out_shape=[jax.ShapeDtypeStruct((b, m, mw), F32)] * 2,
        compiler_params=pltpu.CompilerParams(dimension_semantics=("parallel",)),
        name="mem_kv",
    )(mem, g, wk, wv)


def _layer_weights(l, c_sgu, ffn1_norm, ffn1_w_gate, ffn1_w_up, ffn1_w_down, mix_norm, w_in, sgu_norm, sgu_w,
                   sgu_b, pool_w, pool_scale, gdn_conv_w, gdn_a_log, gdn_dt_bias, gdn_out_norm, w_out,
                   xattn_norm, w_mq, w_mo, ffn2_norm, ffn2_w_gate, ffn2_w_up, ffn2_w_down):
    d = w_in.shape[1]
    sgu_width = sgu_norm.shape[1]
    pool_width = pool_scale.shape[1]
    gdn_width = gdn_out_norm.shape[1] * GDN_HEADS
    n_in = w_in.shape[2]
    o_pool = 2 * sgu_width
    o_qkv = o_pool + pool_width
    o_z = o_qkv + 3 * gdn_width
    o_gate = o_z + gdn_width
    n_pad = o_gate + LANES
    row = lambda vec: vec.reshape(1, -1).astype(F32)
    win = jnp.pad(w_in[l], ((0, 0), (0, n_pad - n_in))).astype(BF16)
    groups = len(POOL_WINDOWS)
    gdim = pool_width // groups
    poolw = jnp.zeros((pool_width, pool_width), F32)
    for gi in range(groups):
        poolw = poolw.at[gi * gdim:(gi + 1) * gdim, gi * gdim:(gi + 1) * gdim].set(pool_w[l, gi])
    hd = sgu_width // SGU_HEADS
    seg = jnp.arange(sgu_width) // hd
    avg = jnp.where(seg[:, None] == seg[None, :], 1.0 / hd, 0.0).astype(BF16)
    sgw = jnp.concatenate([sgu_w[l, hh, :c_sgu, :c_sgu] for hh in range(SGU_HEADS)], axis=1)
    sgb = jnp.repeat(sgu_b[l, :, :c_sgu].T, hd, axis=1)
    lane_pad = lambda vec: jnp.pad(vec, (GDN_HEADS, LANES - 2 * GDN_HEADS)).reshape(1, LANES)
    return dict(
        splits=(o_pool, o_qkv, o_z, n_pad),
        g1=row(ffn1_norm[l]), wg1=ffn1_w_gate[l].astype(BF16), wu1=ffn1_w_up[l].astype(BF16),
        wd1=ffn1_w_down[l].astype(BF16), gmix=row(mix_norm[l]), win=win,
        sgw=sgw, sgb=sgb, sgn=row(sgu_norm[l]), avg=avg,
        poolw=poolw.astype(BF16), pools=row(pool_scale[l]),
        convw=gdn_conv_w[l], alog=lane_pad(gdn_a_log[l]), dtb=lane_pad(gdn_dt_bias[l]),
        onorm=row(jnp.tile(gdn_out_norm[l], GDN_HEADS)), wout=w_out[l].astype(BF16),
        gx=row(xattn_norm[l]), wmq=w_mq[l].astype(BF16), wmo=w_mo[l].astype(BF16),
        g2=row(ffn2_norm[l]), wg2=ffn2_w_gate[l].astype(BF16), wu2=ffn2_w_up[l].astype(BF16),
        wd2=ffn2_w_down[l].astype(BF16),
    )


def _layer(x, mk, mv, pool0, conv0, ssm0, lw, gf, final, pos0, tm, ts, tm_attn):
    b, l, d = x.shape
    h, a, p, qkv, zg = _ffn_in(x.reshape(b * l, d), lw, tm)
    shp = lambda arr: arr.reshape(b, l, arr.shape[-1])
    c_sgu = lw["sgb"].shape[0]
    h2, sgu_v, pool_new, conv_new, ssm_new = _mixer(shp(h), shp(a), shp(p), shp(qkv), shp(zg),
                                                    pool0, conv0, ssm0, lw, ts, c_sgu, pos0)
    out = _attn_ffn(h2, mk, mv, lw, gf, tm_attn, final)
    return (out, sgu_v, pool_new[:, POOL_PAD - POOL_HIST:], conv_new[:, CONV_PAD - (GDN_CONV - 1):], ssm_new)


def kernel(x_prompt, x_sample, mem_prompt, cache_mem_k, cache_mem_v, state_pool, state_conv, state_ssm, ffn1_norm, ffn1_w_gate, ffn1_w_up, ffn1_w_down, mix_norm, w_in, sgu_norm, sgu_w, sgu_b, pool_w, pool_scale, gdn_conv_w, gdn_a_log, gdn_dt_bias, gdn_out_norm, w_out, xattn_norm, mem_norm, w_mq, w_mk, w_mv, w_mo, ffn2_norm, ffn2_w_gate, ffn2_w_up, ffn2_w_down, final_norm):
    depth = w_in.shape[0]
    bp, lp, d = x_prompt.shape
    bs, ls, _ = x_sample.shape
    mem_len = mem_prompt.shape[1]
    mem_w = w_mk.shape[2]
    pool_width = state_pool.shape[-1]
    qkv_width = state_conv.shape[-1]
    dk = state_ssm.shape[-1]
    layer_args = (ffn1_norm, ffn1_w_gate, ffn1_w_up, ffn1_w_down, mix_norm, w_in, sgu_norm, sgu_w, sgu_b,
                  pool_w, pool_scale, gdn_conv_w, gdn_a_log, gdn_dt_bias, gdn_out_norm, w_out,
                  xattn_norm, w_mq, w_mo, ffn2_norm, ffn2_w_gate, ffn2_w_up, ffn2_w_down)
    gf = final_norm.reshape(1, d)
    pool0_p = jnp.zeros((bp, POOL_PAD, pool_width), F32)
    conv0_p = jnp.zeros((bp, CONV_PAD, qkv_width), F32)
    ssm0_p = jnp.zeros((bp, GDN_HEADS, dk, dk), F32)
    xp, xs = x_prompt, x_sample
    outs_p = [[] for _ in range(5)]
    outs_s = [[] for _ in range(4)]
    for l in range(depth):
        final = l == depth - 1
        lw_p = _layer_weights(l, min(128, lp), *layer_args)
        lw_s = lw_p if min(128, ls) == min(128, lp) else _layer_weights(l, min(128, ls), *layer_args)
        mk, mv = _mem_kv(mem_prompt, mem_norm[l].reshape(1, d), w_mk[l].astype(BF16), w_mv[l].astype(BF16))
        xp, _, pp, pc, ps = _layer(xp, mk, mv, pool0_p, conv0_p, ssm0_p, lw_p, gf, final, 0,
                                   tm=512, ts=256, tm_attn=512)
        pool0_s = jnp.pad(state_pool[l], ((0, 0), (POOL_PAD - POOL_HIST, 0), (0, 0)))
        conv0_s = jnp.pad(state_conv[l], ((0, 0), (CONV_PAD - (GDN_CONV - 1), 0), (0, 0)))
        xs, sv, sp, sc, ss = _layer(xs, cache_mem_k[l].reshape(bs, mem_len, mem_w),
                                    cache_mem_v[l].reshape(bs, mem_len, mem_w),
                                    pool0_s, conv0_s, state_ssm[l], lw_s, gf, final, PAST_LEN,
                                    tm=bs * ls, ts=ls, tm_attn=ls)
        for lst, val in zip(outs_p, (pp, pc, ps, mk.reshape(bp, mem_len, MEM_HEADS, mem_w // MEM_HEADS),
                                     mv.reshape(bp, mem_len, MEM_HEADS, mem_w // MEM_HEADS))):
            lst.append(val)
        for lst, val in zip(outs_s, (sp, sc, ss, sv)):
            lst.append(val)
    return (xp, xs, *(jnp.stack(v) for v in outs_p), *(jnp.stack(v) for v in outs_s))
```

```python
import functools
import math

import jax
import jax.numpy as jnp
from jax import lax
from jax.experimental import pallas as pl
from jax.experimental.pallas import tpu as pltpu

F32 = jnp.float32
BF16 = jnp.bfloat16

EPS = 1e-6
L2_EPS = 1e-6
LANES = 128
MXU_COLS = 256
VMEM_LIMIT_BYTES = 56 * 1024 * 1024

SGU_HEADS = 4
POOL_WINDOWS = (2, 4, 8, 16)
POOL_HIST = max(POOL_WINDOWS) - 1
POOL_PAD = 16
GDN_HEADS = 4
GDN_CONV = 4
CONV_PAD = 8
GDN_CHUNK = 64
MEM_HEADS = 4
PAST_LEN = 2048
NEG_BIG = -1e30


def _dot(a, b):
    return jnp.dot(a, b, preferred_element_type=F32)


def _dot_nt(a, b):
    return lax.dot_general(a, b, (((1,), (1,)), ((), ())), preferred_element_type=F32)


def _rms(x, g):
    return x * lax.rsqrt(jnp.mean(x * x, axis=-1, keepdims=True) + EPS) * g


def _sigmoid(x):
    return 1.0 / (1.0 + jnp.exp(-x))


def _gelu_tanh(x):
    return 0.5 * x * (1.0 + jnp.tanh(math.sqrt(2.0 / math.pi) * (x + 0.044715 * (x * x * x))))


def _softplus(x):
    return jnp.maximum(x, 0.0) + jnp.log1p(jnp.exp(-jnp.abs(x)))


def _swiglu(nb_ref, wg_ref, wu_ref, wd_ref, act_ref):
    d_ff = wg_ref.shape[1]
    for c in range(d_ff // MXU_COLS):
        sl = slice(c * MXU_COLS, (c + 1) * MXU_COLS)
        g = _dot(nb_ref[...], wg_ref[:, sl])
        u = _dot(nb_ref[...], wu_ref[:, sl])
        act_ref[:, sl] = (g * _sigmoid(g) * u).astype(BF16)
    return _dot(act_ref[...], wd_ref[...])


def _ffn_in_kernel(x_ref, g1_ref, wg_ref, wu_ref, wd_ref, gm_ref, win_ref,
                   h_ref, a_ref, p_ref, qkv_ref, zg_ref, nb_ref, act_ref, *, splits):
    x = x_ref[...]
    nb_ref[...] = _rms(x, g1_ref[...]).astype(BF16)
    h = x + 0.5 * _swiglu(nb_ref, wg_ref, wu_ref, wd_ref, act_ref)
    h_ref[...] = h
    nb_ref[...] = _rms(h, gm_ref[...]).astype(BF16)
    o_pool, o_qkv, o_z, n_pad = splits
    a_ref[...] = _gelu_tanh(_dot(nb_ref[...], win_ref[:, 0:o_pool]))
    p_ref[...] = _dot(nb_ref[...], win_ref[:, o_pool:o_qkv])
    qkv_ref[...] = _dot(nb_ref[...], win_ref[:, o_qkv:o_z])
    zg_ref[...] = _dot(nb_ref[...], win_ref[:, o_z:n_pad])


def _resident(shape):
    nd = len(shape)
    return pl.BlockSpec(shape, lambda *_: (0,) * nd, pipeline_mode=pl.Buffered(1))


def _ffn_in(x2d, lw, tm):
    t, d = x2d.shape
    d_ff = lw["wg1"].shape[1]
    o_pool, o_qkv, o_z, n_pad = lw["splits"]
    row = lambda width: pl.BlockSpec((tm, width), lambda i: (i, 0))
    widths = (d, o_pool, o_qkv - o_pool, o_z - o_qkv, n_pad - o_z)
    return pl.pallas_call(
        functools.partial(_ffn_in_kernel, splits=lw["splits"]),
        grid=(t // tm,),
        in_specs=[row(d), _resident((1, d)), _resident((d, d_ff)), _resident((d, d_ff)),
                  _resident((d_ff, d)), _resident((1, d)), _resident((d, n_pad))],
        out_specs=[row(w) for w in widths],
        out_shape=[jax.ShapeDtypeStruct((t, w), F32) for w in widths],
        scratch_shapes=[pltpu.VMEM((tm, d), BF16), pltpu.VMEM((tm, d_ff), BF16)],
        compiler_params=pltpu.CompilerParams(dimension_semantics=("parallel",),
                                             vmem_limit_bytes=VMEM_LIMIT_BYTES),
        name="ffn_in",
    )(x2d, lw["g1"], lw["wg1"], lw["wu1"], lw["wd1"], lw["gmix"], lw["win"])


def _mixer_kernel(a_ref, p_ref, qkv_ref, zg_ref, h_ref, pool0_ref, conv0_ref, ssm0_ref,
                  sgw_ref, sgb_ref, sgn_ref, avg_ref, poolw_ref, pools_ref,
                  convw_ref, alog_ref, dtb_ref, onorm_ref, wout_ref,
                  h2_ref, sguv_ref, poolst_ref, convst_ref, ssm_ref,
                  pbuf, cbuf, q_s, k_s, v_s, u_s, wq_s, ql_s, o_s, y_s, s_s,
                  *, ts, c_sgu, pos0):
    j = pl.program_id(1)
    nj = pl.num_programs(1)
    sgu_w = sgn_ref.shape[1]
    pool_w = pools_ref.shape[1]
    gdn_w = onorm_ref.shape[1]
    dk = gdn_w // GDN_HEADS
    cc = GDN_CHUNK
    nch = ts // cc

    @pl.when(j == 0)
    def _():
        pbuf[0:POOL_PAD, :] = pool0_ref[...]
        cbuf[0:CONV_PAD, :] = conv0_ref[...]
        s_s[...] = ssm0_ref[...]

    a = a_ref[...]
    v = a[:, sgu_w:]
    avg = avg_ref[...]

    def seg_mean(x):
        hi = x.astype(BF16)
        lo = (x - hi.astype(F32)).astype(BF16)
        return _dot(hi, avg) + _dot(lo, avg)

    vc = v - seg_mean(v)
    vn = vc * lax.rsqrt(seg_mean(vc * vc) + EPS) * sgn_ref[...]
    sguv_ref[...] = vn
    hd = sgu_w // SGU_HEADS
    wi = lax.broadcasted_iota(jnp.int32, (c_sgu, SGU_HEADS * c_sgu), 0)
    wj = lax.broadcasted_iota(jnp.int32, (c_sgu, SGU_HEADS * c_sgu), 1) % c_sgu
    wmask = jnp.where(wi // 64 >= wj // 64, sgw_ref[...], 0.0).astype(BF16)
    lane_head = lax.broadcasted_iota(jnp.int32, (c_sgu, sgu_w), 1) // hd
    for c in range(ts // c_sgu):
        rows = slice(c * c_sgu, (c + 1) * c_sgu)
        vch = vn[rows]
        vstack = jnp.concatenate([jnp.where(lane_head == hh, vch, 0.0) for hh in range(SGU_HEADS)],
                                 axis=0).astype(BF16)
        s = _dot(wmask, vstack) + sgb_ref[...]
        y_s[rows, 0:sgu_w] = (a[rows, 0:sgu_w] * s).astype(BF16)

    p = p_ref[...]
    pbuf[POOL_PAD:POOL_PAD + ts, :] = p
    lane = lax.broadcasted_iota(jnp.int32, (ts, pool_w), 1)
    gdim = pool_w // len(POOL_WINDOWS)
    wlane = jnp.full((ts, pool_w), POOL_WINDOWS[-1], jnp.int32)
    for gi in range(len(POOL_WINDOWS) - 2, -1, -1):
        wlane = jnp.where(lane < (gi + 1) * gdim, POOL_WINDOWS[gi], wlane)
    acc = p
    for jj in range(1, POOL_HIST + 1):
        acc = acc + jnp.where(wlane > jj, pbuf[POOL_PAD - jj:POOL_PAD - jj + ts, :], 0.0)
    pos = pos0 + j * ts + lax.broadcasted_iota(jnp.int32, (ts, pool_w), 0)
    cnt = jnp.minimum(wlane, pos + 1).astype(F32)
    dlt = (acc / cnt - p).astype(BF16)
    y_s[:, sgu_w:sgu_w + pool_w] = (_dot(dlt, poolw_ref[...]) * pools_ref[...]).astype(BF16)
    tail = pbuf[ts:ts + POOL_PAD, :]
    poolst_ref[...] = tail
    pbuf[0:POOL_PAD, :] = tail

    cbuf[CONV_PAD:CONV_PAD + ts, :] = qkv_ref[...]
    qscale = dk ** -0.5
    for cb in range(3 * gdn_w // dk):
        cs = slice(cb * dk, (cb + 1) * dk)
        y = cbuf[CONV_PAD:CONV_PAD + ts, cs] * convw_ref[GDN_CONV - 1:GDN_CONV, cs]
        for t in range(1, GDN_CONV):
            y = y + cbuf[CONV_PAD - t:CONV_PAD - t + ts, cs] * convw_ref[GDN_CONV - 1 - t:GDN_CONV - t, cs]
        y = y * _sigmoid(y)
        which, hh = divmod(cb, GDN_HEADS)
        hs = slice(hh * dk, (hh + 1) * dk)
        if which == 0:
            q_s[:, hs] = y * (lax.rsqrt(jnp.sum(y * y, axis=-1, keepdims=True) + L2_EPS) * qscale)
        elif which == 1:
            k_s[:, hs] = y * lax.rsqrt(jnp.sum(y * y, axis=-1, keepdims=True) + L2_EPS)
        else:
            v_s[:, hs] = y
    ctail = cbuf[ts:ts + CONV_PAD, :]
    convst_ref[...] = ctail
    cbuf[0:CONV_PAD, :] = ctail

    gates = zg_ref[:, gdn_w:gdn_w + LANES]
    beta = _sigmoid(gates)
    g = -jnp.exp(alog_ref[...]) * _softplus(gates + dtb_ref[...])

    ti = lax.broadcasted_iota(jnp.int32, (2 * ts, ts), 0)
    tj = lax.broadcasted_iota(jnp.int32, (2 * ts, ts), 1)
    tr = jnp.where(ti < ts, ti, ti - ts)
    same_chunk = tr // cc == tj // cc
    summat = jnp.where(same_chunk, jnp.where(ti < ts, jnp.where(tj <= tr, 1.0, 0.0), 1.0), 0.0).astype(BF16)
    g_hi = g.astype(BF16)
    g_r = g - g_hi.astype(F32)
    g_mid = g_r.astype(BF16)
    g_lo = (g_r - g_mid.astype(F32)).astype(BF16)
    sums = _dot(summat, g_hi) + _dot(summat, g_mid) + _dot(summat, g_lo)
    gc = sums[0:ts]
    tot = sums[ts:2 * ts]
    egc = jnp.exp(gc)
    erg = jnp.exp(tot - gc)
    etot = jnp.exp(tot)

    ri = lax.broadcasted_iota(jnp.int32, (cc, cc), 0)
    ci = lax.broadcasted_iota(jnp.int32, (cc, cc), 1)
    incl = ri >= ci
    strict = ri > ci
    chains = [(n, hh) for n in range(nch) for hh in range(GDN_HEADS)]
    gct = [gc[n * cc:(n + 1) * cc].T for n in range(nch)]

    nmats = []
    for n, hh in chains:
        rows = slice(n * cc, (n + 1) * cc)
        hs = slice(hh * dk, (hh + 1) * dk)
        gl = GDN_HEADS + hh
        k = k_s[rows, hs]
        kbq = jnp.concatenate([k * beta[rows, hh:hh + 1], q_s[rows, hs]], axis=0).astype(BF16)
        prod = _dot_nt(kbq, k.astype(BF16))
        decay = jnp.exp(jnp.where(incl, gc[rows, gl:gl + 1] - gct[n][gl:gl + 1, :], NEG_BIG))
        nmats.append(jnp.where(strict, -(prod[0:cc] * decay), 0.0))
        ql_s[n, hh, 0:cc, :] = (prod[cc:2 * cc] * decay).astype(BF16)
    tps = list(nmats)
    pows = list(nmats)
    for _ in range(5):
        pows = [_dot(m.astype(BF16), m.astype(BF16)) for m in pows]
        tps = [t + m + _dot(t.astype(BF16), m.astype(BF16)) for t, m in zip(tps, pows)]
    for (n, hh), tp in zip(chains, tps):
        rows = slice(n * cc, (n + 1) * cc)
        hs = slice(hh * dk, (hh + 1) * dk)
        gl = GDN_HEADS + hh
        k = k_s[rows, hs]
        b = beta[rows, hh:hh + 1]
        eg = egc[rows, gl:gl + 1]
        rhs = jnp.concatenate([v_s[rows, hs] * b, k * (b * eg)], axis=1)
        sol = rhs + _dot(tp.astype(BF16), rhs.astype(BF16))
        u_s[rows, hs] = sol[:, 0:dk]
        wq_s[n, hh, 0:cc, :] = sol[:, dk:2 * dk].astype(BF16)
        wq_s[n, hh, cc:2 * cc, :] = (q_s[rows, hs] * eg).astype(BF16)
        ql_s[n, hh, cc:cc + dk, :] = (k * erg[rows, gl:gl + 1]).T.astype(BF16)

    for n in range(nch):
        rows = slice(n * cc, (n + 1) * cc)
        sts = [s_s[hh] for hh in range(GDN_HEADS)]
        r1 = [_dot(wq_s[n, hh], sts[hh].astype(BF16)) for hh in range(GDN_HEADS)]
        vnew = [u_s[rows, hh * dk:(hh + 1) * dk] - r1[hh][0:cc] for hh in range(GDN_HEADS)]
        r2 = [_dot(ql_s[n, hh], vnew[hh].astype(BF16)) for hh in range(GDN_HEADS)]
        for hh in range(GDN_HEADS):
            gl = GDN_HEADS + hh
            o_s[rows, hh * dk:(hh + 1) * dk] = r1[hh][cc:2 * cc] + r2[hh][0:cc]
            glcol = etot[rows, gl:gl + 1]
            s_s[hh] = sts[hh] * jnp.concatenate([glcol] * (dk // cc), axis=0) + r2[hh][cc:cc + dk]

    @pl.when(j == nj - 1)
    def _():
        ssm_ref[...] = s_s[...]

    for hh in range(GDN_HEADS):
        hs = slice(hh * dk, (hh + 1) * dk)
        o = o_s[:, hs]
        z = zg_ref[:, hs]
        on = o * lax.rsqrt(jnp.mean(o * o, axis=-1, keepdims=True) + EPS) * onorm_ref[:, hs]
        y_s[:, sgu_w + pool_w + hh * dk:sgu_w + pool_w + (hh + 1) * dk] = (on * (z * _sigmoid(z))).astype(BF16)
    h2_ref[...] = h_ref[...] + _dot(y_s[...], wout_ref[...])


def _mixer(h, a, p, qkv, zg, pool0, conv0, ssm0, lw, ts, c_sgu, pos0):
    b, l, d = h.shape
    sgu_w, pool_w, gdn_w = lw["sgn"].shape[1], lw["pools"].shape[1], lw["onorm"].shape[1]
    dk = gdn_w // GDN_HEADS
    nch = ts // GDN_CHUNK
    mix_w = sgu_w + pool_w + gdn_w
    tile = lambda width: pl.BlockSpec((None, ts, width), lambda i, j: (i, j, 0))
    per_b = lambda *shape: pl.BlockSpec((None,) + shape, lambda i, j: (i,) + (0,) * len(shape))
    res = lambda arr: _resident(arr.shape)
    weights = [lw["sgw"], lw["sgb"], lw["sgn"], lw["avg"], lw["poolw"], lw["pools"],
               lw["convw"], lw["alog"], lw["dtb"], lw["onorm"], lw["wout"]]
    return pl.pallas_call(
        functools.partial(_mixer_kernel, ts=ts, c_sgu=c_sgu, pos0=pos0),
        grid=(b, l // ts),
        in_specs=[tile(2 * sgu_w), tile(pool_w), tile(3 * gdn_w), tile(gdn_w + LANES), tile(d),
                  per_b(POOL_PAD, pool_w), per_b(CONV_PAD, 3 * gdn_w), per_b(GDN_HEADS, dk, dk)]
                 + [res(w) for w in weights],
        out_specs=[tile(d), tile(sgu_w), per_b(POOL_PAD, pool_w), per_b(CONV_PAD, 3 * gdn_w),
                   per_b(GDN_HEADS, dk, dk)],
        out_shape=[jax.ShapeDtypeStruct((b, l, d), F32), jax.ShapeDtypeStruct((b, l, sgu_w), F32),
                   jax.ShapeDtypeStruct((b, POOL_PAD, pool_w), F32),
                   jax.ShapeDtypeStruct((b, CONV_PAD, 3 * gdn_w), F32),
                   jax.ShapeDtypeStruct((b, GDN_HEADS, dk, dk), F32)],
        scratch_shapes=[
            pltpu.VMEM((POOL_PAD + ts, pool_w), F32),
            pltpu.VMEM((CONV_PAD + ts, 3 * gdn_w), F32),
            pltpu.VMEM((ts, gdn_w), F32),
            pltpu.VMEM((ts, gdn_w), F32),
            pltpu.VMEM((ts, gdn_w), F32),
            pltpu.VMEM((ts, gdn_w), F32),
            pltpu.VMEM((nch, GDN_HEADS, 2 * GDN_CHUNK, dk), BF16),
            pltpu.VMEM((nch, GDN_HEADS, GDN_CHUNK + dk, GDN_CHUNK), BF16),
            pltpu.VMEM((ts, gdn_w), F32),
            pltpu.VMEM((ts, mix_w), BF16),
            pltpu.VMEM((GDN_HEADS, dk, dk), F32),
        ],
        compiler_params=pltpu.CompilerParams(dimension_semantics=("parallel", "arbitrary"),
                                             vmem_limit_bytes=VMEM_LIMIT_BYTES),
        name="mixer",
    )(a, p, qkv, zg, h, pool0, conv0, ssm0, *weights)


def _attn_ffn_kernel(h_ref, mk_ref, mv_ref, gx_ref, wq_ref, wo_ref, g2_ref, wg_ref, wu_ref, wd_ref, gf_ref,
                     out_ref, nb_ref, act_ref, ob_ref, *, final):
    h = h_ref[...]
    nb_ref[...] = _rms(h, gx_ref[...]).astype(BF16)
    q = _dot(nb_ref[...], wq_ref[...])
    dh = q.shape[1] // MEM_HEADS
    scale = dh ** -0.5
    for hh in range(MEM_HEADS):
        hs = slice(hh * dh, (hh + 1) * dh)
        s = _dot_nt(q[:, hs].astype(BF16), mk_ref[:, hs].astype(BF16)) * scale
        e = jnp.exp(s - jnp.max(s, axis=-1, keepdims=True))
        pr = e / jnp.sum(e, axis=-1, keepdims=True)
        ob_ref[:, hs] = _dot(pr.astype(BF16), mv_ref[:, hs].astype(BF16)).astype(BF16)
    h = h + _dot(ob_ref[...], wo_ref[...])
    nb_ref[...] = _rms(h, g2_ref[...]).astype(BF16)
    h = h + 0.5 * _swiglu(nb_ref, wg_ref, wu_ref, wd_ref, act_ref)
    if final:
        h = _rms(h, gf_ref[...])
    out_ref[...] = h


def _attn_ffn(h, mk, mv, lw, gf, tm, final):
    b, l, d = h.shape
    d_ff = lw["wg2"].shape[1]
    m, mw = mk.shape[1], mk.shape[2]
    tile = pl.BlockSpec((None, tm, d), lambda i, j: (i, j, 0))
    per_b = pl.BlockSpec((None, m, mw), lambda i, j: (i, 0, 0))
    return pl.pallas_call(
        functools.partial(_attn_ffn_kernel, final=final),
        grid=(b, l // tm),
        in_specs=[tile, per_b, per_b, _resident((1, d)), _resident((d, mw)), _resident((mw, d)),
                  _resident((1, d)), _resident((d, d_ff)), _resident((d, d_ff)), _resident((d_ff, d)),
                  _resident((1, d))],
        out_specs=tile,
        out_shape=jax.ShapeDtypeStruct((b, l, d), F32),
        scratch_shapes=[pltpu.VMEM((tm, d), BF16), pltpu.VMEM((tm, d_ff), BF16), pltpu.VMEM((tm, mw), BF16)],
        compiler_params=pltpu.CompilerParams(dimension_semantics=("parallel", "parallel"),
                                             vmem_limit_bytes=VMEM_LIMIT_BYTES),
        name="attn_ffn",
    )(h, mk, mv, lw["gx"], lw["wmq"], lw["wmo"], lw["g2"], lw["wg2"], lw["wu2"], lw["wd2"], gf)


def _mem_kv_kernel(mem_ref, g_ref, wk_ref, wv_ref, k_ref, v_ref):
    mb = _rms(mem_ref[...], g_ref[...]).astype(BF16)
    k_ref[...] = _dot(mb, wk_ref[...])
    v_ref[...] = _dot(mb, wv_ref[...])


def _mem_kv(mem, g, wk, wv):
    b, m, d = mem.shape
    mw = wk.shape[1]
    blk = lambda width: pl.BlockSpec((None, m, width), lambda i: (i, 0, 0))
    return pl.pallas_call(
        _mem_kv_kernel,
        grid=(b,),
        in_specs=[blk(d), _resident((1, d)), _resident((d, mw)), _resident((d, mw))],
        out_specs=[blk(mw), blk(mw)],
        out_shape=[jax.ShapeDtypeStruct((b, m, mw), F32)] * 2,
        compiler_params=pltpu.CompilerParams(dimension_semantics=("parallel",)),
        name="mem_kv",
    )(mem, g, wk, wv)


def _layer_weights(l, c_sgu, ffn1_norm, ffn1_w_gate, ffn1_w_up, ffn1_w_down, mix_norm, w_in, sgu_norm, sgu_w,
                   sgu_b, pool_w, pool_scale, gdn_conv_w, gdn_a_log, gdn_dt_bias, gdn_out_norm, w_out,
                   xattn_norm, w_mq, w_mo, ffn2_norm, ffn2_w_gate, ffn2_w_up, ffn2_w_down):
    d = w_in.shape[1]
    sgu_width = sgu_norm.shape[1]
    pool_width = pool_scale.shape[1]
    gdn_width = gdn_out_norm.shape[1] * GDN_HEADS
    n_in = w_in.shape[2]
    o_pool = 2 * sgu_width
    o_qkv = o_pool + pool_width
    o_z = o_qkv + 3 * gdn_width
    o_gate = o_z + gdn_width
    n_pad = o_gate + LANES
    row = lambda vec: vec.reshape(1, -1).astype(F32)
    win = jnp.pad(w_in[l], ((0, 0), (0, n_pad - n_in))).astype(BF16)
    groups = len(POOL_WINDOWS)
    gdim = pool_width // groups
    poolw = jnp.zeros((pool_width, pool_width), F32)
    for gi in range(groups):
        poolw = poolw.at[gi * gdim:(gi + 1) * gdim, gi * gdim:(gi + 1) * gdim].set(pool_w[l, gi])
    hd = sgu_width // SGU_HEADS
    seg = jnp.arange(sgu_width) // hd
    avg = jnp.where(seg[:, None] == seg[None, :], 1.0 / hd, 0.0).astype(BF16)
    sgw = jnp.concatenate([sgu_w[l, hh, :c_sgu, :c_sgu] for hh in range(SGU_HEADS)], axis=1)
    sgb = jnp.repeat(sgu_b[l, :, :c_sgu].T, hd, axis=1)
    lane_pad = lambda vec: jnp.pad(vec, (GDN_HEADS, LANES - 2 * GDN_HEADS)).reshape(1, LANES)
    return dict(
        splits=(o_pool, o_qkv, o_z, n_pad),
        g1=row(ffn1_norm[l]), wg1=ffn1_w_gate[l].astype(BF16), wu1=ffn1_w_up[l].astype(BF16),
        wd1=ffn1_w_down[l].astype(BF16), gmix=row(mix_norm[l]), win=win,
        sgw=sgw, sgb=sgb, sgn=row(sgu_norm[l]), avg=avg,
        poolw=poolw.astype(BF16), pools=row(pool_scale[l]),
        convw=gdn_conv_w[l], alog=lane_pad(gdn_a_log[l]), dtb=lane_pad(gdn_dt_bias[l]),
        onorm=row(jnp.tile(gdn_out_norm[l], GDN_HEADS)), wout=w_out[l].astype(BF16),
        gx=row(xattn_norm[l]), wmq=w_mq[l].astype(BF16), wmo=w_mo[l].astype(BF16),
        g2=row(ffn2_norm[l]), wg2=ffn2_w_gate[l].astype(BF16), wu2=ffn2_w_up[l].astype(BF16),
        wd2=ffn2_w_down[l].astype(BF16),
    )


def _layer(x, mk, mv, pool0, conv0, ssm0, lw, gf, final, pos0, tm, ts, tm_attn):
    b, l, d = x.shape
    h, a, p, qkv, zg = _ffn_in(x.reshape(b * l, d), lw, tm)
    shp = lambda arr: arr.reshape(b, l, arr.shape[-1])
    c_sgu = lw["sgb"].shape[0]
    h2, sgu_v, pool_new, conv_new, ssm_new = _mixer(shp(h), shp(a), shp(p), shp(qkv), shp(zg),
                                                    pool0, conv0, ssm0, lw, ts, c_sgu, pos0)
    out = _attn_ffn(h2, mk, mv, lw, gf, tm_attn, final)
    return (out, sgu_v, pool_new[:, POOL_PAD - POOL_HIST:], conv_new[:, CONV_PAD - (GDN_CONV - 1):], ssm_new)


def kernel(x_prompt, x_sample, mem_prompt, cache_mem_k, cache_mem_v, state_pool, state_conv, state_ssm, ffn1_norm, ffn1_w_gate, ffn1_w_up, ffn1_w_down, mix_norm, w_in, sgu_norm, sgu_w, sgu_b, pool_w, pool_scale, gdn_conv_w, gdn_a_log, gdn_dt_bias, gdn_out_norm, w_out, xattn_norm, mem_norm, w_mq, w_mk, w_mv, w_mo, ffn2_norm, ffn2_w_gate, ffn2_w_up, ffn2_w_down, final_norm):
    depth = w_in.shape[0]
    bp, lp, d = x_prompt.shape
    bs, ls, _ = x_sample.shape
    mem_len = mem_prompt.shape[1]
    mem_w = w_mk.shape[2]
    pool_width = state_pool.shape[-1]
    qkv_width = state_conv.shape[-1]
    dk = state_ssm.shape[-1]
    layer_args = (ffn1_norm, ffn1_w_gate, ffn1_w_up, ffn1_w_down, mix_norm, w_in, sgu_norm, sgu_w, sgu_b,
                  pool_w, pool_scale, gdn_conv_w, gdn_a_log, gdn_dt_bias, gdn_out_norm, w_out,
                  xattn_norm, w_mq, w_mo, ffn2_norm, ffn2_w_gate, ffn2_w_up, ffn2_w_down)
    gf = final_norm.reshape(1, d)
    pool0_p = jnp.zeros((bp, POOL_PAD, pool_width), F32)
    conv0_p = jnp.zeros((bp, CONV_PAD, qkv_width), F32)
    ssm0_p = jnp.zeros((bp, GDN_HEADS, dk, dk), F32)
    xp, xs = x_prompt, x_sample
    outs_p = [[] for _ in range(5)]
    outs_s = [[] for _ in range(4)]
    for l in range(depth):
        final = l == depth - 1
        lw_p = _layer_weights(l, min(128, lp), *layer_args)
        lw_s = lw_p if min(128, ls) == min(128, lp) else _layer_weights(l, min(128, ls), *layer_args)
        mk, mv = _mem_kv(mem_prompt, mem_norm[l].reshape(1, d), w_mk[l].astype(BF16), w_mv[l].astype(BF16))
        xp, _, pp, pc, ps = _layer(xp, mk, mv, pool0_p, conv0_p, ssm0_p, lw_p, gf, final, 0,
                                   tm=512, ts=256, tm_attn=512)
        pool0_s = jnp.pad(state_pool[l], ((0, 0), (POOL_PAD - POOL_HIST, 0), (0, 0)))
        conv0_s = jnp.pad(state_conv[l], ((0, 0), (CONV_PAD - (GDN_CONV - 1), 0), (0, 0)))
        xs, sv, sp, sc, ss = _layer(xs, cache_mem_k[l].reshape(bs, mem_len, mem_w),
                                    cache_mem_v[l].reshape(bs, mem_len, mem_w),
                                    pool0_s, conv0_s, state_ssm[l], lw_s, gf, final, PAST_LEN,
                                    tm=bs * ls, ts=ls, tm_attn=ls)
        for lst, val in zip(outs_p, (pp, pc, ps, mk.reshape(bp, mem_len, MEM_HEADS, mem_w // MEM_HEADS),
                                     mv.reshape(bp, mem_len, MEM_HEADS, mem_w // MEM_HEADS))):
            lst.append(val)
        for lst, val in zip(outs_s, (sp, sc, ss, sv)):
            lst.append(val)
    return (xp, xs, *(jnp.stack(v) for v in outs_p), *(jnp.stack(v) for v in outs_s))
```

```python
import functools
import math

import jax
import jax.numpy as jnp
from jax import lax
from jax.experimental import pallas as pl
from jax.experimental.pallas import tpu as pltpu

F32 = jnp.float32
BF16 = jnp.bfloat16

EPS = 1e-6
L2_EPS = 1e-6
LANES = 128
MXU_COLS = 256
VMEM_LIMIT_BYTES = 58 * 1024 * 1024

SGU_HEADS = 4
SGU_BLOCK = 64
POOL_WINDOWS = (2, 4, 8, 16)
POOL_HIST = max(POOL_WINDOWS) - 1
POOL_PAD = 16
GDN_HEADS = 4
GDN_CONV = 4
CONV_PAD = 8
GDN_CHUNK = 64
MEM_HEADS = 4
PAST_LEN = 2048
NEG_BIG = -1e30


def _dot(a, b):
    return jnp.dot(a, b, preferred_element_type=F32)


def _dot_nt(a, b):
    return lax.dot_general(a, b, (((1,), (1,)), ((), ())), preferred_element_type=F32)


def _rms(x, g):
    return x * lax.rsqrt(jnp.mean(x * x, axis=-1, keepdims=True) + EPS) * g


def _sigmoid(x):
    return 1.0 / (1.0 + jnp.exp(-x))


def _gelu_tanh(x):
    return 0.5 * x * (1.0 + jnp.tanh(math.sqrt(2.0 / math.pi) * (x + 0.044715 * (x * x * x))))


def _softplus(x):
    return jnp.maximum(x, 0.0) + jnp.log1p(jnp.exp(-jnp.abs(x)))


def _swiglu(nb_ref, wg_ref, wu_ref, wd_ref, act_ref):
    d_ff = wg_ref.shape[1]
    for c in range(d_ff // MXU_COLS):
        sl = slice(c * MXU_COLS, (c + 1) * MXU_COLS)
        g = _dot(nb_ref[...], wg_ref[:, sl])
        u = _dot(nb_ref[...], wu_ref[:, sl])
        act_ref[:, sl] = (g * _sigmoid(g) * u).astype(BF16)
    return _dot(act_ref[...], wd_ref[...])


def _ffn_in_kernel(x_ref, pool0_ref, conv0_ref, g1_ref, wg_ref, wu_ref, wd_ref, gm_ref, win_ref,
                   sgn_ref, avg_ref, poolw_ref, pools_ref, convw_ref,
                   h_ref, zg_ref, ug_ref, vn_ref, yb_ref, q_ref, k_ref, v_ref, poolst_ref, convst_ref,
                   nb_ref, act_ref, uv_s, pbuf, cbuf, *, splits, sl, tps, pos0):
    i = pl.program_id(0)
    tm = x_ref.shape[0]
    nseq = tm // sl
    sgu_w = sgn_ref.shape[1]
    pool_w = pools_ref.shape[1]
    gdn_w = q_ref.shape[1]
    dk = gdn_w // GDN_HEADS
    o_pool, o_qkv, o_z, n_pad = splits
    iv = i

    @pl.when(i % tps == 0)
    def _():
        pbuf[:, 0:POOL_PAD, :] = pool0_ref[...]
        cbuf[:, 0:CONV_PAD, :] = conv0_ref[...]

    def matmul_stage():
        x = x_ref[...]
        nb_ref[...] = _rms(x, g1_ref[...]).astype(BF16)
        h = x + 0.5 * _swiglu(nb_ref, wg_ref, wu_ref, wd_ref, act_ref)
        h_ref[...] = h
        nb_ref[...] = _rms(h, gm_ref[...]).astype(BF16)
        uv_s[...] = _dot(nb_ref[...], win_ref[:, 0:o_pool])
        zg_ref[...] = _dot(nb_ref[...], win_ref[:, o_z:n_pad])
        p = _dot(nb_ref[...], win_ref[:, o_pool:o_qkv])
        qkv = _dot(nb_ref[...], win_ref[:, o_qkv:o_z])
        for s in range(nseq):
            pbuf[s, POOL_PAD:POOL_PAD + sl, :] = p[s * sl:(s + 1) * sl]
            cbuf[s, CONV_PAD:CONV_PAD + sl, :] = qkv[s * sl:(s + 1) * sl]

    def vector_stage():
        uv = _gelu_tanh(uv_s[...])
        ug_ref[...] = uv[:, 0:sgu_w]
        v = uv[:, sgu_w:]
        avg = avg_ref[...]

        def seg_mean(t):
            hi = t.astype(BF16)
            lo = (t - hi.astype(F32)).astype(BF16)
            return _dot(hi, avg) + _dot(lo, avg)

        vc = v - seg_mean(v)
        vn_ref[...] = vc * lax.rsqrt(seg_mean(vc * vc) + EPS) * sgn_ref[...]

        gdim = pool_w // len(POOL_WINDOWS)
        lane = lax.broadcasted_iota(jnp.int32, (sl, LANES), 1)
        pos = pos0 + (iv % tps) * sl + lax.broadcasted_iota(jnp.int32, (sl, LANES), 0)
        dparts = []
        for s in range(nseq):
            dblk = []
            for blk in range(pool_w // LANES):
                ls = slice(blk * LANES, (blk + 1) * LANES)
                wins = [POOL_WINDOWS[(blk * LANES + l0) // gdim] for l0 in range(0, LANES, gdim)]
                wl = jnp.full((sl, LANES), wins[-1], jnp.int32)
                for gi in range(len(wins) - 2, -1, -1):
                    wl = jnp.where(lane < (gi + 1) * gdim, wins[gi], wl)
                pb = pbuf[s, POOL_PAD:POOL_PAD + sl, ls]
                acc = pb
                prev = 1
                for w in sorted(set(wins)):
                    part = None
                    for jj in range(prev, w):
                        sh = pbuf[s, POOL_PAD - jj:POOL_PAD - jj + sl, ls]
                        part = sh if part is None else part + sh
                    if part is not None:
                        acc = acc + (part if w == min(wins) else jnp.where(wl >= w, part, 0.0))
                    prev = w
                cnt = jnp.minimum(wl, pos + 1).astype(F32)
                dblk.append(acc / cnt - pb)
            dparts.append(jnp.concatenate(dblk, axis=1))
            tail = pbuf[s, sl:sl + POOL_PAD, :]
            poolst_ref[s] = tail
            pbuf[s, 0:POOL_PAD, :] = tail
        dlt = (dparts[0] if nseq == 1 else jnp.concatenate(dparts, axis=0)).astype(BF16)
        yb_ref[...] = (_dot(dlt, poolw_ref[...]) * pools_ref[...]).astype(BF16)

        qscale = dk ** -0.5
        for cb in range(3 * gdn_w // dk):
            cs = slice(cb * dk, (cb + 1) * dk)
            which, hh = divmod(cb, GDN_HEADS)
            hs = slice(hh * dk, (hh + 1) * dk)
            for s in range(nseq):
                rows = slice(s * sl, (s + 1) * sl)
                y = cbuf[s, CONV_PAD:CONV_PAD + sl, cs] * convw_ref[GDN_CONV - 1:GDN_CONV, cs]
                for t in range(1, GDN_CONV):
                    y = y + (cbuf[s, CONV_PAD - t:CONV_PAD - t + sl, cs]
                             * convw_ref[GDN_CONV - 1 - t:GDN_CONV - t, cs])
                y = y * _sigmoid(y)
                if which == 0:
                    q_ref[rows, hs] = y * (lax.rsqrt(jnp.sum(y * y, axis=-1, keepdims=True) + L2_EPS) * qscale)
                elif which == 1:
                    k_ref[rows, hs] = y * lax.rsqrt(jnp.sum(y * y, axis=-1, keepdims=True) + L2_EPS)
                else:
                    v_ref[rows, hs] = y
        for s in range(nseq):
            ctail = cbuf[s, sl:sl + CONV_PAD, :]
            convst_ref[s] = ctail
            cbuf[s, 0:CONV_PAD, :] = ctail

    matmul_stage()
    vector_stage()


def _res(w):
    if isinstance(w, tuple):
        arr, l = w
        nd = arr.ndim - 1
        return pl.BlockSpec((None,) + arr.shape[1:], lambda *_: (l,) + (0,) * nd, pipeline_mode=pl.Buffered(1))
    nd = w.ndim
    return pl.BlockSpec(w.shape, lambda *_: (0,) * nd, pipeline_mode=pl.Buffered(1))


def _arr(w):
    return w[0] if isinstance(w, tuple) else w


def _cast_kernel(x_ref, o_ref):
    w = x_ref.shape[-1]
    o_ref[:, 0:w] = x_ref[...].astype(BF16)
    if o_ref.shape[-1] > w:
        o_ref[:, w:] = jnp.zeros((o_ref.shape[0], o_ref.shape[-1] - w), BF16)


def _cast_bf16(w, cols=None, block_rows=256):
    depth, rows, c = w.shape
    cols = c if cols is None else cols
    br = min(block_rows, rows)
    return pl.pallas_call(
        _cast_kernel,
        grid=(depth, rows // br),
        in_specs=[pl.BlockSpec((None, br, c), lambda l, i: (l, i, 0))],
        out_specs=pl.BlockSpec((None, br, cols), lambda l, i: (l, i, 0)),
        out_shape=jax.ShapeDtypeStruct((depth, rows, cols), BF16),
        compiler_params=pltpu.CompilerParams(dimension_semantics=("parallel", "parallel")),
        name="cast_bf16",
    )(w)


def _ffn_in(x2d, pool0, conv0, lw, tm, seq_len, pos0):
    t, d = x2d.shape
    d_ff = _arr(lw["wg1"]).shape[-1]
    o_pool, o_qkv, o_z, n_pad = lw["splits"]
    sgu_w, pool_w, gdn_w = lw["sgn"].shape[1], lw["pools"].shape[1], lw["onorm"].shape[1]
    sl = min(seq_len, tm)
    nseq = tm // sl
    tps = seq_len // sl
    row = lambda width: pl.BlockSpec((tm, width), lambda i: (i, 0))
    hist = lambda pad, width: pl.BlockSpec((nseq, pad, width), lambda i: (i // tps, 0, 0))
    res = _res
    weights = [lw["g1"], lw["wg1"], lw["wu1"], lw["wd1"], lw["gmix"], lw["win"],
               lw["sgn"], lw["avg"], lw["poolw"], lw["pools"], lw["convw"]]
    outs = [(d, F32), (n_pad - o_z, F32), (sgu_w, F32), (sgu_w, F32), (pool_w, BF16),
            (gdn_w, F32), (gdn_w, F32), (gdn_w, F32)]
    nbatch = t // seq_len
    return pl.pallas_call(
        functools.partial(_ffn_in_kernel, splits=lw["splits"], sl=sl, tps=tps, pos0=pos0),
        grid=(t // tm,),
        in_specs=[row(d), hist(POOL_PAD, pool_w), hist(CONV_PAD, 3 * gdn_w)] + [res(w) for w in weights],
        out_specs=[row(w) for w, _ in outs] + [hist(POOL_PAD, pool_w), hist(CONV_PAD, 3 * gdn_w)],
        out_shape=[jax.ShapeDtypeStruct((t, w), dt) for w, dt in outs]
                  + [jax.ShapeDtypeStruct((nbatch, POOL_PAD, pool_w), F32),
                     jax.ShapeDtypeStruct((nbatch, CONV_PAD, 3 * gdn_w), F32)],
        scratch_shapes=[pltpu.VMEM((tm, d), BF16), pltpu.VMEM((tm, d_ff), BF16),
                        pltpu.VMEM((tm, o_pool), F32),
                        pltpu.VMEM((nseq, POOL_PAD + sl, pool_w), F32),
                        pltpu.VMEM((nseq, CONV_PAD + sl, 3 * gdn_w), F32)],
        compiler_params=pltpu.CompilerParams(dimension_semantics=("arbitrary",),
                                             vmem_limit_bytes=VMEM_LIMIT_BYTES),
        name="ffn_in",
    )(x2d, pool0, conv0, *(_arr(w) for w in weights))


def _mixer_kernel(ug_ref, vn_ref, yb_ref, q_ref, k_ref, v_ref, zg_ref, h_ref, ssm0_ref,
                  sgw_ref, sgb_ref, alog_ref, dtb_ref, onorm_ref, wout_ref,
                  h2_ref, ssm_ref,
                  u_s, wq_s, ql_s, o_s, y_s, s_s, *, ts, c_sgu):
    j = pl.program_id(1)
    nj = pl.num_programs(1)
    sgu_w = vn_ref.shape[1]
    pool_w = yb_ref.shape[1]
    gdn_w = onorm_ref.shape[1]
    dk = gdn_w // GDN_HEADS
    cc = GDN_CHUNK
    nch = ts // cc

    @pl.when(j == 0)
    def _():
        s_s[...] = ssm0_ref[...]

    hd = sgu_w // SGU_HEADS
    wi = lax.broadcasted_iota(jnp.int32, (c_sgu, SGU_HEADS * c_sgu), 0)
    wj = lax.broadcasted_iota(jnp.int32, (c_sgu, SGU_HEADS * c_sgu), 1) % c_sgu
    wmask = jnp.where(wi // SGU_BLOCK >= wj // SGU_BLOCK, sgw_ref[...], 0.0).astype(BF16)
    lane_head = lax.broadcasted_iota(jnp.int32, (c_sgu, sgu_w), 1) // hd
    for c in range(ts // c_sgu):
        rows = slice(c * c_sgu, (c + 1) * c_sgu)
        vch = vn_ref[rows, :]
        vstack = jnp.concatenate([jnp.where(lane_head == hh, vch, 0.0) for hh in range(SGU_HEADS)],
                                 axis=0).astype(BF16)
        s = _dot(wmask, vstack) + sgb_ref[...]
        y_s[rows, 0:sgu_w] = (ug_ref[rows, :] * s).astype(BF16)
    y_s[:, sgu_w:sgu_w + pool_w] = yb_ref[...]

    gates = zg_ref[:, gdn_w:gdn_w + LANES]
    beta = _sigmoid(gates)
    g = -jnp.exp(alog_ref[...]) * _softplus(gates + dtb_ref[...])

    ti = lax.broadcasted_iota(jnp.int32, (2 * ts, ts), 0)
    tj = lax.broadcasted_iota(jnp.int32, (2 * ts, ts), 1)
    tr = jnp.where(ti < ts, ti, ti - ts)
    same_chunk = tr // cc == tj // cc
    summat = jnp.where(same_chunk, jnp.where(ti < ts, jnp.where(tj <= tr, 1.0, 0.0), 1.0), 0.0).astype(BF16)
    g_hi = g.astype(BF16)
    g_r = g - g_hi.astype(F32)
    g_mid = g_r.astype(BF16)
    g_lo = (g_r - g_mid.astype(F32)).astype(BF16)
    sums = _dot(summat, g_hi) + _dot(summat, g_mid) + _dot(summat, g_lo)
    gc = sums[0:ts]
    tot = sums[ts:2 * ts]
    egc = jnp.exp(gc)
    erg = jnp.exp(tot - gc)
    etot = jnp.exp(tot)

    ri = lax.broadcasted_iota(jnp.int32, (cc, cc), 0)
    ci = lax.broadcasted_iota(jnp.int32, (cc, cc), 1)
    incl = ri >= ci
    strict = ri > ci
    chains = [(n, hh) for n in range(nch) for hh in range(GDN_HEADS)]
    gct = [gc[n * cc:(n + 1) * cc].T for n in range(nch)]

    nmats = []
    for n, hh in chains:
        rows = slice(n * cc, (n + 1) * cc)
        hs = slice(hh * dk, (hh + 1) * dk)
        gl = GDN_HEADS + hh
        k = k_ref[rows, hs]
        kbq = jnp.concatenate([k * beta[rows, hh:hh + 1], q_ref[rows, hs]], axis=0).astype(BF16)
        prod = _dot_nt(kbq, k.astype(BF16))
        decay = jnp.exp(jnp.where(incl, gc[rows, gl:gl + 1] - gct[n][gl:gl + 1, :], NEG_BIG))
        nmats.append(jnp.where(strict, -(prod[0:cc] * decay), 0.0))
        ql_s[n, hh, 0:cc, :] = (prod[cc:2 * cc] * decay).astype(BF16)
    tps = list(nmats)
    pows = list(nmats)
    for _ in range(5):
        pows = [_dot(m.astype(BF16), m.astype(BF16)) for m in pows]
        tps = [t + m + _dot(t.astype(BF16), m.astype(BF16)) for t, m in zip(tps, pows)]
    for (n, hh), tp in zip(chains, tps):
        rows = slice(n * cc, (n + 1) * cc)
        hs = slice(hh * dk, (hh + 1) * dk)
        gl = GDN_HEADS + hh
        k = k_ref[rows, hs]
        b = beta[rows, hh:hh + 1]
        eg = egc[rows, gl:gl + 1]
        rhs = jnp.concatenate([v_ref[rows, hs] * b, k * (b * eg)], axis=1)
        sol = rhs + _dot(tp.astype(BF16), rhs.astype(BF16))
        u_s[rows, hs] = sol[:, 0:dk]
        wq_s[n, hh, 0:cc, :] = sol[:, dk:2 * dk].astype(BF16)
        wq_s[n, hh, cc:2 * cc, :] = (q_ref[rows, hs] * eg).astype(BF16)
        ql_s[n, hh, cc:cc + dk, :] = (k * erg[rows, gl:gl + 1]).T.astype(BF16)

    pre_w = sgu_w + pool_w
    fill = []
    for kb in range(pre_w // MXU_COLS):
        for nb in range(h_ref.shape[1] // MXU_COLS):
            fill.append((slice(kb * MXU_COLS, (kb + 1) * MXU_COLS), slice(nb * MXU_COLS, (nb + 1) * MXU_COLS), kb == 0))
    gaps = 2 * nch

    def run_fill(gap):
        for ks, ns, first in fill[gap * len(fill) // gaps:(gap + 1) * len(fill) // gaps]:
            part = _dot(y_s[:, ks], wout_ref[ks, ns])
            h2_ref[:, ns] = (h_ref[:, ns] if first else h2_ref[:, ns]) + part

    for n in range(nch):
        rows = slice(n * cc, (n + 1) * cc)
        sts = [s_s[hh] for hh in range(GDN_HEADS)]
        r1 = [_dot(wq_s[n, hh], sts[hh].astype(BF16)) for hh in range(GDN_HEADS)]
        run_fill(2 * n)
        vnew = [u_s[rows, hh * dk:(hh + 1) * dk] - r1[hh][0:cc] for hh in range(GDN_HEADS)]
        r2 = [_dot(ql_s[n, hh], vnew[hh].astype(BF16)) for hh in range(GDN_HEADS)]
        run_fill(2 * n + 1)
        for hh in range(GDN_HEADS):
            gl = GDN_HEADS + hh
            o_s[rows, hh * dk:(hh + 1) * dk] = r1[hh][cc:2 * cc] + r2[hh][0:cc]
            glcol = etot[rows, gl:gl + 1]
            s_s[hh] = sts[hh] * jnp.concatenate([glcol] * (dk // cc), axis=0) + r2[hh][cc:cc + dk]

    @pl.when(j == nj - 1)
    def _():
        ssm_ref[...] = s_s[...]

    for hh in range(GDN_HEADS):
        hs = slice(hh * dk, (hh + 1) * dk)
        o = o_s[:, hs]
        z = zg_ref[:, hs]
        on = o * lax.rsqrt(jnp.mean(o * o, axis=-1, keepdims=True) + EPS) * onorm_ref[:, hs]
        y_s[:, sgu_w + pool_w + hh * dk:sgu_w + pool_w + (hh + 1) * dk] = (on * (z * _sigmoid(z))).astype(BF16)
    h2_ref[...] = h2_ref[...] + _dot(y_s[:, pre_w:], wout_ref[pre_w:, :])


def _mixer(h, ug, vn, yb, q, k, v, zg, ssm0, lw, ts, c_sgu):
    b, l, d = h.shape
    sgu_w, pool_w, gdn_w = vn.shape[2], yb.shape[2], q.shape[2]
    dk = gdn_w // GDN_HEADS
    nch = ts // GDN_CHUNK
    mix_w = sgu_w + pool_w + gdn_w
    tile = lambda width: pl.BlockSpec((None, ts, width), lambda i, j: (i, j, 0))
    per_b = lambda *shape: pl.BlockSpec((None,) + shape, lambda i, j: (i,) + (0,) * len(shape))
    res = _res
    weights = [lw["sgw"], lw["sgb"], lw["alog"], lw["dtb"], lw["onorm"], lw["wout"]]
    return pl.pallas_call(
        functools.partial(_mixer_kernel, ts=ts, c_sgu=c_sgu),
        grid=(b, l // ts),
        in_specs=[tile(sgu_w), tile(sgu_w), tile(pool_w), tile(gdn_w), tile(gdn_w), tile(gdn_w),
                  tile(zg.shape[2]), tile(d), per_b(GDN_HEADS, dk, dk)] + [res(w) for w in weights],
        out_specs=[tile(d), per_b(GDN_HEADS, dk, dk)],
        out_shape=[jax.ShapeDtypeStruct((b, l, d), F32), jax.ShapeDtypeStruct((b, GDN_HEADS, dk, dk), F32)],
        scratch_shapes=[
            pltpu.VMEM((ts, gdn_w), F32),
            pltpu.VMEM((nch, GDN_HEADS, 2 * GDN_CHUNK, dk), BF16),
            pltpu.VMEM((nch, GDN_HEADS, GDN_CHUNK + dk, GDN_CHUNK), BF16),
            pltpu.VMEM((ts, gdn_w), F32),
            pltpu.VMEM((ts, mix_w), BF16),
            pltpu.VMEM((GDN_HEADS, dk, dk), F32),
        ],
        compiler_params=pltpu.CompilerParams(dimension_semantics=("parallel", "arbitrary"),
                                             vmem_limit_bytes=VMEM_LIMIT_BYTES),
        name="mixer",
    )(ug, vn, yb, q, k, v, zg, h, ssm0, *(_arr(w) for w in weights))


def _attn_ffn_kernel(h_ref, mk_ref, mv_ref, gx_ref, wq_ref, wo_ref, g2_ref, wg_ref, wu_ref, wd_ref, gf_ref,
                     out_ref, nb_ref, act_ref, ob_ref, *, final, sl):
    h = h_ref[...]
    nb_ref[...] = _rms(h, gx_ref[...]).astype(BF16)
    q = _dot(nb_ref[...], wq_ref[...])
    dh = q.shape[1] // MEM_HEADS
    scale = dh ** -0.5
    for s in range(h.shape[0] // sl):
        rows = slice(s * sl, (s + 1) * sl)
        for hh in range(MEM_HEADS):
            hs = slice(hh * dh, (hh + 1) * dh)
            sc = _dot_nt(q[rows, hs].astype(BF16), mk_ref[s, :, hs].astype(BF16)) * scale
            e = jnp.exp(sc - jnp.max(sc, axis=-1, keepdims=True))
            pr = e / jnp.sum(e, axis=-1, keepdims=True)
            ob_ref[rows, hs] = _dot(pr.astype(BF16), mv_ref[s, :, hs].astype(BF16)).astype(BF16)
    h = h + _dot(ob_ref[...], wo_ref[...])
    nb_ref[...] = _rms(h, g2_ref[...]).astype(BF16)
    h = h + 0.5 * _swiglu(nb_ref, wg_ref, wu_ref, wd_ref, act_ref)
    if final:
        h = _rms(h, gf_ref[...])
    out_ref[...] = h


def _attn_ffn(h2d, mk, mv, lw, gf, tm, seq_len, final):
    t, d = h2d.shape
    d_ff = _arr(lw["wg2"]).shape[-1]
    m, mw = mk.shape[1], mk.shape[2]
    sl = min(seq_len, tm)
    nseq = tm // sl
    tps = seq_len // sl
    tile = pl.BlockSpec((tm, d), lambda i: (i, 0))
    mem = pl.BlockSpec((nseq, m, mw), lambda i: (i // tps, 0, 0))
    weights = [lw["gx"], lw["wmq"], lw["wmo"], lw["g2"], lw["wg2"], lw["wu2"], lw["wd2"], gf]
    return pl.pallas_call(
        functools.partial(_attn_ffn_kernel, final=final, sl=sl),
        grid=(t // tm,),
        in_specs=[tile, mem, mem] + [_res(w) for w in weights],
        out_specs=tile,
        out_shape=jax.ShapeDtypeStruct((t, d), F32),
        scratch_shapes=[pltpu.VMEM((tm, d), BF16), pltpu.VMEM((tm, d_ff), BF16), pltpu.VMEM((tm, mw), BF16)],
        compiler_params=pltpu.CompilerParams(dimension_semantics=("parallel",),
                                             vmem_limit_bytes=VMEM_LIMIT_BYTES),
        name="attn_ffn",
    )(h2d, mk, mv, *(_arr(w) for w in weights))


def _mem_kv_kernel(mem_ref, g_ref, wk_ref, wv_ref, k_ref, v_ref):
    mb = _rms(mem_ref[...], g_ref[...]).astype(BF16)
    k_ref[...] = _dot(mb, wk_ref[...])
    v_ref[...] = _dot(mb, wv_ref[...])


def _mem_kv(mem, g, wk, wv):
    b, m, d = mem.shape
    mw = _arr(wk).shape[-1]
    blk = lambda width: pl.BlockSpec((None, m, width), lambda i: (i, 0, 0))
    return pl.pallas_call(
        _mem_kv_kernel,
        grid=(b,),
        in_specs=[blk(d), _res(g), _res(wk), _res(wv)],
        out_specs=[blk(mw), blk(mw)],
        out_shape=[jax.ShapeDtypeStruct((b, m, mw), F32)] * 2,
        compiler_params=pltpu.CompilerParams(dimension_semantics=("parallel",)),
        name="mem_kv",
    )(mem, g, _arr(wk), _arr(wv))


def _layer_weights(l, c_sgu, big, ffn1_norm, mix_norm, sgu_norm, sgu_w, sgu_b, pool_w, pool_scale, gdn_conv_w,
                   gdn_a_log, gdn_dt_bias, gdn_out_norm, xattn_norm, ffn2_norm):
    sgu_width = sgu_norm.shape[1]
    pool_width = pool_scale.shape[1]
    gdn_width = gdn_out_norm.shape[1] * GDN_HEADS
    row = lambda vec: vec.reshape(1, -1).astype(F32)
    groups = len(POOL_WINDOWS)
    gdim = pool_width // groups
    poolw = jnp.zeros((pool_width, pool_width), F32)
    for gi in range(groups):
        poolw = poolw.at[gi * gdim:(gi + 1) * gdim, gi * gdim:(gi + 1) * gdim].set(pool_w[l, gi])
    hd = sgu_width // SGU_HEADS
    seg = jnp.arange(sgu_width) // hd
    avg = jnp.where(seg[:, None] == seg[None, :], 1.0 / hd, 0.0).astype(BF16)
    sgw = jnp.concatenate([sgu_w[l, hh, :c_sgu, :c_sgu] for hh in range(SGU_HEADS)], axis=1)
    sgb = jnp.repeat(sgu_b[l, :, :c_sgu].T, hd, axis=1)
    lane_pad = lambda vec: jnp.pad(vec, (GDN_HEADS, LANES - 2 * GDN_HEADS)).reshape(1, LANES)
    lw = {name: (arr, l) for name, arr in big.items()}
    lw.update(
        splits=_in_splits(sgu_width, pool_width, gdn_width),
        g1=row(ffn1_norm[l]), gmix=row(mix_norm[l]),
        sgw=sgw, sgb=sgb, sgn=row(sgu_norm[l]), avg=avg,
        poolw=poolw.astype(BF16), pools=row(pool_scale[l]),
        convw=gdn_conv_w[l], alog=lane_pad(gdn_a_log[l]), dtb=lane_pad(gdn_dt_bias[l]),
        onorm=row(jnp.tile(gdn_out_norm[l], GDN_HEADS)),
        gx=row(xattn_norm[l]), g2=row(ffn2_norm[l]),
    )
    return lw


def _in_splits(sgu_width, pool_width, gdn_width):
    o_pool = 2 * sgu_width
    o_qkv = o_pool + pool_width
    o_z = o_qkv + 3 * gdn_width
    return o_pool, o_qkv, o_z, o_z + gdn_width + LANES


def _layer(x, mk, mv, pool0, conv0, ssm0, lw, gf, final, pos0, tm, ts):
    b, l, d = x.shape
    h, zg, ug, vn, yb, q, k, v, pool_new, conv_new = _ffn_in(x.reshape(b * l, d), pool0, conv0, lw, tm, l, pos0)
    shp = lambda arr: arr.reshape(b, l, arr.shape[-1])
    c_sgu = lw["sgb"].shape[0]
    h2, ssm_new = _mixer(shp(h), shp(ug), shp(vn), shp(yb), shp(q), shp(k), shp(v), shp(zg), ssm0, lw, ts, c_sgu)
    out = _attn_ffn(h2.reshape(b * l, d), mk, mv, lw, gf, tm, l, final)
    return (shp(out), shp(vn), pool_new[:, POOL_PAD - POOL_HIST:], conv_new[:, CONV_PAD - (GDN_CONV - 1):], ssm_new)


def kernel(x_prompt, x_sample, mem_prompt, cache_mem_k, cache_mem_v, state_pool, state_conv, state_ssm, ffn1_norm, ffn1_w_gate, ffn1_w_up, ffn1_w_down, mix_norm, w_in, sgu_norm, sgu_w, sgu_b, pool_w, pool_scale, gdn_conv_w, gdn_a_log, gdn_dt_bias, gdn_out_norm, w_out, xattn_norm, mem_norm, w_mq, w_mk, w_mv, w_mo, ffn2_norm, ffn2_w_gate, ffn2_w_up, ffn2_w_down, final_norm):
    depth = w_in.shape[0]
    bp, lp, d = x_prompt.shape
    bs, ls, _ = x_sample.shape
    mem_len = mem_prompt.shape[1]
    mem_w = w_mk.shape[2]
    pool_width = state_pool.shape[-1]
    qkv_width = state_conv.shape[-1]
    dk = state_ssm.shape[-1]
    n_pad = _in_splits(sgu_norm.shape[1], pool_width, gdn_out_norm.shape[1] * GDN_HEADS)[3]
    big = dict(wg1=_cast_bf16(ffn1_w_gate), wu1=_cast_bf16(ffn1_w_up), wd1=_cast_bf16(ffn1_w_down),
               win=_cast_bf16(w_in, cols=n_pad), wout=_cast_bf16(w_out), wmq=_cast_bf16(w_mq),
               wmo=_cast_bf16(w_mo), wg2=_cast_bf16(ffn2_w_gate), wu2=_cast_bf16(ffn2_w_up),
               wd2=_cast_bf16(ffn2_w_down))
    wmk, wmv = _cast_bf16(w_mk), _cast_bf16(w_mv)
    small = (ffn1_norm, mix_norm, sgu_norm, sgu_w, sgu_b, pool_w, pool_scale, gdn_conv_w, gdn_a_log, gdn_dt_bias,
             gdn_out_norm, xattn_norm, ffn2_norm)
    gf = final_norm.reshape(1, d)
    pool0_p = jnp.zeros((bp, POOL_PAD, pool_width), F32)
    conv0_p = jnp.zeros((bp, CONV_PAD, qkv_width), F32)
    ssm0_p = jnp.zeros((bp, GDN_HEADS, dk, dk), F32)
    xp, xs = x_prompt, x_sample
    outs_p = [[] for _ in range(5)]
    outs_s = [[] for _ in range(4)]
    for l in range(depth):
        final = l == depth - 1
        lw_p = _layer_weights(l, min(128, lp), big, *small)
        lw_s = lw_p if min(128, ls) == min(128, lp) else _layer_weights(l, min(128, ls), big, *small)
        mk, mv = _mem_kv(mem_prompt, mem_norm[l].reshape(1, d), (wmk, l), (wmv, l))
        xp, _, pp, pc, ps = _layer(xp, mk, mv, pool0_p, conv0_p, ssm0_p, lw_p, gf, final, 0, tm=512, ts=256)
        pool0_s = jnp.pad(state_pool[l], ((0, 0), (POOL_PAD - POOL_HIST, 0), (0, 0)))
        conv0_s = jnp.pad(state_conv[l], ((0, 0), (CONV_PAD - (GDN_CONV - 1), 0), (0, 0)))
        xs, sv, sp, sc, ss = _layer(xs, cache_mem_k[l].reshape(bs, mem_len, mem_w),
                                    cache_mem_v[l].reshape(bs, mem_len, mem_w),
                                    pool0_s, conv0_s, state_ssm[l], lw_s, gf, final, PAST_LEN, tm=bs * ls, ts=ls)
        for lst, val in zip(outs_p, (pp, pc, ps, mk.reshape(bp, mem_len, MEM_HEADS, mem_w // MEM_HEADS),
                                     mv.reshape(bp, mem_len, MEM_HEADS, mem_w // MEM_HEADS))):
            lst.append(val)
        for lst, val in zip(outs_s, (sp, sc, ss, sv)):
            lst.append(val)
    return (xp, xs, *(jnp.stack(v) for v in outs_p), *(jnp.stack(v) for v in outs_s))
```

```python
import functools
import math

import jax
import jax.numpy as jnp
from jax import lax
from jax.experimental import pallas as pl
from jax.experimental.pallas import tpu as pltpu

F32 = jnp.float32
BF16 = jnp.bfloat16

EPS = 1e-6
L2_EPS = 1e-6
LANES = 128
MXU_COLS = 256
VMEM_LIMIT_BYTES = 58 * 1024 * 1024

SGU_HEADS = 4
SGU_BLOCK = 64
POOL_WINDOWS = (2, 4, 8, 16)
POOL_HIST = max(POOL_WINDOWS) - 1
POOL_PAD = 16
GDN_HEADS = 4
GDN_CONV = 4
CONV_PAD = 8
GDN_CHUNK = 64
MEM_HEADS = 4
PAST_LEN = 2048
NEG_BIG = -1e30
MIXER_SEQS = 2


def _dot(a, b):
    return jnp.dot(a, b, preferred_element_type=F32)


def _dot_nt(a, b):
    return lax.dot_general(a, b, (((1,), (1,)), ((), ())), preferred_element_type=F32)


def _rms(x, g):
    return x * lax.rsqrt(jnp.mean(x * x, axis=-1, keepdims=True) + EPS) * g


def _sigmoid(x):
    return 1.0 / (1.0 + jnp.exp(-x))


def _gelu_tanh(x):
    return 0.5 * x * (1.0 + jnp.tanh(math.sqrt(2.0 / math.pi) * (x + 0.044715 * (x * x * x))))


def _softplus(x):
    return jnp.maximum(x, 0.0) + jnp.log1p(jnp.exp(-jnp.abs(x)))


def _swiglu(nb_ref, wg_ref, wu_ref, wd_ref, act_ref):
    d_ff = wg_ref.shape[1]
    for c in range(d_ff // MXU_COLS):
        sl = slice(c * MXU_COLS, (c + 1) * MXU_COLS)
        g = _dot(nb_ref[...], wg_ref[:, sl])
        u = _dot(nb_ref[...], wu_ref[:, sl])
        act_ref[:, sl] = (g * _sigmoid(g) * u).astype(BF16)
    return _dot(act_ref[...], wd_ref[...])


def _ffn_in_kernel(x_ref, pool0_ref, conv0_ref, g1_ref, wg_ref, wu_ref, wd_ref, gm_ref, win_ref,
                   sgn_ref, avg_ref, poolw_ref, pools_ref, convw_ref,
                   h_ref, zg_ref, ug_ref, vn_ref, yb_ref, q_ref, k_ref, v_ref, poolst_ref, convst_ref,
                   nb_ref, act_ref, uv_s, pbuf, cbuf, *, splits, sl, tps, pos0):
    i = pl.program_id(0)
    tm = x_ref.shape[0]
    nseq = tm // sl
    sgu_w = sgn_ref.shape[1]
    pool_w = pools_ref.shape[1]
    gdn_w = q_ref.shape[1]
    dk = gdn_w // GDN_HEADS
    o_pool, o_qkv, o_z, n_pad = splits
    iv = i

    @pl.when(i % tps == 0)
    def _():
        pbuf[:, 0:POOL_PAD, :] = pool0_ref[...]
        cbuf[:, 0:CONV_PAD, :] = conv0_ref[...]

    def matmul_stage():
        x = x_ref[...]
        nb_ref[...] = _rms(x, g1_ref[...]).astype(BF16)
        h = x + 0.5 * _swiglu(nb_ref, wg_ref, wu_ref, wd_ref, act_ref)
        h_ref[...] = h
        nb_ref[...] = _rms(h, gm_ref[...]).astype(BF16)
        uv_s[...] = _dot(nb_ref[...], win_ref[:, 0:o_pool])
        zg_ref[...] = _dot(nb_ref[...], win_ref[:, o_z:n_pad])
        p = _dot(nb_ref[...], win_ref[:, o_pool:o_qkv])
        qkv = _dot(nb_ref[...], win_ref[:, o_qkv:o_z])
        for s in range(nseq):
            pbuf[s, POOL_PAD:POOL_PAD + sl, :] = p[s * sl:(s + 1) * sl]
            cbuf[s, CONV_PAD:CONV_PAD + sl, :] = qkv[s * sl:(s + 1) * sl]

    def vector_stage():
        uv = _gelu_tanh(uv_s[...])
        ug_ref[...] = uv[:, 0:sgu_w]
        v = uv[:, sgu_w:]
        avg = avg_ref[...]

        def seg_mean(t):
            hi = t.astype(BF16)
            lo = (t - hi.astype(F32)).astype(BF16)
            return _dot(hi, avg) + _dot(lo, avg)

        vc = v - seg_mean(v)
        vn_ref[...] = vc * lax.rsqrt(seg_mean(vc * vc) + EPS) * sgn_ref[...]

        gdim = pool_w // len(POOL_WINDOWS)
        lane = lax.broadcasted_iota(jnp.int32, (sl, LANES), 1)
        pos = pos0 + (iv % tps) * sl + lax.broadcasted_iota(jnp.int32, (sl, LANES), 0)
        dparts = []
        for s in range(nseq):
            dblk = []
            for blk in range(pool_w // LANES):
                ls = slice(blk * LANES, (blk + 1) * LANES)
                wins = [POOL_WINDOWS[(blk * LANES + l0) // gdim] for l0 in range(0, LANES, gdim)]
                wl = jnp.full((sl, LANES), wins[-1], jnp.int32)
                for gi in range(len(wins) - 2, -1, -1):
                    wl = jnp.where(lane < (gi + 1) * gdim, wins[gi], wl)
                pb = pbuf[s, POOL_PAD:POOL_PAD + sl, ls]
                acc = pb
                prev = 1
                for w in sorted(set(wins)):
                    part = None
                    for jj in range(prev, w):
                        sh = pbuf[s, POOL_PAD - jj:POOL_PAD - jj + sl, ls]
                        part = sh if part is None else part + sh
                    if part is not None:
                        acc = acc + (part if w == min(wins) else jnp.where(wl >= w, part, 0.0))
                    prev = w
                cnt = jnp.minimum(wl, pos + 1).astype(F32)
                dblk.append(acc / cnt - pb)
            dparts.append(jnp.concatenate(dblk, axis=1))
            tail = pbuf[s, sl:sl + POOL_PAD, :]
            poolst_ref[s] = tail
            pbuf[s, 0:POOL_PAD, :] = tail
        dlt = (dparts[0] if nseq == 1 else jnp.concatenate(dparts, axis=0)).astype(BF16)
        yb_ref[...] = (_dot(dlt, poolw_ref[...]) * pools_ref[...]).astype(BF16)

        qscale = dk ** -0.5
        for cb in range(3 * gdn_w // dk):
            cs = slice(cb * dk, (cb + 1) * dk)
            which, hh = divmod(cb, GDN_HEADS)
            hs = slice(hh * dk, (hh + 1) * dk)
            for s in range(nseq):
                rows = slice(s * sl, (s + 1) * sl)
                y = cbuf[s, CONV_PAD:CONV_PAD + sl, cs] * convw_ref[GDN_CONV - 1:GDN_CONV, cs]
                for t in range(1, GDN_CONV):
                    y = y + (cbuf[s, CONV_PAD - t:CONV_PAD - t + sl, cs]
                             * convw_ref[GDN_CONV - 1 - t:GDN_CONV - t, cs])
                y = y * _sigmoid(y)
                if which == 0:
                    q_ref[rows, hs] = y * (lax.rsqrt(jnp.sum(y * y, axis=-1, keepdims=True) + L2_EPS) * qscale)
                elif which == 1:
                    k_ref[rows, hs] = y * lax.rsqrt(jnp.sum(y * y, axis=-1, keepdims=True) + L2_EPS)
                else:
                    v_ref[rows, hs] = y
        for s in range(nseq):
            ctail = cbuf[s, sl:sl + CONV_PAD, :]
            convst_ref[s] = ctail
            cbuf[s, 0:CONV_PAD, :] = ctail

    matmul_stage()
    vector_stage()


def _res(w):
    if isinstance(w, tuple):
        arr, l = w
        nd = arr.ndim - 1
        return pl.BlockSpec((None,) + arr.shape[1:], lambda *_: (l,) + (0,) * nd, pipeline_mode=pl.Buffered(1))
    nd = w.ndim
    return pl.BlockSpec(w.shape, lambda *_: (0,) * nd, pipeline_mode=pl.Buffered(1))


def _arr(w):
    return w[0] if isinstance(w, tuple) else w


def _cast_kernel(x_ref, o_ref):
    w = x_ref.shape[-1]
    o_ref[:, 0:w] = x_ref[...].astype(BF16)
    if o_ref.shape[-1] > w:
        o_ref[:, w:] = jnp.zeros((o_ref.shape[0], o_ref.shape[-1] - w), BF16)


def _cast_bf16(w, cols=None, block_rows=256):
    depth, rows, c = w.shape
    cols = c if cols is None else cols
    br = min(block_rows, rows)
    return pl.pallas_call(
        _cast_kernel,
        grid=(depth, rows // br),
        in_specs=[pl.BlockSpec((None, br, c), lambda l, i: (l, i, 0))],
        out_specs=pl.BlockSpec((None, br, cols), lambda l, i: (l, i, 0)),
        out_shape=jax.ShapeDtypeStruct((depth, rows, cols), BF16),
        compiler_params=pltpu.CompilerParams(dimension_semantics=("parallel", "parallel")),
        name="cast_bf16",
    )(w)


def _ffn_in(x2d, pool0, conv0, lw, tm, seq_len, pos0):
    t, d = x2d.shape
    d_ff = _arr(lw["wg1"]).shape[-1]
    o_pool, o_qkv, o_z, n_pad = lw["splits"]
    sgu_w, pool_w, gdn_w = lw["sgn"].shape[1], lw["pools"].shape[1], lw["onorm"].shape[1]
    sl = min(seq_len, tm)
    nseq = tm // sl
    tps = seq_len // sl
    row = lambda width: pl.BlockSpec((tm, width), lambda i: (i, 0))
    hist = lambda pad, width: pl.BlockSpec((nseq, pad, width), lambda i: (i // tps, 0, 0))
    res = _res
    weights = [lw["g1"], lw["wg1"], lw["wu1"], lw["wd1"], lw["gmix"], lw["win"],
               lw["sgn"], lw["avg"], lw["poolw"], lw["pools"], lw["convw"]]
    outs = [(d, F32), (n_pad - o_z, F32), (sgu_w, F32), (sgu_w, F32), (pool_w, BF16),
            (gdn_w, F32), (gdn_w, F32), (gdn_w, F32)]
    nbatch = t // seq_len
    return pl.pallas_call(
        functools.partial(_ffn_in_kernel, splits=lw["splits"], sl=sl, tps=tps, pos0=pos0),
        grid=(t // tm,),
        in_specs=[row(d), hist(POOL_PAD, pool_w), hist(CONV_PAD, 3 * gdn_w)] + [res(w) for w in weights],
        out_specs=[row(w) for w, _ in outs] + [hist(POOL_PAD, pool_w), hist(CONV_PAD, 3 * gdn_w)],
        out_shape=[jax.ShapeDtypeStruct((t, w), dt) for w, dt in outs]
                  + [jax.ShapeDtypeStruct((nbatch, POOL_PAD, pool_w), F32),
                     jax.ShapeDtypeStruct((nbatch, CONV_PAD, 3 * gdn_w), F32)],
        scratch_shapes=[pltpu.VMEM((tm, d), BF16), pltpu.VMEM((tm, d_ff), BF16),
                        pltpu.VMEM((tm, o_pool), F32),
                        pltpu.VMEM((nseq, POOL_PAD + sl, pool_w), F32),
                        pltpu.VMEM((nseq, CONV_PAD + sl, 3 * gdn_w), F32)],
        compiler_params=pltpu.CompilerParams(dimension_semantics=("arbitrary",),
                                             vmem_limit_bytes=VMEM_LIMIT_BYTES),
        name="ffn_in",
    )(x2d, pool0, conv0, *(_arr(w) for w in weights))


def _mixer_kernel(ug_ref, vn_ref, yb_ref, q_ref, k_ref, v_ref, zg_ref, h_ref, ssm0_ref,
                  sgw_ref, sgb_ref, alog_ref, dtb_ref, onorm_ref, wout_ref,
                  h2_ref, ssm_ref,
                  u_s, wq_s, ql_s, y_s, s_s, *, ts, c_sgu):
    j = pl.program_id(1)
    nj = pl.num_programs(1)
    ns = vn_ref.shape[0]
    sgu_w = vn_ref.shape[2]
    pool_w = yb_ref.shape[2]
    gdn_w = onorm_ref.shape[1]
    dk = gdn_w // GDN_HEADS
    cc = GDN_CHUNK
    nch = ts // cc
    pre_w = sgu_w + pool_w
    seqs = range(ns)

    @pl.when(j == 0)
    def _():
        s_s[...] = ssm0_ref[...]

    hd = sgu_w // SGU_HEADS
    wi = lax.broadcasted_iota(jnp.int32, (c_sgu, SGU_HEADS * c_sgu), 0)
    wj = lax.broadcasted_iota(jnp.int32, (c_sgu, SGU_HEADS * c_sgu), 1) % c_sgu
    wmask = jnp.where(wi // SGU_BLOCK >= wj // SGU_BLOCK, sgw_ref[...], 0.0).astype(BF16)
    lane_head = lax.broadcasted_iota(jnp.int32, (c_sgu, sgu_w), 1) // hd
    for e in seqs:
        for c in range(ts // c_sgu):
            rows = slice(c * c_sgu, (c + 1) * c_sgu)
            vch = vn_ref[e, rows, :]
            vstack = jnp.concatenate([jnp.where(lane_head == hh, vch, 0.0) for hh in range(SGU_HEADS)],
                                     axis=0).astype(BF16)
            s = _dot(wmask, vstack) + sgb_ref[...]
            y_s[e * ts + c * c_sgu:e * ts + (c + 1) * c_sgu, 0:sgu_w] = (ug_ref[e, rows, :] * s).astype(BF16)
        y_s[e * ts:(e + 1) * ts, sgu_w:pre_w] = yb_ref[e]

    ti = lax.broadcasted_iota(jnp.int32, (2 * ts, ts), 0)
    tj = lax.broadcasted_iota(jnp.int32, (2 * ts, ts), 1)
    tr = jnp.where(ti < ts, ti, ti - ts)
    same_chunk = tr // cc == tj // cc
    summat = jnp.where(same_chunk, jnp.where(ti < ts, jnp.where(tj <= tr, 1.0, 0.0), 1.0), 0.0).astype(BF16)
    beta, gc, egc, erg, etot = [], [], [], [], []
    for e in seqs:
        gates = zg_ref[e, :, gdn_w:gdn_w + LANES]
        beta.append(_sigmoid(gates))
        g = -jnp.exp(alog_ref[...]) * _softplus(gates + dtb_ref[...])
        g_hi = g.astype(BF16)
        g_r = g - g_hi.astype(F32)
        g_mid = g_r.astype(BF16)
        g_lo = (g_r - g_mid.astype(F32)).astype(BF16)
        sums = _dot(summat, g_hi) + _dot(summat, g_mid) + _dot(summat, g_lo)
        gc.append(sums[0:ts])
        tot = sums[ts:2 * ts]
        egc.append(jnp.exp(gc[e]))
        erg.append(jnp.exp(tot - gc[e]))
        etot.append(jnp.exp(tot))

    ri = lax.broadcasted_iota(jnp.int32, (cc, cc), 0)
    ci = lax.broadcasted_iota(jnp.int32, (cc, cc), 1)
    incl = ri >= ci
    strict = ri > ci
    chains = [(e, n, hh) for n in range(nch) for e in seqs for hh in range(GDN_HEADS)]
    gct = {(e, n): gc[e][n * cc:(n + 1) * cc].T for e in seqs for n in range(nch)}

    nmats = []
    for e, n, hh in chains:
        rows = slice(n * cc, (n + 1) * cc)
        hs = slice(hh * dk, (hh + 1) * dk)
        gl = GDN_HEADS + hh
        k = k_ref[e, rows, hs]
        kbq = jnp.concatenate([k * beta[e][rows, hh:hh + 1], q_ref[e, rows, hs]], axis=0).astype(BF16)
        prod = _dot_nt(kbq, k.astype(BF16))
        decay = jnp.exp(jnp.where(incl, gc[e][rows, gl:gl + 1] - gct[e, n][gl:gl + 1, :], NEG_BIG))
        nmats.append(jnp.where(strict, -(prod[0:cc] * decay), 0.0))
        ql_s[e, n, hh, 0:cc, :] = (prod[cc:2 * cc] * decay).astype(BF16)
    tps = list(nmats)
    pows = list(nmats)
    for _ in range(5):
        pows = [_dot(m.astype(BF16), m.astype(BF16)) for m in pows]
        tps = [t + m + _dot(t.astype(BF16), m.astype(BF16)) for t, m in zip(tps, pows)]
    for (e, n, hh), tp in zip(chains, tps):
        rows = slice(n * cc, (n + 1) * cc)
        hs = slice(hh * dk, (hh + 1) * dk)
        gl = GDN_HEADS + hh
        k = k_ref[e, rows, hs]
        b = beta[e][rows, hh:hh + 1]
        eg = egc[e][rows, gl:gl + 1]
        rhs = jnp.concatenate([v_ref[e, rows, hs] * b, k * (b * eg)], axis=1)
        sol = rhs + _dot(tp.astype(BF16), rhs.astype(BF16))
        u_s[e, rows, hs] = sol[:, 0:dk]
        wq_s[e, n, hh, 0:cc, :] = sol[:, dk:2 * dk].astype(BF16)
        wq_s[e, n, hh, cc:2 * cc, :] = (q_ref[e, rows, hs] * eg).astype(BF16)
        ql_s[e, n, hh, cc:cc + dk, :] = (k * erg[e][rows, gl:gl + 1]).T.astype(BF16)

    fill = []
    for kb in range(pre_w // MXU_COLS):
        for nb in range(h_ref.shape[2] // MXU_COLS):
            fill.append((slice(kb * MXU_COLS, (kb + 1) * MXU_COLS), slice(nb * MXU_COLS, (nb + 1) * MXU_COLS), kb == 0))
    gaps = 2 * nch

    def run_fill(gap):
        for ks, cs, first in fill[gap * len(fill) // gaps:(gap + 1) * len(fill) // gaps]:
            part = _dot(y_s[:, ks], wout_ref[ks, cs])
            for e in seqs:
                base = h_ref[e, :, cs] if first else h2_ref[e, :, cs]
                h2_ref[e, :, cs] = base + part[e * ts:(e + 1) * ts]

    heads = [(e, hh) for e in seqs for hh in range(GDN_HEADS)]
    for n in range(nch):
        rows = slice(n * cc, (n + 1) * cc)
        sts = [s_s[e, hh] for e, hh in heads]
        r1 = [_dot(wq_s[e, n, hh], st.astype(BF16)) for (e, hh), st in zip(heads, sts)]
        run_fill(2 * n)
        vnew = [u_s[e, rows, hh * dk:(hh + 1) * dk] - r[0:cc] for (e, hh), r in zip(heads, r1)]
        r2 = [_dot(ql_s[e, n, hh], vn.astype(BF16)) for (e, hh), vn in zip(heads, vnew)]
        run_fill(2 * n + 1)
        for (e, hh), st, ra, rb in zip(heads, sts, r1, r2):
            gl = GDN_HEADS + hh
            hs = slice(hh * dk, (hh + 1) * dk)
            glcol = etot[e][rows, gl:gl + 1]
            s_s[e, hh] = st * jnp.concatenate([glcol] * (dk // cc), axis=0) + rb[cc:cc + dk]
            o = ra[cc:2 * cc] + rb[0:cc]
            z = zg_ref[e, rows, hs]
            on = o * lax.rsqrt(jnp.mean(o * o, axis=-1, keepdims=True) + EPS) * onorm_ref[:, hs]
            y_s[e * ts + n * cc:e * ts + (n + 1) * cc, pre_w + hh * dk:pre_w + (hh + 1) * dk] = (
                on * (z * _sigmoid(z))).astype(BF16)

    @pl.when(j == nj - 1)
    def _():
        ssm_ref[...] = s_s[...]

    part = _dot(y_s[:, pre_w:], wout_ref[pre_w:, :])
    for e in seqs:
        h2_ref[e] = h2_ref[e] + part[e * ts:(e + 1) * ts]


def _mixer(h, ug, vn, yb, q, k, v, zg, ssm0, lw, ts, c_sgu, ns):
    b, l, d = h.shape
    sgu_w, pool_w, gdn_w = vn.shape[2], yb.shape[2], q.shape[2]
    dk = gdn_w // GDN_HEADS
    nch = ts // GDN_CHUNK
    mix_w = sgu_w + pool_w + gdn_w
    tile = lambda width: pl.BlockSpec((ns, ts, width), lambda i, j: (i, j, 0))
    per_b = lambda *shape: pl.BlockSpec((ns,) + shape, lambda i, j: (i,) + (0,) * len(shape))
    res = _res
    weights = [lw["sgw"], lw["sgb"], lw["alog"], lw["dtb"], lw["onorm"], lw["wout"]]
    return pl.pallas_call(
        functools.partial(_mixer_kernel, ts=ts, c_sgu=c_sgu),
        grid=(b // ns, l // ts),
        in_specs=[tile(sgu_w), tile(sgu_w), tile(pool_w), tile(gdn_w), tile(gdn_w), tile(gdn_w),
                  tile(zg.shape[2]), tile(d), per_b(GDN_HEADS, dk, dk)] + [res(w) for w in weights],
        out_specs=[tile(d), per_b(GDN_HEADS, dk, dk)],
        out_shape=[jax.ShapeDtypeStruct((b, l, d), F32), jax.ShapeDtypeStruct((b, GDN_HEADS, dk, dk), F32)],
        scratch_shapes=[
            pltpu.VMEM((ns, ts, gdn_w), F32),
            pltpu.VMEM((ns, nch, GDN_HEADS, 2 * GDN_CHUNK, dk), BF16),
            pltpu.VMEM((ns, nch, GDN_HEADS, GDN_CHUNK + dk, GDN_CHUNK), BF16),
            pltpu.VMEM((ns * ts, mix_w), BF16),
            pltpu.VMEM((ns, GDN_HEADS, dk, dk), F32),
        ],
        compiler_params=pltpu.CompilerParams(dimension_semantics=("parallel", "arbitrary"),
                                             vmem_limit_bytes=VMEM_LIMIT_BYTES),
        name="mixer",
    )(ug, vn, yb, q, k, v, zg, h, ssm0, *(_arr(w) for w in weights))


def _attn_ffn_kernel(h_ref, mk_ref, mv_ref, gx_ref, wq_ref, wo_ref, g2_ref, wg_ref, wu_ref, wd_ref, gf_ref,
                     out_ref, nb_ref, act_ref, ob_ref, *, final, sl):
    h = h_ref[...]
    nb_ref[...] = _rms(h, gx_ref[...]).astype(BF16)
    q = _dot(nb_ref[...], wq_ref[...])
    dh = q.shape[1] // MEM_HEADS
    scale = dh ** -0.5
    for s in range(h.shape[0] // sl):
        rows = slice(s * sl, (s + 1) * sl)
        for hh in range(MEM_HEADS):
            hs = slice(hh * dh, (hh + 1) * dh)
            sc = _dot_nt(q[rows, hs].astype(BF16), mk_ref[s, :, hs].astype(BF16)) * scale
            e = jnp.exp(sc - jnp.max(sc, axis=-1, keepdims=True))
            pr = e / jnp.sum(e, axis=-1, keepdims=True)
            ob_ref[rows, hs] = _dot(pr.astype(BF16), mv_ref[s, :, hs].astype(BF16)).astype(BF16)
    h = h + _dot(ob_ref[...], wo_ref[...])
    nb_ref[...] = _rms(h, g2_ref[...]).astype(BF16)
    h = h + 0.5 * _swiglu(nb_ref, wg_ref, wu_ref, wd_ref, act_ref)
    if final:
        h = _rms(h, gf_ref[...])
    out_ref[...] = h


def _attn_ffn(h2d, mk, mv, lw, gf, tm, seq_len, final):
    t, d = h2d.shape
    d_ff = _arr(lw["wg2"]).shape[-1]
    m, mw = mk.shape[1], mk.shape[2]
    sl = min(seq_len, tm)
    nseq = tm // sl
    tps = seq_len // sl
    tile = pl.BlockSpec((tm, d), lambda i: (i, 0))
    mem = pl.BlockSpec((nseq, m, mw), lambda i: (i // tps, 0, 0))
    weights = [lw["gx"], lw["wmq"], lw["wmo"], lw["g2"], lw["wg2"], lw["wu2"], lw["wd2"], gf]
    return pl.pallas_call(
        functools.partial(_attn_ffn_kernel, final=final, sl=sl),
        grid=(t // tm,),
        in_specs=[tile, mem, mem] + [_res(w) for w in weights],
        out_specs=tile,
        out_shape=jax.ShapeDtypeStruct((t, d), F32),
        scratch_shapes=[pltpu.VMEM((tm, d), BF16), pltpu.VMEM((tm, d_ff), BF16), pltpu.VMEM((tm, mw), BF16)],
        compiler_params=pltpu.CompilerParams(dimension_semantics=("parallel",),
                                             vmem_limit_bytes=VMEM_LIMIT_BYTES),
        name="attn_ffn",
    )(h2d, mk, mv, *(_arr(w) for w in weights))


def _mem_kv_kernel(mem_ref, g_ref, wk_ref, wv_ref, k_ref, v_ref):
    mb = _rms(mem_ref[...], g_ref[...]).astype(BF16)
    k_ref[...] = _dot(mb, wk_ref[...])
    v_ref[...] = _dot(mb, wv_ref[...])


def _mem_kv(mem, g, wk, wv):
    b, m, d = mem.shape
    mw = _arr(wk).shape[-1]
    blk = lambda width: pl.BlockSpec((None, m, width), lambda i: (i, 0, 0))
    return pl.pallas_call(
        _mem_kv_kernel,
        grid=(b,),
        in_specs=[blk(d), _res(g), _res(wk), _res(wv)],
        out_specs=[blk(mw), blk(mw)],
        out_shape=[jax.ShapeDtypeStruct((b, m, mw), F32)] * 2,
        compiler_params=pltpu.CompilerParams(dimension_semantics=("parallel",)),
        name="mem_kv",
    )(mem, g, _arr(wk), _arr(wv))


def _layer_weights(l, c_sgu, big, ffn1_norm, mix_norm, sgu_norm, sgu_w, sgu_b, pool_w, pool_scale, gdn_conv_w,
                   gdn_a_log, gdn_dt_bias, gdn_out_norm, xattn_norm, ffn2_norm):
    sgu_width = sgu_norm.shape[1]
    pool_width = pool_scale.shape[1]
    gdn_width = gdn_out_norm.shape[1] * GDN_HEADS
    row = lambda vec: vec.reshape(1, -1).astype(F32)
    groups = len(POOL_WINDOWS)
    gdim = pool_width // groups
    poolw = jnp.zeros((pool_width, pool_width), F32)
    for gi in range(groups):
        poolw = poolw.at[gi * gdim:(gi + 1) * gdim, gi * gdim:(gi + 1) * gdim].set(pool_w[l, gi])
    hd = sgu_width // SGU_HEADS
    seg = jnp.arange(sgu_width) // hd
    avg = jnp.where(seg[:, None] == seg[None, :], 1.0 / hd, 0.0).astype(BF16)
    sgw = jnp.concatenate([sgu_w[l, hh, :c_sgu, :c_sgu] for hh in range(SGU_HEADS)], axis=1)
    sgb = jnp.repeat(sgu_b[l, :, :c_sgu].T, hd, axis=1)
    lane_pad = lambda vec: jnp.pad(vec, (GDN_HEADS, LANES - 2 * GDN_HEADS)).reshape(1, LANES)
    lw = {name: (arr, l) for name, arr in big.items()}
    lw.update(
        splits=_in_splits(sgu_width, pool_width, gdn_width),
        g1=row(ffn1_norm[l]), gmix=row(mix_norm[l]),
        sgw=sgw, sgb=sgb, sgn=row(sgu_norm[l]), avg=avg,
        poolw=poolw.astype(BF16), pools=row(pool_scale[l]),
        convw=gdn_conv_w[l], alog=lane_pad(gdn_a_log[l]), dtb=lane_pad(gdn_dt_bias[l]),
        onorm=row(jnp.tile(gdn_out_norm[l], GDN_HEADS)),
        gx=row(xattn_norm[l]), g2=row(ffn2_norm[l]),
    )
    return lw


def _in_splits(sgu_width, pool_width, gdn_width):
    o_pool = 2 * sgu_width
    o_qkv = o_pool + pool_width
    o_z = o_qkv + 3 * gdn_width
    return o_pool, o_qkv, o_z, o_z + gdn_width + LANES


def _layer(x, mk, mv, pool0, conv0, ssm0, lw, gf, final, pos0, tm, ts):
    b, l, d = x.shape
    h, zg, ug, vn, yb, q, k, v, pool_new, conv_new = _ffn_in(x.reshape(b * l, d), pool0, conv0, lw, tm, l, pos0)
    shp = lambda arr: arr.reshape(b, l, arr.shape[-1])
    c_sgu = lw["sgb"].shape[0]
    h2, ssm_new = _mixer(shp(h), shp(ug), shp(vn), shp(yb), shp(q), shp(k), shp(v), shp(zg), ssm0, lw, ts, c_sgu,
                         MIXER_SEQS)
    out = _attn_ffn(h2.reshape(b * l, d), mk, mv, lw, gf, tm, l, final)
    return (shp(out), shp(vn), pool_new[:, POOL_PAD - POOL_HIST:], conv_new[:, CONV_PAD - (GDN_CONV - 1):], ssm_new)


def kernel(x_prompt, x_sample, mem_prompt, cache_mem_k, cache_mem_v, state_pool, state_conv, state_ssm, ffn1_norm, ffn1_w_gate, ffn1_w_up, ffn1_w_down, mix_norm, w_in, sgu_norm, sgu_w, sgu_b, pool_w, pool_scale, gdn_conv_w, gdn_a_log, gdn_dt_bias, gdn_out_norm, w_out, xattn_norm, mem_norm, w_mq, w_mk, w_mv, w_mo, ffn2_norm, ffn2_w_gate, ffn2_w_up, ffn2_w_down, final_norm):
    depth = w_in.shape[0]
    bp, lp, d = x_prompt.shape
    bs, ls, _ = x_sample.shape
    mem_len = mem_prompt.shape[1]
    mem_w = w_mk.shape[2]
    pool_width = state_pool.shape[-1]
    qkv_width = state_conv.shape[-1]
    dk = state_ssm.shape[-1]
    n_pad = _in_splits(sgu_norm.shape[1], pool_width, gdn_out_norm.shape[1] * GDN_HEADS)[3]
    big = dict(wg1=_cast_bf16(ffn1_w_gate), wu1=_cast_bf16(ffn1_w_up), wd1=_cast_bf16(ffn1_w_down),
               win=_cast_bf16(w_in, cols=n_pad), wout=_cast_bf16(w_out), wmq=_cast_bf16(w_mq),
               wmo=_cast_bf16(w_mo), wg2=_cast_bf16(ffn2_w_gate), wu2=_cast_bf16(ffn2_w_up),
               wd2=_cast_bf16(ffn2_w_down))
    wmk, wmv = _cast_bf16(w_mk), _cast_bf16(w_mv)
    small = (ffn1_norm, mix_norm, sgu_norm, sgu_w, sgu_b, pool_w, pool_scale, gdn_conv_w, gdn_a_log, gdn_dt_bias,
             gdn_out_norm, xattn_norm, ffn2_norm)
    gf = final_norm.reshape(1, d)
    pool0_p = jnp.zeros((bp, POOL_PAD, pool_width), F32)
    conv0_p = jnp.zeros((bp, CONV_PAD, qkv_width), F32)
    ssm0_p = jnp.zeros((bp, GDN_HEADS, dk, dk), F32)
    xp, xs = x_prompt, x_sample
    outs_p = [[] for _ in range(5)]
    outs_s = [[] for _ in range(4)]
    for l in range(depth):
        final = l == depth - 1
        lw_p = _layer_weights(l, min(128, lp), big, *small)
        lw_s = lw_p if min(128, ls) == min(128, lp) else _layer_weights(l, min(128, ls), big, *small)
        mk, mv = _mem_kv(mem_prompt, mem_norm[l].reshape(1, d), (wmk, l), (wmv, l))
        xp, _, pp, pc, ps = _layer(xp, mk, mv, pool0_p, conv0_p, ssm0_p, lw_p, gf, final, 0, tm=512, ts=256)
        pool0_s = jnp.pad(state_pool[l], ((0, 0), (POOL_PAD - POOL_HIST, 0), (0, 0)))
        conv0_s = jnp.pad(state_conv[l], ((0, 0), (CONV_PAD - (GDN_CONV - 1), 0), (0, 0)))
        xs, sv, sp, sc, ss = _layer(xs, cache_mem_k[l].reshape(bs, mem_len, mem_w),
                                    cache_mem_v[l].reshape(bs, mem_len, mem_w),
                                    pool0_s, conv0_s, state_ssm[l], lw_s, gf, final, PAST_LEN, tm=bs * ls, ts=ls)
        for lst, val in zip(outs_p, (pp, pc, ps, mk.reshape(bp, mem_len, MEM_HEADS, mem_w // MEM_HEADS),
                                     mv.reshape(bp, mem_len, MEM_HEADS, mem_w // MEM_HEADS))):
            lst.append(val)
        for lst, val in zip(outs_s, (sp, sc, ss, sv)):
            lst.append(val)
    return (xp, xs, *(jnp.stack(v) for v in outs_p), *(jnp.stack(v) for v in outs_s))
```

```python
import functools
import math

import jax
import jax.numpy as jnp
from jax import lax
from jax.experimental import pallas as pl
from jax.experimental.pallas import tpu as pltpu

F32 = jnp.float32
BF16 = jnp.bfloat16

EPS = 1e-6
L2_EPS = 1e-6
LANES = 128
MXU_COLS = 256
VMEM_LIMIT_BYTES = 58 * 1024 * 1024

SGU_HEADS = 4
SGU_BLOCK = 64
POOL_WINDOWS = (2, 4, 8, 16)
POOL_HIST = max(POOL_WINDOWS) - 1
POOL_PAD = 16
GDN_HEADS = 4
GDN_CONV = 4
CONV_PAD = 8
GDN_CHUNK = 64
MEM_HEADS = 4
PAST_LEN = 2048
NEG_BIG = -1e30
MIXER_SEQS = 2


def _dot(a, b):
    return jnp.dot(a, b, preferred_element_type=F32)


def _dot_nt(a, b):
    return lax.dot_general(a, b, (((1,), (1,)), ((), ())), preferred_element_type=F32)


def _rms(x, g):
    return x * lax.rsqrt(jnp.mean(x * x, axis=-1, keepdims=True) + EPS) * g


def _sigmoid(x):
    return 1.0 / (1.0 + jnp.exp(-x))


def _gelu_tanh(x):
    return 0.5 * x * (1.0 + jnp.tanh(math.sqrt(2.0 / math.pi) * (x + 0.044715 * (x * x * x))))


def _softplus(x):
    return jnp.maximum(x, 0.0) + jnp.log1p(jnp.exp(-jnp.abs(x)))


def _swiglu(nb_ref, wg_ref, wu_ref, wd_ref, act_ref):
    d_ff = wg_ref.shape[1]
    for c in range(d_ff // MXU_COLS):
        sl = slice(c * MXU_COLS, (c + 1) * MXU_COLS)
        g = _dot(nb_ref[...], wg_ref[:, sl])
        u = _dot(nb_ref[...], wu_ref[:, sl])
        act_ref[:, sl] = (g * _sigmoid(g) * u).astype(BF16)
    return _dot(act_ref[...], wd_ref[...])


def _ffn_in_kernel(x_ref, pool0_ref, conv0_ref, g1_ref, wg_ref, wu_ref, wd_ref, gm_ref, win_ref,
                   sgn_ref, avg_ref, poolw_ref, pools_ref, convw_ref,
                   h_ref, zg_ref, ug_ref, vn_ref, yb_ref, q_ref, k_ref, v_ref, poolst_ref, convst_ref,
                   nb_ref, act_ref, uv_s, pbuf, cbuf, *, splits, sl, tps, pos0):
    i = pl.program_id(0)
    tm = x_ref.shape[0]
    nseq = tm // sl
    sgu_w = sgn_ref.shape[1]
    pool_w = pools_ref.shape[1]
    gdn_w = q_ref.shape[1]
    dk = gdn_w // GDN_HEADS
    o_pool, o_qkv, o_z, n_pad = splits
    iv = i

    @pl.when(i % tps == 0)
    def _():
        pbuf[:, 0:POOL_PAD, :] = pool0_ref[...]
        cbuf[:, 0:CONV_PAD, :] = conv0_ref[...]

    def matmul_stage():
        x = x_ref[...]
        nb_ref[...] = _rms(x, g1_ref[...]).astype(BF16)
        h = x + 0.5 * _swiglu(nb_ref, wg_ref, wu_ref, wd_ref, act_ref)
        h_ref[...] = h
        nb_ref[...] = _rms(h, gm_ref[...]).astype(BF16)
        uv_s[...] = _dot(nb_ref[...], win_ref[:, 0:o_pool])
        zg_ref[...] = _dot(nb_ref[...], win_ref[:, o_z:n_pad])
        p = _dot(nb_ref[...], win_ref[:, o_pool:o_qkv])
        qkv = _dot(nb_ref[...], win_ref[:, o_qkv:o_z])
        for s in range(nseq):
            pbuf[s, POOL_PAD:POOL_PAD + sl, :] = p[s * sl:(s + 1) * sl]
            cbuf[s, CONV_PAD:CONV_PAD + sl, :] = qkv[s * sl:(s + 1) * sl]

    def vector_stage():
        uv = _gelu_tanh(uv_s[...])
        ug_ref[...] = uv[:, 0:sgu_w]
        v = uv[:, sgu_w:]
        avg = avg_ref[...]

        def seg_mean(t):
            hi = t.astype(BF16)
            lo = (t - hi.astype(F32)).astype(BF16)
            return _dot(hi, avg) + _dot(lo, avg)

        vc = v - seg_mean(v)
        vn_ref[...] = vc * lax.rsqrt(seg_mean(vc * vc) + EPS) * sgn_ref[...]

        gdim = pool_w // len(POOL_WINDOWS)
        lane = lax.broadcasted_iota(jnp.int32, (sl, LANES), 1)
        pos = pos0 + (iv % tps) * sl + lax.broadcasted_iota(jnp.int32, (sl, LANES), 0)
        dparts = []
        for s in range(nseq):
            dblk = []
            for blk in range(pool_w // LANES):
                ls = slice(blk * LANES, (blk + 1) * LANES)
                wins = [POOL_WINDOWS[(blk * LANES + l0) // gdim] for l0 in range(0, LANES, gdim)]
                wl = jnp.full((sl, LANES), wins[-1], jnp.int32)
                for gi in range(len(wins) - 2, -1, -1):
                    wl = jnp.where(lane < (gi + 1) * gdim, wins[gi], wl)
                pb = pbuf[s, POOL_PAD:POOL_PAD + sl, ls]
                acc = pb
                prev = 1
                for w in sorted(set(wins)):
                    part = None
                    for jj in range(prev, w):
                        sh = pbuf[s, POOL_PAD - jj:POOL_PAD - jj + sl, ls]
                        part = sh if part is None else part + sh
                    if part is not None:
                        acc = acc + (part if w == min(wins) else jnp.where(wl >= w, part, 0.0))
                    prev = w
                cnt = jnp.minimum(wl, pos + 1).astype(F32)
                dblk.append(acc / cnt - pb)
            dparts.append(jnp.concatenate(dblk, axis=1))
            tail = pbuf[s, sl:sl + POOL_PAD, :]
            poolst_ref[s] = tail
            pbuf[s, 0:POOL_PAD, :] = tail
        dlt = (dparts[0] if nseq == 1 else jnp.concatenate(dparts, axis=0)).astype(BF16)
        yb_ref[...] = (_dot(dlt, poolw_ref[...]) * pools_ref[...]).astype(BF16)

        qscale = dk ** -0.5
        for cb in range(3 * gdn_w // dk):
            cs = slice(cb * dk, (cb + 1) * dk)
            which, hh = divmod(cb, GDN_HEADS)
            hs = slice(hh * dk, (hh + 1) * dk)
            for s in range(nseq):
                rows = slice(s * sl, (s + 1) * sl)
                y = cbuf[s, CONV_PAD:CONV_PAD + sl, cs] * convw_ref[GDN_CONV - 1:GDN_CONV, cs]
                for t in range(1, GDN_CONV):
                    y = y + (cbuf[s, CONV_PAD - t:CONV_PAD - t + sl, cs]
                             * convw_ref[GDN_CONV - 1 - t:GDN_CONV - t, cs])
                y = y * _sigmoid(y)
                if which == 0:
                    q_ref[rows, hs] = y * (lax.rsqrt(jnp.sum(y * y, axis=-1, keepdims=True) + L2_EPS) * qscale)
                elif which == 1:
                    k_ref[rows, hs] = y * lax.rsqrt(jnp.sum(y * y, axis=-1, keepdims=True) + L2_EPS)
                else:
                    v_ref[rows, hs] = y
        for s in range(nseq):
            ctail = cbuf[s, sl:sl + CONV_PAD, :]
            convst_ref[s] = ctail
            cbuf[s, 0:CONV_PAD, :] = ctail

    matmul_stage()
    vector_stage()


def _res(w):
    if isinstance(w, tuple):
        arr, l = w
        nd = arr.ndim - 1
        return pl.BlockSpec((None,) + arr.shape[1:], lambda *_: (l,) + (0,) * nd, pipeline_mode=pl.Buffered(1))
    nd = w.ndim
    return pl.BlockSpec(w.shape, lambda *_: (0,) * nd, pipeline_mode=pl.Buffered(1))


def _arr(w):
    return w[0] if isinstance(w, tuple) else w


def _cast_kernel(x_ref, o_ref):
    w = x_ref.shape[-1]
    o_ref[:, 0:w] = x_ref[...].astype(BF16)
    if o_ref.shape[-1] > w:
        o_ref[:, w:] = jnp.zeros((o_ref.shape[0], o_ref.shape[-1] - w), BF16)


def _cast_bf16(w, cols=None, block_rows=256):
    depth, rows, c = w.shape
    cols = c if cols is None else cols
    br = min(block_rows, rows)
    return pl.pallas_call(
        _cast_kernel,
        grid=(depth, rows // br),
        in_specs=[pl.BlockSpec((None, br, c), lambda l, i: (l, i, 0))],
        out_specs=pl.BlockSpec((None, br, cols), lambda l, i: (l, i, 0)),
        out_shape=jax.ShapeDtypeStruct((depth, rows, cols), BF16),
        compiler_params=pltpu.CompilerParams(dimension_semantics=("parallel", "parallel")),
        name="cast_bf16",
    )(w)


def _ffn_in(x2d, pool0, conv0, lw, tm, seq_len, pos0):
    t, d = x2d.shape
    d_ff = _arr(lw["wg1"]).shape[-1]
    o_pool, o_qkv, o_z, n_pad = lw["splits"]
    sgu_w, pool_w, gdn_w = lw["sgn"].shape[1], lw["pools"].shape[1], lw["onorm"].shape[1]
    sl = min(seq_len, tm)
    nseq = tm // sl
    tps = seq_len // sl
    row = lambda width: pl.BlockSpec((tm, width), lambda i: (i, 0))
    hist = lambda pad, width: pl.BlockSpec((nseq, pad, width), lambda i: (i // tps, 0, 0))
    res = _res
    weights = [lw["g1"], lw["wg1"], lw["wu1"], lw["wd1"], lw["gmix"], lw["win"],
               lw["sgn"], lw["avg"], lw["poolw"], lw["pools"], lw["convw"]]
    outs = [(d, F32), (n_pad - o_z, F32), (sgu_w, F32), (sgu_w, F32), (pool_w, BF16),
            (gdn_w, F32), (gdn_w, F32), (gdn_w, F32)]
    nbatch = t // seq_len
    return pl.pallas_call(
        functools.partial(_ffn_in_kernel, splits=lw["splits"], sl=sl, tps=tps, pos0=pos0),
        grid=(t // tm,),
        in_specs=[row(d), hist(POOL_PAD, pool_w), hist(CONV_PAD, 3 * gdn_w)] + [res(w) for w in weights],
        out_specs=[row(w) for w, _ in outs] + [hist(POOL_PAD, pool_w), hist(CONV_PAD, 3 * gdn_w)],
        out_shape=[jax.ShapeDtypeStruct((t, w), dt) for w, dt in outs]
                  + [jax.ShapeDtypeStruct((nbatch, POOL_PAD, pool_w), F32),
                     jax.ShapeDtypeStruct((nbatch, CONV_PAD, 3 * gdn_w), F32)],
        scratch_shapes=[pltpu.VMEM((tm, d), BF16), pltpu.VMEM((tm, d_ff), BF16),
                        pltpu.VMEM((tm, o_pool), F32),
                        pltpu.VMEM((nseq, POOL_PAD + sl, pool_w), F32),
                        pltpu.VMEM((nseq, CONV_PAD + sl, 3 * gdn_w), F32)],
        compiler_params=pltpu.CompilerParams(dimension_semantics=("arbitrary",),
                                             vmem_limit_bytes=VMEM_LIMIT_BYTES),
        name="ffn_in",
    )(x2d, pool0, conv0, *(_arr(w) for w in weights))


def _mixer_kernel(ug_ref, vn_ref, yb_ref, q_ref, k_ref, v_ref, zg_ref, h_ref, ssm0_ref,
                  sgw_ref, sgb_ref, alog_ref, dtb_ref, onorm_ref, wout_ref,
                  h2_ref, ssm_ref,
                  u_s, wq_s, ql_s, y_s, s_s, *, ts, c_sgu):
    j = pl.program_id(1)
    nj = pl.num_programs(1)
    ns = vn_ref.shape[0]
    sgu_w = vn_ref.shape[2]
    pool_w = yb_ref.shape[2]
    gdn_w = onorm_ref.shape[1]
    dk = gdn_w // GDN_HEADS
    cc = GDN_CHUNK
    nch = ts // cc
    pre_w = sgu_w + pool_w
    seqs = range(ns)

    @pl.when(j == 0)
    def _():
        s_s[...] = ssm0_ref[...]

    hd = sgu_w // SGU_HEADS
    wi = lax.broadcasted_iota(jnp.int32, (c_sgu, SGU_HEADS * c_sgu), 0)
    wj = lax.broadcasted_iota(jnp.int32, (c_sgu, SGU_HEADS * c_sgu), 1) % c_sgu
    wmask = jnp.where(wi // SGU_BLOCK >= wj // SGU_BLOCK, sgw_ref[...], 0.0).astype(BF16)
    lane_head = lax.broadcasted_iota(jnp.int32, (c_sgu, sgu_w), 1) // hd
    for e in seqs:
        for c in range(ts // c_sgu):
            rows = slice(c * c_sgu, (c + 1) * c_sgu)
            vch = vn_ref[e, rows, :]
            vstack = jnp.concatenate([jnp.where(lane_head == hh, vch, 0.0) for hh in range(SGU_HEADS)],
                                     axis=0).astype(BF16)
            s = _dot(wmask, vstack) + sgb_ref[...]
            y_s[e * ts + c * c_sgu:e * ts + (c + 1) * c_sgu, 0:sgu_w] = (ug_ref[e, rows, :] * s).astype(BF16)
        y_s[e * ts:(e + 1) * ts, sgu_w:pre_w] = yb_ref[e]

    ti = lax.broadcasted_iota(jnp.int32, (2 * ts, ts), 0)
    tj = lax.broadcasted_iota(jnp.int32, (2 * ts, ts), 1)
    tr = jnp.where(ti < ts, ti, ti - ts)
    same_chunk = tr // cc == tj // cc
    summat = jnp.where(same_chunk, jnp.where(ti < ts, jnp.where(tj <= tr, 1.0, 0.0), 1.0), 0.0).astype(BF16)
    beta, gc, egc, erg, etot = [], [], [], [], []
    for e in seqs:
        gates = zg_ref[e, :, gdn_w:gdn_w + LANES]
        beta.append(_sigmoid(gates))
        g = -jnp.exp(alog_ref[...]) * _softplus(gates + dtb_ref[...])
        g_hi = g.astype(BF16)
        g_r = g - g_hi.astype(F32)
        g_mid = g_r.astype(BF16)
        g_lo = (g_r - g_mid.astype(F32)).astype(BF16)
        sums = _dot(summat, g_hi) + _dot(summat, g_mid) + _dot(summat, g_lo)
        gc.append(sums[0:ts])
        tot = sums[ts:2 * ts]
        egc.append(jnp.exp(gc[e]))
        erg.append(jnp.exp(tot - gc[e]))
        etot.append(jnp.exp(tot))

    ri = lax.broadcasted_iota(jnp.int32, (cc, cc), 0)
    ci = lax.broadcasted_iota(jnp.int32, (cc, cc), 1)
    incl = ri >= ci
    strict = ri > ci
    chains = [(e, n, hh) for n in range(nch) for e in seqs for hh in range(GDN_HEADS)]
    gct = {(e, n): gc[e][n * cc:(n + 1) * cc].T for e in seqs for n in range(nch)}

    nmats = []
    for e, n, hh in chains:
        rows = slice(n * cc, (n + 1) * cc)
        hs = slice(hh * dk, (hh + 1) * dk)
        gl = GDN_HEADS + hh
        k = k_ref[e, rows, hs]
        kbq = jnp.concatenate([k * beta[e][rows, hh:hh + 1], q_ref[e, rows, hs]], axis=0).astype(BF16)
        prod = _dot_nt(kbq, k.astype(BF16))
        decay = jnp.exp(jnp.where(incl, gc[e][rows, gl:gl + 1] - gct[e, n][gl:gl + 1, :], NEG_BIG))
        nmats.append(jnp.where(strict, -(prod[0:cc] * decay), 0.0))
        ql_s[e, n, hh, 0:cc, :] = (prod[cc:2 * cc] * decay).astype(BF16)
    tps = list(nmats)
    pows = list(nmats)
    for _ in range(5):
        pows = [_dot(m.astype(BF16), m.astype(BF16)) for m in pows]
        tps = [t + m + _dot(t.astype(BF16), m.astype(BF16)) for t, m in zip(tps, pows)]
    for (e, n, hh), tp in zip(chains, tps):
        rows = slice(n * cc, (n + 1) * cc)
        hs = slice(hh * dk, (hh + 1) * dk)
        gl = GDN_HEADS + hh
        k = k_ref[e, rows, hs]
        b = beta[e][rows, hh:hh + 1]
        eg = egc[e][rows, gl:gl + 1]
        rhs = jnp.concatenate([v_ref[e, rows, hs] * b, k * (b * eg)], axis=1)
        sol = rhs + _dot(tp.astype(BF16), rhs.astype(BF16))
        u_s[e, rows, hs] = sol[:, 0:dk]
        wq_s[e, n, hh, 0:cc, :] = sol[:, dk:2 * dk].astype(BF16)
        wq_s[e, n, hh, cc:2 * cc, :] = (q_ref[e, rows, hs] * eg).astype(BF16)
        ql_s[e, n, hh, cc:cc + dk, :] = (k * erg[e][rows, gl:gl + 1]).T.astype(BF16)

    fill = []
    for kb in range(pre_w // MXU_COLS):
        for nb in range(h_ref.shape[2] // MXU_COLS):
            fill.append((slice(kb * MXU_COLS, (kb + 1) * MXU_COLS), slice(nb * MXU_COLS, (nb + 1) * MXU_COLS), kb == 0))
    gaps = 2 * nch

    def run_fill(gap):
        for ks, cs, first in fill[gap * len(fill) // gaps:(gap + 1) * len(fill) // gaps]:
            part = _dot(y_s[:, ks], wout_ref[ks, cs])
            for e in seqs:
                base = h_ref[e, :, cs] if first else h2_ref[e, :, cs]
                h2_ref[e, :, cs] = base + part[e * ts:(e + 1) * ts]

    heads = [(e, hh) for e in seqs for hh in range(GDN_HEADS)]
    for n in range(nch):
        rows = slice(n * cc, (n + 1) * cc)
        sts = [s_s[e, hh] for e, hh in heads]
        r1 = [_dot(wq_s[e, n, hh], st.astype(BF16)) for (e, hh), st in zip(heads, sts)]
        run_fill(2 * n)
        vnew = [u_s[e, rows, hh * dk:(hh + 1) * dk] - r[0:cc] for (e, hh), r in zip(heads, r1)]
        r2 = [_dot(ql_s[e, n, hh], vn.astype(BF16)) for (e, hh), vn in zip(heads, vnew)]
        run_fill(2 * n + 1)
        for (e, hh), st, ra, rb in zip(heads, sts, r1, r2):
            gl = GDN_HEADS + hh
            hs = slice(hh * dk, (hh + 1) * dk)
            glcol = etot[e][rows, gl:gl + 1]
            s_s[e, hh] = st * jnp.concatenate([glcol] * (dk // cc), axis=0) + rb[cc:cc + dk]
            o = ra[cc:2 * cc] + rb[0:cc]
            z = zg_ref[e, rows, hs]
            on = o * lax.rsqrt(jnp.mean(o * o, axis=-1, keepdims=True) + EPS) * onorm_ref[:, hs]
            y_s[e * ts + n * cc:e * ts + (n + 1) * cc, pre_w + hh * dk:pre_w + (hh + 1) * dk] = (
                on * (z * _sigmoid(z))).astype(BF16)

    @pl.when(j == nj - 1)
    def _():
        ssm_ref[...] = s_s[...]

    part = _dot(y_s[:, pre_w:], wout_ref[pre_w:, :])
    for e in seqs:
        h2_ref[e] = h2_ref[e] + part[e * ts:(e + 1) * ts]


def _mixer(h, ug, vn, yb, q, k, v, zg, ssm0, lw, ts, c_sgu, ns):
    b, l, d = h.shape
    sgu_w, pool_w, gdn_w = vn.shape[2], yb.shape[2], q.shape[2]
    dk = gdn_w // GDN_HEADS
    nch = ts // GDN_CHUNK
    mix_w = sgu_w + pool_w + gdn_w
    tile = lambda width: pl.BlockSpec((ns, ts, width), lambda i, j: (i, j, 0))
    per_b = lambda *shape: pl.BlockSpec((ns,) + shape, lambda i, j: (i,) + (0,) * len(shape))
    res = _res
    weights = [lw["sgw"], lw["sgb"], lw["alog"], lw["dtb"], lw["onorm"], lw["wout"]]
    return pl.pallas_call(
        functools.partial(_mixer_kernel, ts=ts, c_sgu=c_sgu),
        grid=(b // ns, l // ts),
        in_specs=[tile(sgu_w), tile(sgu_w), tile(pool_w), tile(gdn_w), tile(gdn_w), tile(gdn_w),
                  tile(zg.shape[2]), tile(d), per_b(GDN_HEADS, dk, dk)] + [res(w) for w in weights],
        out_specs=[tile(d), per_b(GDN_HEADS, dk, dk)],
        out_shape=[jax.ShapeDtypeStruct((b, l, d), F32), jax.ShapeDtypeStruct((b, GDN_HEADS, dk, dk), F32)],
        scratch_shapes=[
            pltpu.VMEM((ns, ts, gdn_w), F32),
            pltpu.VMEM((ns, nch, GDN_HEADS, 2 * GDN_CHUNK, dk), BF16),
            pltpu.VMEM((ns, nch, GDN_HEADS, GDN_CHUNK + dk, GDN_CHUNK), BF16),
            pltpu.VMEM((ns * ts, mix_w), BF16),
            pltpu.VMEM((ns, GDN_HEADS, dk, dk), F32),
        ],
        compiler_params=pltpu.CompilerParams(dimension_semantics=("parallel", "arbitrary"),
                                             vmem_limit_bytes=VMEM_LIMIT_BYTES),
        name="mixer",
    )(ug, vn, yb, q, k, v, zg, h, ssm0, *(_arr(w) for w in weights))


def _attn_ffn_kernel(h_ref, mk_ref, mv_ref, gx_ref, wq_ref, wo_ref, g2_ref, wg_ref, wu_ref, wd_ref, gf_ref,
                     out_ref, nb_ref, act_ref, ob_ref, kh_s, vh_s, *, final, sl, tps):
    nseq = h_ref.shape[0] // sl
    heads, dh = mk_ref.shape[2], mk_ref.shape[3]

    @pl.when(pl.program_id(0) % tps == 0)
    def _():
        for s in range(nseq):
            for hh in range(heads):
                kh_s[s, hh] = mk_ref[s, :, hh, :].astype(BF16)
                vh_s[s, hh] = mv_ref[s, :, hh, :].astype(BF16)

    h = h_ref[...]
    nb_ref[...] = _rms(h, gx_ref[...]).astype(BF16)
    q = _dot(nb_ref[...], wq_ref[...])
    scale = dh ** -0.5
    pairs = [(s, hh, slice(s * sl, (s + 1) * sl), slice(hh * dh, (hh + 1) * dh))
             for s in range(nseq) for hh in range(heads)]
    scs = [_dot_nt(q[rows, hs].astype(BF16), kh_s[s, hh]) * scale for s, hh, rows, hs in pairs]
    es = [jnp.exp(sc - jnp.max(sc, axis=-1, keepdims=True)) for sc in scs]
    prs = [e / jnp.sum(e, axis=-1, keepdims=True) for e in es]
    for (s, hh, rows, hs), pr in zip(pairs, prs):
        ob_ref[rows, hs] = _dot(pr.astype(BF16), vh_s[s, hh]).astype(BF16)
    h = h + _dot(ob_ref[...], wo_ref[...])
    nb_ref[...] = _rms(h, g2_ref[...]).astype(BF16)
    h = h + 0.5 * _swiglu(nb_ref, wg_ref, wu_ref, wd_ref, act_ref)
    if final:
        h = _rms(h, gf_ref[...])
    out_ref[...] = h


def _attn_ffn(h2d, mk, mv, l, lw, gf, tm, seq_len, final):
    t, d = h2d.shape
    d_ff = _arr(lw["wg2"]).shape[-1]
    m, heads, dh = mk.shape[2:]
    sl = min(seq_len, tm)
    nseq = tm // sl
    tps = seq_len // sl
    tile = pl.BlockSpec((tm, d), lambda i: (i, 0))
    mem = pl.BlockSpec((None, nseq, m, heads, dh), lambda i: (l, i // tps, 0, 0, 0))
    weights = [lw["gx"], lw["wmq"], lw["wmo"], lw["g2"], lw["wg2"], lw["wu2"], lw["wd2"], gf]
    return pl.pallas_call(
        functools.partial(_attn_ffn_kernel, final=final, sl=sl, tps=tps),
        grid=(t // tm,),
        in_specs=[tile, mem, mem] + [_res(w) for w in weights],
        out_specs=tile,
        out_shape=jax.ShapeDtypeStruct((t, d), F32),
        scratch_shapes=[pltpu.VMEM((tm, d), BF16), pltpu.VMEM((tm, d_ff), BF16), pltpu.VMEM((tm, heads * dh), BF16),
                        pltpu.VMEM((nseq, heads, m, dh), BF16), pltpu.VMEM((nseq, heads, m, dh), BF16)],
        compiler_params=pltpu.CompilerParams(dimension_semantics=("arbitrary",),
                                             vmem_limit_bytes=VMEM_LIMIT_BYTES),
        name="attn_ffn",
    )(h2d, mk, mv, *(_arr(w) for w in weights))


def _mem_kv_kernel(mem_ref, g_ref, wk_ref, wv_ref, k_ref, v_ref):
    mb = _rms(mem_ref[...], g_ref[...]).astype(BF16)
    k = _dot(mb, wk_ref[...])
    v = _dot(mb, wv_ref[...])
    heads, dh = k_ref.shape[1], k_ref.shape[2]
    for hh in range(heads):
        k_ref[:, hh, :] = k[:, hh * dh:(hh + 1) * dh]
        v_ref[:, hh, :] = v[:, hh * dh:(hh + 1) * dh]


def _mem_kv(mem, g, wk, wv, heads):
    b, m, d = mem.shape
    depth, _, mw = wk.shape
    per_layer = lambda *shape: pl.BlockSpec((None,) + shape, lambda l, i: (l,) + (0,) * len(shape))
    out = pl.BlockSpec((None, None, m, heads, mw // heads), lambda l, i: (l, i, 0, 0, 0))
    return pl.pallas_call(
        _mem_kv_kernel,
        grid=(depth, b),
        in_specs=[pl.BlockSpec((None, m, d), lambda l, i: (i, 0, 0)), per_layer(1, d), per_layer(d, mw),
                  per_layer(d, mw)],
        out_specs=[out, out],
        out_shape=[jax.ShapeDtypeStruct((depth, b, m, heads, mw // heads), F32)] * 2,
        compiler_params=pltpu.CompilerParams(dimension_semantics=("parallel", "parallel")),
        name="mem_kv",
    )(mem, g.reshape(depth, 1, d), wk, wv)


def _layer_weights(l, c_sgu, big, ffn1_norm, mix_norm, sgu_norm, sgu_w, sgu_b, pool_w, pool_scale, gdn_conv_w,
                   gdn_a_log, gdn_dt_bias, gdn_out_norm, xattn_norm, ffn2_norm):
    sgu_width = sgu_norm.shape[1]
    pool_width = pool_scale.shape[1]
    gdn_width = gdn_out_norm.shape[1] * GDN_HEADS
    row = lambda vec: vec.reshape(1, -1).astype(F32)
    groups = len(POOL_WINDOWS)
    gdim = pool_width // groups
    poolw = jnp.zeros((pool_width, pool_width), F32)
    for gi in range(groups):
        poolw = poolw.at[gi * gdim:(gi + 1) * gdim, gi * gdim:(gi + 1) * gdim].set(pool_w[l, gi])
    hd = sgu_width // SGU_HEADS
    seg = jnp.arange(sgu_width) // hd
    avg = jnp.where(seg[:, None] == seg[None, :], 1.0 / hd, 0.0).astype(BF16)
    sgw = jnp.concatenate([sgu_w[l, hh, :c_sgu, :c_sgu] for hh in range(SGU_HEADS)], axis=1)
    sgb = jnp.repeat(sgu_b[l, :, :c_sgu].T, hd, axis=1)
    lane_pad = lambda vec: jnp.pad(vec, (GDN_HEADS, LANES - 2 * GDN_HEADS)).reshape(1, LANES)
    lw = {name: (arr, l) for name, arr in big.items()}
    lw.update(
        splits=_in_splits(sgu_width, pool_width, gdn_width),
        g1=row(ffn1_norm[l]), gmix=row(mix_norm[l]),
        sgw=sgw, sgb=sgb, sgn=row(sgu_norm[l]), avg=avg,
        poolw=poolw.astype(BF16), pools=row(pool_scale[l]),
        convw=gdn_conv_w[l], alog=lane_pad(gdn_a_log[l]), dtb=lane_pad(gdn_dt_bias[l]),
        onorm=row(jnp.tile(gdn_out_norm[l], GDN_HEADS)),
        gx=row(xattn_norm[l]), g2=row(ffn2_norm[l]),
    )
    return lw


def _in_splits(sgu_width, pool_width, gdn_width):
    o_pool = 2 * sgu_width
    o_qkv = o_pool + pool_width
    o_z = o_qkv + 3 * gdn_width
    return o_pool, o_qkv, o_z, o_z + gdn_width + LANES


def _layer(x, mk, mv, l, pool0, conv0, ssm0, lw, gf, final, pos0, tm, ts):
    b, seq, d = x.shape
    h, zg, ug, vn, yb, q, k, v, pool_new, conv_new = _ffn_in(x.reshape(b * seq, d), pool0, conv0, lw, tm, seq, pos0)
    shp = lambda arr: arr.reshape(b, seq, arr.shape[-1])
    c_sgu = lw["sgb"].shape[0]
    h2, ssm_new = _mixer(shp(h), shp(ug), shp(vn), shp(yb), shp(q), shp(k), shp(v), shp(zg), ssm0, lw, ts, c_sgu,
                         MIXER_SEQS)
    out = _attn_ffn(h2.reshape(b * seq, d), mk, mv, l, lw, gf, tm, seq, final)
    return (shp(out), shp(vn), pool_new[:, POOL_PAD - POOL_HIST:], conv_new[:, CONV_PAD - (GDN_CONV - 1):], ssm_new)


def kernel(x_prompt, x_sample, mem_prompt, cache_mem_k, cache_mem_v, state_pool, state_conv, state_ssm, ffn1_norm, ffn1_w_gate, ffn1_w_up, ffn1_w_down, mix_norm, w_in, sgu_norm, sgu_w, sgu_b, pool_w, pool_scale, gdn_conv_w, gdn_a_log, gdn_dt_bias, gdn_out_norm, w_out, xattn_norm, mem_norm, w_mq, w_mk, w_mv, w_mo, ffn2_norm, ffn2_w_gate, ffn2_w_up, ffn2_w_down, final_norm):
    depth = w_in.shape[0]
    bp, lp, d = x_prompt.shape
    bs, ls, _ = x_sample.shape
    pool_width = state_pool.shape[-1]
    qkv_width = state_conv.shape[-1]
    dk = state_ssm.shape[-1]
    n_pad = _in_splits(sgu_norm.shape[1], pool_width, gdn_out_norm.shape[1] * GDN_HEADS)[3]
    big = dict(wg1=_cast_bf16(ffn1_w_gate), wu1=_cast_bf16(ffn1_w_up), wd1=_cast_bf16(ffn1_w_down),
               win=_cast_bf16(w_in, cols=n_pad), wout=_cast_bf16(w_out), wmq=_cast_bf16(w_mq),
               wmo=_cast_bf16(w_mo), wg2=_cast_bf16(ffn2_w_gate), wu2=_cast_bf16(ffn2_w_up),
               wd2=_cast_bf16(ffn2_w_down))
    small = (ffn1_norm, mix_norm, sgu_norm, sgu_w, sgu_b, pool_w, pool_scale, gdn_conv_w, gdn_a_log, gdn_dt_bias,
             gdn_out_norm, xattn_norm, ffn2_norm)
    gf = final_norm.reshape(1, d)
    prompt_mk, prompt_mv = _mem_kv(mem_prompt, mem_norm, _cast_bf16(w_mk), _cast_bf16(w_mv), cache_mem_k.shape[3])
    pool0_p = jnp.zeros((bp, POOL_PAD, pool_width), F32)
    conv0_p = jnp.zeros((bp, CONV_PAD, qkv_width), F32)
    ssm0_p = jnp.zeros((bp, GDN_HEADS, dk, dk), F32)
    xp, xs = x_prompt, x_sample
    outs_p = [[] for _ in range(3)]
    outs_s = [[] for _ in range(4)]
    for l in range(depth):
        final = l == depth - 1
        lw_p = _layer_weights(l, min(128, lp), big, *small)
        lw_s = lw_p if min(128, ls) == min(128, lp) else _layer_weights(l, min(128, ls), big, *small)
        xp, _, pp, pc, ps = _layer(xp, prompt_mk, prompt_mv, l, pool0_p, conv0_p, ssm0_p, lw_p, gf, final, 0,
                                   tm=512, ts=256)
        pool0_s = jnp.pad(state_pool[l], ((0, 0), (POOL_PAD - POOL_HIST, 0), (0, 0)))
        conv0_s = jnp.pad(state_conv[l], ((0, 0), (CONV_PAD - (GDN_CONV - 1), 0), (0, 0)))
        xs, sv, sp, sc, ss = _layer(xs, cache_mem_k, cache_mem_v, l, pool0_s, conv0_s, state_ssm[l], lw_s, gf, final,
                                    PAST_LEN, tm=bs * ls, ts=ls)
        for lst, val in zip(outs_p, (pp, pc, ps)):
            lst.append(val)
        for lst, val in zip(outs_s, (sp, sc, ss, sv)):
            lst.append(val)
    return (xp, xs, *(jnp.stack(v) for v in outs_p), prompt_mk, prompt_mv, *(jnp.stack(v) for v in outs_s))
```

```python
import functools
import math

import jax
import jax.numpy as jnp
from jax import lax
from jax.experimental import pallas as pl
from jax.experimental.pallas import tpu as pltpu

F32 = jnp.float32
BF16 = jnp.bfloat16

EPS = 1e-6
L2_EPS = 1e-6
LANES = 128
MXU_COLS = 256
VMEM_LIMIT_BYTES = 58 * 1024 * 1024

SGU_HEADS = 4
SGU_BLOCK = 64
POOL_WINDOWS = (2, 4, 8, 16)
POOL_HIST = max(POOL_WINDOWS) - 1
POOL_PAD = 16
GDN_HEADS = 4
GDN_CONV = 4
CONV_PAD = 8
CONV_PHASES = 4
GDN_CHUNK = 64
MEM_HEADS = 4
PAST_LEN = 2048
NEG_BIG = -1e30
MIXER_SEQS = 2


def _dot(a, b):
    return jnp.dot(a, b, preferred_element_type=F32)


def _dot_nt(a, b):
    return lax.dot_general(a, b, (((1,), (1,)), ((), ())), preferred_element_type=F32)


def _rms(x, g):
    return x * lax.rsqrt(jnp.mean(x * x, axis=-1, keepdims=True) + EPS) * g


def _sigmoid(x):
    return 1.0 / (1.0 + jnp.exp(-x))


def _gelu_tanh(x):
    return 0.5 * x * (1.0 + jnp.tanh(math.sqrt(2.0 / math.pi) * (x + 0.044715 * (x * x * x))))


def _softplus(x):
    return jnp.maximum(x, 0.0) + jnp.log1p(jnp.exp(-jnp.abs(x)))


def _swiglu(nb_ref, wg_ref, wu_ref, wd_ref, act_ref):
    d_ff = wg_ref.shape[1]
    for c in range(d_ff // MXU_COLS):
        sl = slice(c * MXU_COLS, (c + 1) * MXU_COLS)
        g = _dot(nb_ref[...], wg_ref[:, sl])
        u = _dot(nb_ref[...], wu_ref[:, sl])
        act_ref[:, sl] = (g * _sigmoid(g) * u).astype(BF16)
    return _dot(act_ref[...], wd_ref[...])


def _ffn_in_kernel(x_ref, pool0_ref, conv0_ref, g1_ref, wg_ref, wu_ref, wd_ref, gm_ref, win_ref,
                   sgn_ref, avg_ref, poolw_ref, pools_ref, convw_ref,
                   h_ref, zg_ref, ug_ref, vn_ref, yb_ref, q_ref, k_ref, v_ref, poolst_ref, convst_ref,
                   nb_ref, act_ref, uv_s, pbuf, cbuf, *, splits, sl, tps, pos0):
    i = pl.program_id(0)
    tm = x_ref.shape[0]
    nseq = tm // sl
    sgu_w = sgn_ref.shape[1]
    pool_w = pools_ref.shape[1]
    dk = q_ref.shape[2]
    gdn_w = GDN_HEADS * dk
    ncb = 3 * GDN_HEADS
    o_pool, o_qkv, o_z, n_pad = splits
    iv = i

    @pl.when(i % tps == 0)
    def _():
        pbuf[:, 0:POOL_PAD, :] = pool0_ref[...]
        for cb in range(ncb):
            cbuf[:, cb, 0:CONV_PAD, :] = conv0_ref[:, :, cb * dk:(cb + 1) * dk]

    def matmul_stage():
        x = x_ref[...]
        nb_ref[...] = _rms(x, g1_ref[...]).astype(BF16)
        h = x + 0.5 * _swiglu(nb_ref, wg_ref, wu_ref, wd_ref, act_ref)
        h_ref[...] = h
        nb_ref[...] = _rms(h, gm_ref[...]).astype(BF16)
        uv_s[...] = _dot(nb_ref[...], win_ref[:, 0:o_pool])
        zg_ref[...] = _dot(nb_ref[...], win_ref[:, o_z:n_pad])
        p = _dot(nb_ref[...], win_ref[:, o_pool:o_qkv])
        qkv = _dot(nb_ref[...], win_ref[:, o_qkv:o_z])
        for s in range(nseq):
            pbuf[s, POOL_PAD:POOL_PAD + sl, :] = p[s * sl:(s + 1) * sl]
            for cb in range(ncb):
                cbuf[s, cb, CONV_PAD:CONV_PAD + sl, :] = qkv[s * sl:(s + 1) * sl, cb * dk:(cb + 1) * dk]

    def vector_stage():
        uv = _gelu_tanh(uv_s[...])
        ug_ref[...] = uv[:, 0:sgu_w]
        v = uv[:, sgu_w:]
        avg = avg_ref[...]

        def seg_mean(t):
            hi = t.astype(BF16)
            lo = (t - hi.astype(F32)).astype(BF16)
            return _dot(hi, avg) + _dot(lo, avg)

        vc = v - seg_mean(v)
        vn_ref[...] = vc * lax.rsqrt(seg_mean(vc * vc) + EPS) * sgn_ref[...]

        gdim = pool_w // len(POOL_WINDOWS)
        lane = lax.broadcasted_iota(jnp.int32, (sl, LANES), 1)
        pos = pos0 + (iv % tps) * sl + lax.broadcasted_iota(jnp.int32, (sl, LANES), 0)
        dparts = []
        for s in range(nseq):
            dblk = []
            for blk in range(pool_w // LANES):
                ls = slice(blk * LANES, (blk + 1) * LANES)
                wins = [POOL_WINDOWS[(blk * LANES + l0) // gdim] for l0 in range(0, LANES, gdim)]
                wl = jnp.full((sl, LANES), wins[-1], jnp.int32)
                for gi in range(len(wins) - 2, -1, -1):
                    wl = jnp.where(lane < (gi + 1) * gdim, wins[gi], wl)
                pb = pbuf[s, POOL_PAD:POOL_PAD + sl, ls]
                acc = pb
                prev = 1
                for w in sorted(set(wins)):
                    part = None
                    for jj in range(prev, w):
                        sh = pbuf[s, POOL_PAD - jj:POOL_PAD - jj + sl, ls]
                        part = sh if part is None else part + sh
                    if part is not None:
                        acc = acc + (part if w == min(wins) else jnp.where(wl >= w, part, 0.0))
                    prev = w
                cnt = jnp.minimum(wl, pos + 1).astype(F32)
                dblk.append(acc / cnt - pb)
            dparts.append(jnp.concatenate(dblk, axis=1))
            tail = pbuf[s, sl:sl + POOL_PAD, :]
            poolst_ref[s] = tail
            pbuf[s, 0:POOL_PAD, :] = tail
        dlt = (dparts[0] if nseq == 1 else jnp.concatenate(dparts, axis=0)).astype(BF16)
        yb_ref[...] = (_dot(dlt, poolw_ref[...]) * pools_ref[...]).astype(BF16)

        qscale = dk ** -0.5
        nrow = sl // CONV_PHASES
        for cb in range(ncb):
            cs = slice(cb * dk, (cb + 1) * dk)
            which, hh = divmod(cb, GDN_HEADS)
            for s in range(nseq):
                for r in range(CONV_PHASES):
                    y = None
                    for t in range(GDN_CONV):
                        tap = (cbuf[s, cb, pl.ds(CONV_PAD + r - t, nrow, stride=CONV_PHASES), :]
                               * convw_ref[GDN_CONV - 1 - t:GDN_CONV - t, cs])
                        y = tap if y is None else y + tap
                    y = y * _sigmoid(y)
                    out_rows = pl.ds(s * sl + r, nrow, stride=CONV_PHASES)
                    if which == 0:
                        q_ref[hh, out_rows, :] = y * (lax.rsqrt(jnp.sum(y * y, axis=-1, keepdims=True) + L2_EPS)
                                                      * qscale)
                    elif which == 1:
                        k_ref[hh, out_rows, :] = y * lax.rsqrt(jnp.sum(y * y, axis=-1, keepdims=True) + L2_EPS)
                    else:
                        v_ref[hh, out_rows, :] = y
        for s in range(nseq):
            for cb in range(ncb):
                ctail = cbuf[s, cb, sl:sl + CONV_PAD, :]
                convst_ref[s, :, cb * dk:(cb + 1) * dk] = ctail
                cbuf[s, cb, 0:CONV_PAD, :] = ctail

    matmul_stage()
    vector_stage()


def _res(w):
    if isinstance(w, tuple):
        arr, l = w
        nd = arr.ndim - 1
        return pl.BlockSpec((None,) + arr.shape[1:], lambda *_: (l,) + (0,) * nd, pipeline_mode=pl.Buffered(1))
    nd = w.ndim
    return pl.BlockSpec(w.shape, lambda *_: (0,) * nd, pipeline_mode=pl.Buffered(1))


def _arr(w):
    return w[0] if isinstance(w, tuple) else w


def _cast_kernel(x_ref, o_ref):
    w = x_ref.shape[-1]
    o_ref[:, 0:w] = x_ref[...].astype(BF16)
    if o_ref.shape[-1] > w:
        o_ref[:, w:] = jnp.zeros((o_ref.shape[0], o_ref.shape[-1] - w), BF16)


def _cast_bf16(w, cols=None, block_rows=256):
    depth, rows, c = w.shape
    cols = c if cols is None else cols
    br = min(block_rows, rows)
    return pl.pallas_call(
        _cast_kernel,
        grid=(depth, rows // br),
        in_specs=[pl.BlockSpec((None, br, c), lambda l, i: (l, i, 0))],
        out_specs=pl.BlockSpec((None, br, cols), lambda l, i: (l, i, 0)),
        out_shape=jax.ShapeDtypeStruct((depth, rows, cols), BF16),
        compiler_params=pltpu.CompilerParams(dimension_semantics=("parallel", "parallel")),
        name="cast_bf16",
    )(w)


def _ffn_in(x2d, pool0, conv0, lw, tm, seq_len, pos0):
    t, d = x2d.shape
    d_ff = _arr(lw["wg1"]).shape[-1]
    o_pool, o_qkv, o_z, n_pad = lw["splits"]
    sgu_w, pool_w, gdn_w = lw["sgn"].shape[1], lw["pools"].shape[1], lw["onorm"].shape[1]
    sl = min(seq_len, tm)
    nseq = tm // sl
    tps = seq_len // sl
    row = lambda width: pl.BlockSpec((tm, width), lambda i: (i, 0))
    hist = lambda pad, width: pl.BlockSpec((nseq, pad, width), lambda i: (i // tps, 0, 0))
    res = _res
    weights = [lw["g1"], lw["wg1"], lw["wu1"], lw["wd1"], lw["gmix"], lw["win"],
               lw["sgn"], lw["avg"], lw["poolw"], lw["pools"], lw["convw"]]
    outs = [(d, F32), (n_pad - o_z, F32), (sgu_w, F32), (sgu_w, F32), (pool_w, BF16)]
    dk = gdn_w // GDN_HEADS
    head_rows = pl.BlockSpec((GDN_HEADS, tm, dk), lambda i: (0, i, 0))
    nbatch = t // seq_len
    return pl.pallas_call(
        functools.partial(_ffn_in_kernel, splits=lw["splits"], sl=sl, tps=tps, pos0=pos0),
        grid=(t // tm,),
        in_specs=[row(d), hist(POOL_PAD, pool_w), hist(CONV_PAD, 3 * gdn_w)] + [res(w) for w in weights],
        out_specs=[row(w) for w, _ in outs] + [head_rows] * 3 + [hist(POOL_PAD, pool_w), hist(CONV_PAD, 3 * gdn_w)],
        out_shape=[jax.ShapeDtypeStruct((t, w), dt) for w, dt in outs]
                  + [jax.ShapeDtypeStruct((GDN_HEADS, t, dk), F32)] * 3
                  + [jax.ShapeDtypeStruct((nbatch, POOL_PAD, pool_w), F32),
                     jax.ShapeDtypeStruct((nbatch, CONV_PAD, 3 * gdn_w), F32)],
        scratch_shapes=[pltpu.VMEM((tm, d), BF16), pltpu.VMEM((tm, d_ff), BF16),
                        pltpu.VMEM((tm, o_pool), F32),
                        pltpu.VMEM((nseq, POOL_PAD + sl, pool_w), F32),
                        pltpu.VMEM((nseq, 3 * GDN_HEADS, CONV_PAD + sl, dk), F32)],
        compiler_params=pltpu.CompilerParams(dimension_semantics=("arbitrary",),
                                             vmem_limit_bytes=VMEM_LIMIT_BYTES),
        name="ffn_in",
    )(x2d, pool0, conv0, *(_arr(w) for w in weights))


def _mixer_kernel(ug_ref, vn_ref, yb_ref, q_ref, k_ref, v_ref, zg_ref, h_ref, ssm0_ref,
                  sgw_ref, sgb_ref, alog_ref, dtb_ref, onorm_ref, wout_ref,
                  h2_ref, ssm_ref,
                  u_s, wq_s, ql_s, y_s, s_s, *, ts, c_sgu):
    j = pl.program_id(1)
    nj = pl.num_programs(1)
    ns = vn_ref.shape[0]
    sgu_w = vn_ref.shape[2]
    pool_w = yb_ref.shape[2]
    gdn_w = onorm_ref.shape[1]
    dk = gdn_w // GDN_HEADS
    cc = GDN_CHUNK
    nch = ts // cc
    pre_w = sgu_w + pool_w
    seqs = range(ns)

    @pl.when(j == 0)
    def _():
        s_s[...] = ssm0_ref[...]

    hd = sgu_w // SGU_HEADS
    wi = lax.broadcasted_iota(jnp.int32, (c_sgu, SGU_HEADS * c_sgu), 0)
    wj = lax.broadcasted_iota(jnp.int32, (c_sgu, SGU_HEADS * c_sgu), 1) % c_sgu
    wmask = jnp.where(wi // SGU_BLOCK >= wj // SGU_BLOCK, sgw_ref[...], 0.0).astype(BF16)
    lane_head = lax.broadcasted_iota(jnp.int32, (c_sgu, sgu_w), 1) // hd
    for e in seqs:
        for c in range(ts // c_sgu):
            rows = slice(c * c_sgu, (c + 1) * c_sgu)
            vch = vn_ref[e, rows, :]
            vstack = jnp.concatenate([jnp.where(lane_head == hh, vch, 0.0) for hh in range(SGU_HEADS)],
                                     axis=0).astype(BF16)
            s = _dot(wmask, vstack) + sgb_ref[...]
            y_s[e * ts + c * c_sgu:e * ts + (c + 1) * c_sgu, 0:sgu_w] = (ug_ref[e, rows, :] * s).astype(BF16)
        y_s[e * ts:(e + 1) * ts, sgu_w:pre_w] = yb_ref[e]

    ti = lax.broadcasted_iota(jnp.int32, (2 * ts, ts), 0)
    tj = lax.broadcasted_iota(jnp.int32, (2 * ts, ts), 1)
    tr = jnp.where(ti < ts, ti, ti - ts)
    same_chunk = tr // cc == tj // cc
    summat = jnp.where(same_chunk, jnp.where(ti < ts, jnp.where(tj <= tr, 1.0, 0.0), 1.0), 0.0).astype(BF16)
    beta, gc, egc, erg, etot = [], [], [], [], []
    for e in seqs:
        gates = zg_ref[e, :, gdn_w:gdn_w + LANES]
        beta.append(_sigmoid(gates))
        g = -jnp.exp(alog_ref[...]) * _softplus(gates + dtb_ref[...])
        g_hi = g.astype(BF16)
        g_r = g - g_hi.astype(F32)
        g_mid = g_r.astype(BF16)
        g_lo = (g_r - g_mid.astype(F32)).astype(BF16)
        sums = _dot(summat, g_hi) + _dot(summat, g_mid) + _dot(summat, g_lo)
        gc.append(sums[0:ts])
        tot = sums[ts:2 * ts]
        egc.append(jnp.exp(gc[e]))
        erg.append(jnp.exp(tot - gc[e]))
        etot.append(jnp.exp(tot))

    ri = lax.broadcasted_iota(jnp.int32, (cc, cc), 0)
    ci = lax.broadcasted_iota(jnp.int32, (cc, cc), 1)
    incl = ri >= ci
    strict = ri > ci
    chains = [(e, n, hh) for n in range(nch) for e in seqs for hh in range(GDN_HEADS)]
    gct = {(e, n): gc[e][n * cc:(n + 1) * cc].T for e in seqs for n in range(nch)}

    nmats = []
    for e, n, hh in chains:
        rows = slice(n * cc, (n + 1) * cc)
        hs = slice(hh * dk, (hh + 1) * dk)
        gl = GDN_HEADS + hh
        k = k_ref[hh, e, rows, :]
        kbq = jnp.concatenate([k * beta[e][rows, hh:hh + 1], q_ref[hh, e, rows, :]], axis=0).astype(BF16)
        prod = _dot_nt(kbq, k.astype(BF16))
        decay = jnp.exp(jnp.where(incl, gc[e][rows, gl:gl + 1] - gct[e, n][gl:gl + 1, :], NEG_BIG))
        nmats.append(jnp.where(strict, -(prod[0:cc] * decay), 0.0))
        ql_s[e, n, hh, 0:cc, :] = (prod[cc:2 * cc] * decay).astype(BF16)
    tps = list(nmats)
    pows = list(nmats)
    for _ in range(5):
        pows = [_dot(m.astype(BF16), m.astype(BF16)) for m in pows]
        tps = [t + m + _dot(t.astype(BF16), m.astype(BF16)) for t, m in zip(tps, pows)]
    for (e, n, hh), tp in zip(chains, tps):
        rows = slice(n * cc, (n + 1) * cc)
        hs = slice(hh * dk, (hh + 1) * dk)
        gl = GDN_HEADS + hh
        k = k_ref[hh, e, rows, :]
        b = beta[e][rows, hh:hh + 1]
        eg = egc[e][rows, gl:gl + 1]
        rhs = jnp.concatenate([v_ref[hh, e, rows, :] * b, k * (b * eg)], axis=1)
        sol = rhs + _dot(tp.astype(BF16), rhs.astype(BF16))
        u_s[e, rows, hs] = sol[:, 0:dk]
        wq_s[e, n, hh, 0:cc, :] = sol[:, dk:2 * dk].astype(BF16)
        wq_s[e, n, hh, cc:2 * cc, :] = (q_ref[hh, e, rows, :] * eg).astype(BF16)
        ql_s[e, n, hh, cc:cc + dk, :] = (k * erg[e][rows, gl:gl + 1]).T.astype(BF16)

    fill = []
    for kb in range(pre_w // MXU_COLS):
        for nb in range(h_ref.shape[2] // MXU_COLS):
            fill.append((slice(kb * MXU_COLS, (kb + 1) * MXU_COLS), slice(nb * MXU_COLS, (nb + 1) * MXU_COLS), kb == 0))
    gaps = 2 * nch

    def run_fill(gap):
        for ks, cs, first in fill[gap * len(fill) // gaps:(gap + 1) * len(fill) // gaps]:
            part = _dot(y_s[:, ks], wout_ref[ks, cs])
            for e in seqs:
                base = h_ref[e, :, cs] if first else h2_ref[e, :, cs]
                h2_ref[e, :, cs] = base + part[e * ts:(e + 1) * ts]

    heads = [(e, hh) for e in seqs for hh in range(GDN_HEADS)]
    for n in range(nch):
        rows = slice(n * cc, (n + 1) * cc)
        sts = [s_s[e, hh] for e, hh in heads]
        r1 = [_dot(wq_s[e, n, hh], st.astype(BF16)) for (e, hh), st in zip(heads, sts)]
        run_fill(2 * n)
        vnew = [u_s[e, rows, hh * dk:(hh + 1) * dk] - r[0:cc] for (e, hh), r in zip(heads, r1)]
        r2 = [_dot(ql_s[e, n, hh], vn.astype(BF16)) for (e, hh), vn in zip(heads, vnew)]
        run_fill(2 * n + 1)
        for (e, hh), st, ra, rb in zip(heads, sts, r1, r2):
            gl = GDN_HEADS + hh
            hs = slice(hh * dk, (hh + 1) * dk)
            glcol = etot[e][rows, gl:gl + 1]
            s_s[e, hh] = st * jnp.concatenate([glcol] * (dk // cc), axis=0) + rb[cc:cc + dk]
            o = ra[cc:2 * cc] + rb[0:cc]
            z = zg_ref[e, rows, hs]
            on = o * lax.rsqrt(jnp.mean(o * o, axis=-1, keepdims=True) + EPS) * onorm_ref[:, hs]
            y_s[e * ts + n * cc:e * ts + (n + 1) * cc, pre_w + hh * dk:pre_w + (hh + 1) * dk] = (
                on * (z * _sigmoid(z))).astype(BF16)

    @pl.when(j == nj - 1)
    def _():
        ssm_ref[...] = s_s[...]

    part = _dot(y_s[:, pre_w:], wout_ref[pre_w:, :])
    for e in seqs:
        h2_ref[e] = h2_ref[e] + part[e * ts:(e + 1) * ts]


def _mixer(h, ug, vn, yb, q, k, v, zg, ssm0, lw, ts, c_sgu, ns):
    b, l, d = h.shape
    sgu_w, pool_w, dk = vn.shape[2], yb.shape[2], q.shape[3]
    gdn_w = GDN_HEADS * dk
    nch = ts // GDN_CHUNK
    mix_w = sgu_w + pool_w + gdn_w
    tile = lambda width: pl.BlockSpec((ns, ts, width), lambda i, j: (i, j, 0))
    per_b = lambda *shape: pl.BlockSpec((ns,) + shape, lambda i, j: (i,) + (0,) * len(shape))
    heads_tile = pl.BlockSpec((GDN_HEADS, ns, ts, dk), lambda i, j: (0, i, j, 0))
    res = _res
    weights = [lw["sgw"], lw["sgb"], lw["alog"], lw["dtb"], lw["onorm"], lw["wout"]]
    return pl.pallas_call(
        functools.partial(_mixer_kernel, ts=ts, c_sgu=c_sgu),
        grid=(b // ns, l // ts),
        in_specs=[tile(sgu_w), tile(sgu_w), tile(pool_w), heads_tile, heads_tile, heads_tile,
                  tile(zg.shape[2]), tile(d), per_b(GDN_HEADS, dk, dk)] + [res(w) for w in weights],
        out_specs=[tile(d), per_b(GDN_HEADS, dk, dk)],
        out_shape=[jax.ShapeDtypeStruct((b, l, d), F32), jax.ShapeDtypeStruct((b, GDN_HEADS, dk, dk), F32)],
        scratch_shapes=[
            pltpu.VMEM((ns, ts, gdn_w), F32),
            pltpu.VMEM((ns, nch, GDN_HEADS, 2 * GDN_CHUNK, dk), BF16),
            pltpu.VMEM((ns, nch, GDN_HEADS, GDN_CHUNK + dk, GDN_CHUNK), BF16),
            pltpu.VMEM((ns * ts, mix_w), BF16),
            pltpu.VMEM((ns, GDN_HEADS, dk, dk), F32),
        ],
        compiler_params=pltpu.CompilerParams(dimension_semantics=("parallel", "arbitrary"),
                                             vmem_limit_bytes=VMEM_LIMIT_BYTES),
        name="mixer",
    )(ug, vn, yb, q, k, v, zg, h, ssm0, *(_arr(w) for w in weights))


def _attn_ffn_kernel(h_ref, mk_ref, mv_ref, gx_ref, wq_ref, wo_ref, g2_ref, wg_ref, wu_ref, wd_ref, gf_ref,
                     out_ref, nb_ref, act_ref, ob_ref, kh_s, vh_s, *, final, sl, tps):
    nseq = h_ref.shape[0] // sl
    heads, dh = mk_ref.shape[2], mk_ref.shape[3]

    @pl.when(pl.program_id(0) % tps == 0)
    def _():
        for s in range(nseq):
            for hh in range(heads):
                kh_s[s, hh] = mk_ref[s, :, hh, :].astype(BF16)
                vh_s[s, hh] = mv_ref[s, :, hh, :].astype(BF16)

    h = h_ref[...]
    nb_ref[...] = _rms(h, gx_ref[...]).astype(BF16)
    q = _dot(nb_ref[...], wq_ref[...])
    scale = dh ** -0.5
    pairs = [(s, hh, slice(s * sl, (s + 1) * sl), slice(hh * dh, (hh + 1) * dh))
             for s in range(nseq) for hh in range(heads)]
    scs = [_dot_nt(q[rows, hs].astype(BF16), kh_s[s, hh]) * scale for s, hh, rows, hs in pairs]
    es = [jnp.exp(sc - jnp.max(sc, axis=-1, keepdims=True)) for sc in scs]
    prs = [e / jnp.sum(e, axis=-1, keepdims=True) for e in es]
    for (s, hh, rows, hs), pr in zip(pairs, prs):
        ob_ref[rows, hs] = _dot(pr.astype(BF16), vh_s[s, hh]).astype(BF16)
    h = h + _dot(ob_ref[...], wo_ref[...])
    nb_ref[...] = _rms(h, g2_ref[...]).astype(BF16)
    h = h + 0.5 * _swiglu(nb_ref, wg_ref, wu_ref, wd_ref, act_ref)
    if final:
        h = _rms(h, gf_ref[...])
    out_ref[...] = h


def _attn_ffn(h2d, mk, mv, l, lw, gf, tm, seq_len, final):
    t, d = h2d.shape
    d_ff = _arr(lw["wg2"]).shape[-1]
    m, heads, dh = mk.shape[2:]
    sl = min(seq_len, tm)
    nseq = tm // sl
    tps = seq_len // sl
    tile = pl.BlockSpec((tm, d), lambda i: (i, 0))
    mem = pl.BlockSpec((None, nseq, m, heads, dh), lambda i: (l, i // tps, 0, 0, 0))
    weights = [lw["gx"], lw["wmq"], lw["wmo"], lw["g2"], lw["wg2"], lw["wu2"], lw["wd2"], gf]
    return pl.pallas_call(
        functools.partial(_attn_ffn_kernel, final=final, sl=sl, tps=tps),
        grid=(t // tm,),
        in_specs=[tile, mem, mem] + [_res(w) for w in weights],
        out_specs=tile,
        out_shape=jax.ShapeDtypeStruct((t, d), F32),
        scratch_shapes=[pltpu.VMEM((tm, d), BF16), pltpu.VMEM((tm, d_ff), BF16), pltpu.VMEM((tm, heads * dh), BF16),
                        pltpu.VMEM((nseq, heads, m, dh), BF16), pltpu.VMEM((nseq, heads, m, dh), BF16)],
        compiler_params=pltpu.CompilerParams(dimension_semantics=("arbitrary",),
                                             vmem_limit_bytes=VMEM_LIMIT_BYTES),
        name="attn_ffn",
    )(h2d, mk, mv, *(_arr(w) for w in weights))


def _mem_kv_kernel(mem_ref, g_ref, wk_ref, wv_ref, k_ref, v_ref):
    mb = _rms(mem_ref[...], g_ref[...]).astype(BF16)
    k = _dot(mb, wk_ref[...])
    v = _dot(mb, wv_ref[...])
    heads, dh = k_ref.shape[1], k_ref.shape[2]
    for hh in range(heads):
        k_ref[:, hh, :] = k[:, hh * dh:(hh + 1) * dh]
        v_ref[:, hh, :] = v[:, hh * dh:(hh + 1) * dh]


def _mem_kv(mem, g, wk, wv, heads):
    b, m, d = mem.shape
    depth, _, mw = wk.shape
    per_layer = lambda *shape: pl.BlockSpec((None,) + shape, lambda l, i: (l,) + (0,) * len(shape))
    out = pl.BlockSpec((None, None, m, heads, mw // heads), lambda l, i: (l, i, 0, 0, 0))
    return pl.pallas_call(
        _mem_kv_kernel,
        grid=(depth, b),
        in_specs=[pl.BlockSpec((None, m, d), lambda l, i: (i, 0, 0)), per_layer(1, d), per_layer(d, mw),
                  per_layer(d, mw)],
        out_specs=[out, out],
        out_shape=[jax.ShapeDtypeStruct((depth, b, m, heads, mw // heads), F32)] * 2,
        compiler_params=pltpu.CompilerParams(dimension_semantics=("parallel", "parallel")),
        name="mem_kv",
    )(mem, g.reshape(depth, 1, d), wk, wv)


def _layer_weights(l, c_sgu, big, ffn1_norm, mix_norm, sgu_norm, sgu_w, sgu_b, pool_w, pool_scale, gdn_conv_w,
                   gdn_a_log, gdn_dt_bias, gdn_out_norm, xattn_norm, ffn2_norm):
    sgu_width = sgu_norm.shape[1]
    pool_width = pool_scale.shape[1]
    gdn_width = gdn_out_norm.shape[1] * GDN_HEADS
    row = lambda vec: vec.reshape(1, -1).astype(F32)
    groups = len(POOL_WINDOWS)
    gdim = pool_width // groups
    poolw = jnp.zeros((pool_width, pool_width), F32)
    for gi in range(groups):
        poolw = poolw.at[gi * gdim:(gi + 1) * gdim, gi * gdim:(gi + 1) * gdim].set(pool_w[l, gi])
    hd = sgu_width // SGU_HEADS
    seg = jnp.arange(sgu_width) // hd
    avg = jnp.where(seg[:, None] == seg[None, :], 1.0 / hd, 0.0).astype(BF16)
    sgw = jnp.concatenate([sgu_w[l, hh, :c_sgu, :c_sgu] for hh in range(SGU_HEADS)], axis=1)
    sgb = jnp.repeat(sgu_b[l, :, :c_sgu].T, hd, axis=1)
    lane_pad = lambda vec: jnp.pad(vec, (GDN_HEADS, LANES - 2 * GDN_HEADS)).reshape(1, LANES)
    lw = {name: (arr, l) for name, arr in big.items()}
    lw.update(
        splits=_in_splits(sgu_width, pool_width, gdn_width),
        g1=row(ffn1_norm[l]), gmix=row(mix_norm[l]),
        sgw=sgw, sgb=sgb, sgn=row(sgu_norm[l]), avg=avg,
        poolw=poolw.astype(BF16), pools=row(pool_scale[l]),
        convw=gdn_conv_w[l], alog=lane_pad(gdn_a_log[l]), dtb=lane_pad(gdn_dt_bias[l]),
        onorm=row(jnp.tile(gdn_out_norm[l], GDN_HEADS)),
        gx=row(xattn_norm[l]), g2=row(ffn2_norm[l]),
    )
    return lw


def _in_splits(sgu_width, pool_width, gdn_width):
    o_pool = 2 * sgu_width
    o_qkv = o_pool + pool_width
    o_z = o_qkv + 3 * gdn_width
    return o_pool, o_qkv, o_z, o_z + gdn_width + LANES


def _layer(x, mk, mv, l, pool0, conv0, ssm0, lw, gf, final, pos0, tm, ts):
    b, seq, d = x.shape
    h, zg, ug, vn, yb, q, k, v, pool_new, conv_new = _ffn_in(x.reshape(b * seq, d), pool0, conv0, lw, tm, seq, pos0)
    shp = lambda arr: arr.reshape(b, seq, arr.shape[-1])
    c_sgu = lw["sgb"].shape[0]
    hshp = lambda arr: arr.reshape(arr.shape[0], b, seq, arr.shape[-1])
    h2, ssm_new = _mixer(shp(h), shp(ug), shp(vn), shp(yb), hshp(q), hshp(k), hshp(v), shp(zg), ssm0, lw, ts, c_sgu,
                         MIXER_SEQS)
    out = _attn_ffn(h2.reshape(b * seq, d), mk, mv, l, lw, gf, tm, seq, final)
    return (shp(out), shp(vn), pool_new[:, POOL_PAD - POOL_HIST:], conv_new[:, CONV_PAD - (GDN_CONV - 1):], ssm_new)


def kernel(x_prompt, x_sample, mem_prompt, cache_mem_k, cache_mem_v, state_pool, state_conv, state_ssm, ffn1_norm, ffn1_w_gate, ffn1_w_up, ffn1_w_down, mix_norm, w_in, sgu_norm, sgu_w, sgu_b, pool_w, pool_scale, gdn_conv_w, gdn_a_log, gdn_dt_bias, gdn_out_norm, w_out, xattn_norm, mem_norm, w_mq, w_mk, w_mv, w_mo, ffn2_norm, ffn2_w_gate, ffn2_w_up, ffn2_w_down, final_norm):
    depth = w_in.shape[0]
    bp, lp, d = x_prompt.shape
    bs, ls, _ = x_sample.shape
    pool_width = state_pool.shape[-1]
    qkv_width = state_conv.shape[-1]
    dk = state_ssm.shape[-1]
    n_pad = _in_splits(sgu_norm.shape[1], pool_width, gdn_out_norm.shape[1] * GDN_HEADS)[3]
    big = dict(wg1=_cast_bf16(ffn1_w_gate), wu1=_cast_bf16(ffn1_w_up), wd1=_cast_bf16(ffn1_w_down),
               win=_cast_bf16(w_in, cols=n_pad), wout=_cast_bf16(w_out), wmq=_cast_bf16(w_mq),
               wmo=_cast_bf16(w_mo), wg2=_cast_bf16(ffn2_w_gate), wu2=_cast_bf16(ffn2_w_up),
               wd2=_cast_bf16(ffn2_w_down))
    small = (ffn1_norm, mix_norm, sgu_norm, sgu_w, sgu_b, pool_w, pool_scale, gdn_conv_w, gdn_a_log, gdn_dt_bias,
             gdn_out_norm, xattn_norm, ffn2_norm)
    gf = final_norm.reshape(1, d)
    prompt_mk, prompt_mv = _mem_kv(mem_prompt, mem_norm, _cast_bf16(w_mk), _cast_bf16(w_mv), cache_mem_k.shape[3])
    pool0_p = jnp.zeros((bp, POOL_PAD, pool_width), F32)
    conv0_p = jnp.zeros((bp, CONV_PAD, qkv_width), F32)
    ssm0_p = jnp.zeros((bp, GDN_HEADS, dk, dk), F32)
    xp, xs = x_prompt, x_sample
    outs_p = [[] for _ in range(3)]
    outs_s = [[] for _ in range(4)]
    for l in range(depth):
        final = l == depth - 1
        lw_p = _layer_weights(l, min(128, lp), big, *small)
        lw_s = lw_p if min(128, ls) == min(128, lp) else _layer_weights(l, min(128, ls), big, *small)
        xp, _, pp, pc, ps = _layer(xp, prompt_mk, prompt_mv, l, pool0_p, conv0_p, ssm0_p, lw_p, gf, final, 0,
                                   tm=512, ts=256)
        pool0_s = jnp.pad(state_pool[l], ((0, 0), (POOL_PAD - POOL_HIST, 0), (0, 0)))
        conv0_s = jnp.pad(state_conv[l], ((0, 0), (CONV_PAD - (GDN_CONV - 1), 0), (0, 0)))
        xs, sv, sp, sc, ss = _layer(xs, cache_mem_k, cache_mem_v, l, pool0_s, conv0_s, state_ssm[l], lw_s, gf, final,
                                    PAST_LEN, tm=bs * ls, ts=ls)
        for lst, val in zip(outs_p, (pp, pc, ps)):
            lst.append(val)
        for lst, val in zip(outs_s, (sp, sc, ss, sv)):
            lst.append(val)
    return (xp, xs, *(jnp.stack(v) for v in outs_p), prompt_mk, prompt_mv, *(jnp.stack(v) for v in outs_s))
```

```python
import functools
import math

import jax
import jax.numpy as jnp
from jax import lax
from jax.experimental import pallas as pl
from jax.experimental.pallas import tpu as pltpu

F32 = jnp.float32
BF16 = jnp.bfloat16

EPS = 1e-6
L2_EPS = 1e-6
LANES = 128
MXU_COLS = 256
VMEM_LIMIT_BYTES = 58 * 1024 * 1024

SGU_HEADS = 4
SGU_BLOCK = 64
POOL_WINDOWS = (2, 4, 8, 16)
POOL_HIST = max(POOL_WINDOWS) - 1
POOL_PAD = 16
GDN_HEADS = 4
GDN_CONV = 4
CONV_PAD = 8
CONV_PHASES = 4
GDN_CHUNK = 64
MEM_HEADS = 4
PAST_LEN = 2048
NEG_BIG = -1e30
MIXER_SEQS = 2
FFN_IN_ROWS = 512
ATTN_FFN_ROWS = 1024
MIXER_ROWS = 256


def _dot(a, b):
    return jnp.dot(a, b, preferred_element_type=F32)


def _dot_nt(a, b):
    return lax.dot_general(a, b, (((1,), (1,)), ((), ())), preferred_element_type=F32)


def _rms(x, g):
    return x * lax.rsqrt(jnp.mean(x * x, axis=-1, keepdims=True) + EPS) * g


def _sigmoid(x):
    return 1.0 / (1.0 + jnp.exp(-x))


def _gelu_tanh(x):
    return 0.5 * x * (1.0 + jnp.tanh(math.sqrt(2.0 / math.pi) * (x + 0.044715 * (x * x * x))))


def _softplus(x):
    return jnp.maximum(x, 0.0) + jnp.log1p(jnp.exp(-jnp.abs(x)))


def _swiglu(nb_ref, wg_ref, wu_ref, wd_ref, act_ref):
    d_ff = wg_ref.shape[1]
    for c in range(d_ff // MXU_COLS):
        sl = slice(c * MXU_COLS, (c + 1) * MXU_COLS)
        g = _dot(nb_ref[...], wg_ref[:, sl])
        u = _dot(nb_ref[...], wu_ref[:, sl])
        act_ref[:, sl] = (g * _sigmoid(g) * u).astype(BF16)
    return _dot(act_ref[...], wd_ref[...])


def _ffn_in_kernel(x_ref, pool0_ref, conv0_ref, g1_ref, wg_ref, wu_ref, wd_ref, gm_ref, win_ref,
                   sgn_ref, avg_ref, poolw_ref, pools_ref, convw_ref,
                   h_ref, zg_ref, ug_ref, vn_ref, yb_ref, q_ref, k_ref, v_ref, poolst_ref, convst_ref,
                   nb_ref, act_ref, uv_s, pbuf, cbuf, *, splits, sl, tps, pos0):
    i = pl.program_id(0)
    tm = x_ref.shape[0]
    nseq = tm // sl
    sgu_w = sgn_ref.shape[1]
    pool_w = pools_ref.shape[1]
    dk = q_ref.shape[2]
    gdn_w = GDN_HEADS * dk
    ncb = 3 * GDN_HEADS
    o_pool, o_qkv, o_z, n_pad = splits
    iv = i

    @pl.when(i % tps == 0)
    def _():
        pbuf[:, 0:POOL_PAD, :] = pool0_ref[...]
        for cb in range(ncb):
            cbuf[:, cb, 0:CONV_PAD, :] = conv0_ref[:, :, cb * dk:(cb + 1) * dk]

    def matmul_stage():
        x = x_ref[...]
        nb_ref[...] = _rms(x, g1_ref[...]).astype(BF16)
        h = x + 0.5 * _swiglu(nb_ref, wg_ref, wu_ref, wd_ref, act_ref)
        h_ref[...] = h
        nb_ref[...] = _rms(h, gm_ref[...]).astype(BF16)
        uv_s[...] = _dot(nb_ref[...], win_ref[:, 0:o_pool])
        zg_ref[...] = _dot(nb_ref[...], win_ref[:, o_z:n_pad])
        p = _dot(nb_ref[...], win_ref[:, o_pool:o_qkv])
        qkv = _dot(nb_ref[...], win_ref[:, o_qkv:o_z])
        for s in range(nseq):
            pbuf[s, POOL_PAD:POOL_PAD + sl, :] = p[s * sl:(s + 1) * sl]
            for cb in range(ncb):
                cbuf[s, cb, CONV_PAD:CONV_PAD + sl, :] = qkv[s * sl:(s + 1) * sl, cb * dk:(cb + 1) * dk]

    def vector_stage():
        uv = _gelu_tanh(uv_s[...])
        ug_ref[...] = uv[:, 0:sgu_w]
        v = uv[:, sgu_w:]
        avg = avg_ref[...]

        def seg_mean(t):
            hi = t.astype(BF16)
            lo = (t - hi.astype(F32)).astype(BF16)
            return _dot(hi, avg) + _dot(lo, avg)

        vc = v - seg_mean(v)
        vn_ref[...] = vc * lax.rsqrt(seg_mean(vc * vc) + EPS) * sgn_ref[...]

        gdim = pool_w // len(POOL_WINDOWS)
        lane = lax.broadcasted_iota(jnp.int32, (sl, LANES), 1)
        pos = pos0 + (iv % tps) * sl + lax.broadcasted_iota(jnp.int32, (sl, LANES), 0)
        dparts = []
        for s in range(nseq):
            dblk = []
            for blk in range(pool_w // LANES):
                ls = slice(blk * LANES, (blk + 1) * LANES)
                wins = [POOL_WINDOWS[(blk * LANES + l0) // gdim] for l0 in range(0, LANES, gdim)]
                wl = jnp.full((sl, LANES), wins[-1], jnp.int32)
                for gi in range(len(wins) - 2, -1, -1):
                    wl = jnp.where(lane < (gi + 1) * gdim, wins[gi], wl)
                pb = pbuf[s, POOL_PAD:POOL_PAD + sl, ls]
                acc = pb
                prev = 1
                for w in sorted(set(wins)):
                    part = None
                    for jj in range(prev, w):
                        sh = pbuf[s, POOL_PAD - jj:POOL_PAD - jj + sl, ls]
                        part = sh if part is None else part + sh
                    if part is not None:
                        acc = acc + (part if w == min(wins) else jnp.where(wl >= w, part, 0.0))
                    prev = w
                cnt = jnp.minimum(wl, pos + 1).astype(F32)
                dblk.append(acc / cnt - pb)
            dparts.append(jnp.concatenate(dblk, axis=1))
            tail = pbuf[s, sl:sl + POOL_PAD, :]
            poolst_ref[s] = tail
            pbuf[s, 0:POOL_PAD, :] = tail
        dlt = (dparts[0] if nseq == 1 else jnp.concatenate(dparts, axis=0)).astype(BF16)
        yb_ref[...] = (_dot(dlt, poolw_ref[...]) * pools_ref[...]).astype(BF16)

        qscale = dk ** -0.5
        nrow = sl // CONV_PHASES
        for cb in range(ncb):
            cs = slice(cb * dk, (cb + 1) * dk)
            which, hh = divmod(cb, GDN_HEADS)
            for s in range(nseq):
                for r in range(CONV_PHASES):
                    y = None
                    for t in range(GDN_CONV):
                        tap = (cbuf[s, cb, pl.ds(CONV_PAD + r - t, nrow, stride=CONV_PHASES), :]
                               * convw_ref[GDN_CONV - 1 - t:GDN_CONV - t, cs])
                        y = tap if y is None else y + tap
                    y = y * _sigmoid(y)
                    out_rows = pl.ds(s * sl + r, nrow, stride=CONV_PHASES)
                    if which == 0:
                        q_ref[hh, out_rows, :] = y * (lax.rsqrt(jnp.sum(y * y, axis=-1, keepdims=True) + L2_EPS)
                                                      * qscale)
                    elif which == 1:
                        k_ref[hh, out_rows, :] = y * lax.rsqrt(jnp.sum(y * y, axis=-1, keepdims=True) + L2_EPS)
                    else:
                        v_ref[hh, out_rows, :] = y
        for s in range(nseq):
            for cb in range(ncb):
                ctail = cbuf[s, cb, sl:sl + CONV_PAD, :]
                convst_ref[s, :, cb * dk:(cb + 1) * dk] = ctail
                cbuf[s, cb, 0:CONV_PAD, :] = ctail

    matmul_stage()
    vector_stage()


def _res(w):
    if isinstance(w, tuple):
        arr, l = w
        nd = arr.ndim - 1
        return pl.BlockSpec((None,) + arr.shape[1:], lambda *_: (l,) + (0,) * nd, pipeline_mode=pl.Buffered(1))
    nd = w.ndim
    return pl.BlockSpec(w.shape, lambda *_: (0,) * nd, pipeline_mode=pl.Buffered(1))


def _arr(w):
    return w[0] if isinstance(w, tuple) else w


def _cast_kernel(x_ref, o_ref):
    w = x_ref.shape[-1]
    o_ref[:, 0:w] = x_ref[...].astype(BF16)
    if o_ref.shape[-1] > w:
        o_ref[:, w:] = jnp.zeros((o_ref.shape[0], o_ref.shape[-1] - w), BF16)


def _cast_bf16(w, cols=None, block_rows=256):
    depth, rows, c = w.shape
    cols = c if cols is None else cols
    br = min(block_rows, rows)
    return pl.pallas_call(
        _cast_kernel,
        grid=(depth, rows // br),
        in_specs=[pl.BlockSpec((None, br, c), lambda l, i: (l, i, 0))],
        out_specs=pl.BlockSpec((None, br, cols), lambda l, i: (l, i, 0)),
        out_shape=jax.ShapeDtypeStruct((depth, rows, cols), BF16),
        compiler_params=pltpu.CompilerParams(dimension_semantics=("parallel", "parallel")),
        name="cast_bf16",
    )(w)


def _ffn_in(x2d, pool0, conv0, lw, tm, seq_len, pos0):
    t, d = x2d.shape
    d_ff = _arr(lw["wg1"]).shape[-1]
    o_pool, o_qkv, o_z, n_pad = lw["splits"]
    sgu_w, pool_w, gdn_w = lw["sgn"].shape[1], lw["pools"].shape[1], lw["onorm"].shape[1]
    sl = min(seq_len, tm)
    nseq = tm // sl
    tps = seq_len // sl
    row = lambda width: pl.BlockSpec((tm, width), lambda i: (i, 0))
    hist = lambda pad, width: pl.BlockSpec((nseq, pad, width), lambda i: (i // tps, 0, 0))
    res = _res
    weights = [lw["g1"], lw["wg1"], lw["wu1"], lw["wd1"], lw["gmix"], lw["win"],
               lw["sgn"], lw["avg"], lw["poolw"], lw["pools"], lw["convw"]]
    outs = [(d, F32), (n_pad - o_z, F32), (sgu_w, F32), (sgu_w, F32), (pool_w, BF16)]
    dk = gdn_w // GDN_HEADS
    head_rows = pl.BlockSpec((GDN_HEADS, tm, dk), lambda i: (0, i, 0))
    nbatch = t // seq_len
    return pl.pallas_call(
        functools.partial(_ffn_in_kernel, splits=lw["splits"], sl=sl, tps=tps, pos0=pos0),
        grid=(t // tm,),
        in_specs=[row(d), hist(POOL_PAD, pool_w), hist(CONV_PAD, 3 * gdn_w)] + [res(w) for w in weights],
        out_specs=[row(w) for w, _ in outs] + [head_rows] * 3 + [hist(POOL_PAD, pool_w), hist(CONV_PAD, 3 * gdn_w)],
        out_shape=[jax.ShapeDtypeStruct((t, w), dt) for w, dt in outs]
                  + [jax.ShapeDtypeStruct((GDN_HEADS, t, dk), F32)] * 3
                  + [jax.ShapeDtypeStruct((nbatch, POOL_PAD, pool_w), F32),
                     jax.ShapeDtypeStruct((nbatch, CONV_PAD, 3 * gdn_w), F32)],
        scratch_shapes=[pltpu.VMEM((tm, d), BF16), pltpu.VMEM((tm, d_ff), BF16),
                        pltpu.VMEM((tm, o_pool), F32),
                        pltpu.VMEM((nseq, POOL_PAD + sl, pool_w), F32),
                        pltpu.VMEM((nseq, 3 * GDN_HEADS, CONV_PAD + sl, dk), F32)],
        compiler_params=pltpu.CompilerParams(dimension_semantics=("arbitrary",),
                                             vmem_limit_bytes=VMEM_LIMIT_BYTES),
        name="ffn_in",
    )(x2d, pool0, conv0, *(_arr(w) for w in weights))


def _mixer_kernel(ug_ref, vn_ref, yb_ref, q_ref, k_ref, v_ref, zg_ref, h_ref, ssm0_ref,
                  sgw_ref, sgb_ref, alog_ref, dtb_ref, onorm_ref, wout_ref,
                  h2_ref, ssm_ref,
                  u_s, wq_s, ql_s, y_s, s_s, *, ts, c_sgu):
    j = pl.program_id(1)
    nj = pl.num_programs(1)
    ns = vn_ref.shape[0]
    sgu_w = vn_ref.shape[2]
    pool_w = yb_ref.shape[2]
    gdn_w = onorm_ref.shape[1]
    dk = gdn_w // GDN_HEADS
    cc = GDN_CHUNK
    nch = ts // cc
    pre_w = sgu_w + pool_w
    seqs = range(ns)

    @pl.when(j == 0)
    def _():
        s_s[...] = ssm0_ref[...]

    hd = sgu_w // SGU_HEADS
    wi = lax.broadcasted_iota(jnp.int32, (c_sgu, SGU_HEADS * c_sgu), 0)
    wj = lax.broadcasted_iota(jnp.int32, (c_sgu, SGU_HEADS * c_sgu), 1) % c_sgu
    wmask = jnp.where(wi // SGU_BLOCK >= wj // SGU_BLOCK, sgw_ref[...], 0.0).astype(BF16)
    lane_head = lax.broadcasted_iota(jnp.int32, (c_sgu, sgu_w), 1) // hd
    for e in seqs:
        for c in range(ts // c_sgu):
            rows = slice(c * c_sgu, (c + 1) * c_sgu)
            vch = vn_ref[e, rows, :]
            vstack = jnp.concatenate([jnp.where(lane_head == hh, vch, 0.0) for hh in range(SGU_HEADS)],
                                     axis=0).astype(BF16)
            s = _dot(wmask, vstack) + sgb_ref[...]
            y_s[e * ts + c * c_sgu:e * ts + (c + 1) * c_sgu, 0:sgu_w] = (ug_ref[e, rows, :] * s).astype(BF16)
        y_s[e * ts:(e + 1) * ts, sgu_w:pre_w] = yb_ref[e]

    ti = lax.broadcasted_iota(jnp.int32, (2 * ts, ts), 0)
    tj = lax.broadcasted_iota(jnp.int32, (2 * ts, ts), 1)
    tr = jnp.where(ti < ts, ti, ti - ts)
    same_chunk = tr // cc == tj // cc
    summat = jnp.where(same_chunk, jnp.where(ti < ts, jnp.where(tj <= tr, 1.0, 0.0), 1.0), 0.0).astype(BF16)
    beta, gc, egc, erg, etot = [], [], [], [], []
    for e in seqs:
        gates = zg_ref[e, :, gdn_w:gdn_w + LANES]
        beta.append(_sigmoid(gates))
        g = -jnp.exp(alog_ref[...]) * _softplus(gates + dtb_ref[...])
        g_hi = g.astype(BF16)
        g_r = g - g_hi.astype(F32)
        g_mid = g_r.astype(BF16)
        g_lo = (g_r - g_mid.astype(F32)).astype(BF16)
        sums = _dot(summat, g_hi) + _dot(summat, g_mid) + _dot(summat, g_lo)
        gc.append(sums[0:ts])
        tot = sums[ts:2 * ts]
        egc.append(jnp.exp(gc[e]))
        erg.append(jnp.exp(tot - gc[e]))
        etot.append(jnp.exp(tot))

    ri = lax.broadcasted_iota(jnp.int32, (cc, cc), 0)
    ci = lax.broadcasted_iota(jnp.int32, (cc, cc), 1)
    incl = ri >= ci
    strict = ri > ci
    chains = [(e, n, hh) for n in range(nch) for e in seqs for hh in range(GDN_HEADS)]
    gct = {(e, n): gc[e][n * cc:(n + 1) * cc].T for e in seqs for n in range(nch)}

    nmats = []
    for e, n, hh in chains:
        rows = slice(n * cc, (n + 1) * cc)
        hs = slice(hh * dk, (hh + 1) * dk)
        gl = GDN_HEADS + hh
        k = k_ref[hh, e, rows, :]
        kbq = jnp.concatenate([k * beta[e][rows, hh:hh + 1], q_ref[hh, e, rows, :]], axis=0).astype(BF16)
        prod = _dot_nt(kbq, k.astype(BF16))
        decay = jnp.exp(jnp.where(incl, gc[e][rows, gl:gl + 1] - gct[e, n][gl:gl + 1, :], NEG_BIG))
        nmats.append(jnp.where(strict, -(prod[0:cc] * decay), 0.0))
        ql_s[e, n, hh, 0:cc, :] = (prod[cc:2 * cc] * decay).astype(BF16)
    tps = list(nmats)
    pows = list(nmats)
    for _ in range(5):
        pows = [_dot(m.astype(BF16), m.astype(BF16)) for m in pows]
        tps = [t + m + _dot(t.astype(BF16), m.astype(BF16)) for t, m in zip(tps, pows)]
    for (e, n, hh), tp in zip(chains, tps):
        rows = slice(n * cc, (n + 1) * cc)
        hs = slice(hh * dk, (hh + 1) * dk)
        gl = GDN_HEADS + hh
        k = k_ref[hh, e, rows, :]
        b = beta[e][rows, hh:hh + 1]
        eg = egc[e][rows, gl:gl + 1]
        rhs = jnp.concatenate([v_ref[hh, e, rows, :] * b, k * (b * eg)], axis=1)
        sol = rhs + _dot(tp.astype(BF16), rhs.astype(BF16))
        u_s[e, rows, hs] = sol[:, 0:dk]
        wq_s[e, n, hh, 0:cc, :] = sol[:, dk:2 * dk].astype(BF16)
        wq_s[e, n, hh, cc:2 * cc, :] = (q_ref[hh, e, rows, :] * eg).astype(BF16)
        ql_s[e, n, hh, cc:cc + dk, :] = (k * erg[e][rows, gl:gl + 1]).T.astype(BF16)

    fill = []
    for kb in range(pre_w // MXU_COLS):
        for nb in range(h_ref.shape[2] // MXU_COLS):
            fill.append((slice(kb * MXU_COLS, (kb + 1) * MXU_COLS), slice(nb * MXU_COLS, (nb + 1) * MXU_COLS), kb == 0))
    gaps = 2 * nch

    def run_fill(gap):
        for ks, cs, first in fill[gap * len(fill) // gaps:(gap + 1) * len(fill) // gaps]:
            part = _dot(y_s[:, ks], wout_ref[ks, cs])
            for e in seqs:
                base = h_ref[e, :, cs] if first else h2_ref[e, :, cs]
                h2_ref[e, :, cs] = base + part[e * ts:(e + 1) * ts]

    heads = [(e, hh) for e in seqs for hh in range(GDN_HEADS)]
    for n in range(nch):
        rows = slice(n * cc, (n + 1) * cc)
        sts = [s_s[e, hh] for e, hh in heads]
        r1 = [_dot(wq_s[e, n, hh], st.astype(BF16)) for (e, hh), st in zip(heads, sts)]
        run_fill(2 * n)
        vnew = [u_s[e, rows, hh * dk:(hh + 1) * dk] - r[0:cc] for (e, hh), r in zip(heads, r1)]
        r2 = [_dot(ql_s[e, n, hh], vn.astype(BF16)) for (e, hh), vn in zip(heads, vnew)]
        run_fill(2 * n + 1)
        for (e, hh), st, ra, rb in zip(heads, sts, r1, r2):
            gl = GDN_HEADS + hh
            hs = slice(hh * dk, (hh + 1) * dk)
            glcol = etot[e][rows, gl:gl + 1]
            s_s[e, hh] = st * jnp.concatenate([glcol] * (dk // cc), axis=0) + rb[cc:cc + dk]
            o = ra[cc:2 * cc] + rb[0:cc]
            z = zg_ref[e, rows, hs]
            on = o * lax.rsqrt(jnp.mean(o * o, axis=-1, keepdims=True) + EPS) * onorm_ref[:, hs]
            y_s[e * ts + n * cc:e * ts + (n + 1) * cc, pre_w + hh * dk:pre_w + (hh + 1) * dk] = (
                on * (z * _sigmoid(z))).astype(BF16)

    @pl.when(j == nj - 1)
    def _():
        ssm_ref[...] = s_s[...]

    part = _dot(y_s[:, pre_w:], wout_ref[pre_w:, :])
    for e in seqs:
        h2_ref[e] = h2_ref[e] + part[e * ts:(e + 1) * ts]


def _mixer(h, ug, vn, yb, q, k, v, zg, ssm0, lw, ts, c_sgu, ns):
    b, l, d = h.shape
    sgu_w, pool_w, dk = vn.shape[2], yb.shape[2], q.shape[3]
    gdn_w = GDN_HEADS * dk
    nch = ts // GDN_CHUNK
    mix_w = sgu_w + pool_w + gdn_w
    tile = lambda width: pl.BlockSpec((ns, ts, width), lambda i, j: (i, j, 0))
    per_b = lambda *shape: pl.BlockSpec((ns,) + shape, lambda i, j: (i,) + (0,) * len(shape))
    heads_tile = pl.BlockSpec((GDN_HEADS, ns, ts, dk), lambda i, j: (0, i, j, 0))
    res = _res
    weights = [lw["sgw"], lw["sgb"], lw["alog"], lw["dtb"], lw["onorm"], lw["wout"]]
    return pl.pallas_call(
        functools.partial(_mixer_kernel, ts=ts, c_sgu=c_sgu),
        grid=(b // ns, l // ts),
        in_specs=[tile(sgu_w), tile(sgu_w), tile(pool_w), heads_tile, heads_tile, heads_tile,
                  tile(zg.shape[2]), tile(d), per_b(GDN_HEADS, dk, dk)] + [res(w) for w in weights],
        out_specs=[tile(d), per_b(GDN_HEADS, dk, dk)],
        out_shape=[jax.ShapeDtypeStruct((b, l, d), F32), jax.ShapeDtypeStruct((b, GDN_HEADS, dk, dk), F32)],
        scratch_shapes=[
            pltpu.VMEM((ns, ts, gdn_w), F32),
            pltpu.VMEM((ns, nch, GDN_HEADS, 2 * GDN_CHUNK, dk), BF16),
            pltpu.VMEM((ns, nch, GDN_HEADS, GDN_CHUNK + dk, GDN_CHUNK), BF16),
            pltpu.VMEM((ns * ts, mix_w), BF16),
            pltpu.VMEM((ns, GDN_HEADS, dk, dk), F32),
        ],
        compiler_params=pltpu.CompilerParams(dimension_semantics=("parallel", "arbitrary"),
                                             vmem_limit_bytes=VMEM_LIMIT_BYTES),
        name="mixer",
    )(ug, vn, yb, q, k, v, zg, h, ssm0, *(_arr(w) for w in weights))


def _attn_ffn_kernel(h_ref, mk_ref, mv_ref, gx_ref, wq_ref, wo_ref, g2_ref, wg_ref, wu_ref, wd_ref, gf_ref,
                     out_ref, nb_ref, act_ref, ob_ref, kh_s, vh_s, *, final, sl, tps):
    nseq = h_ref.shape[0] // sl
    heads, dh = mk_ref.shape[2], mk_ref.shape[3]

    @pl.when(pl.program_id(0) % tps == 0)
    def _():
        for s in range(nseq):
            for hh in range(heads):
                kh_s[s, hh] = mk_ref[s, :, hh, :].astype(BF16)
                vh_s[s, hh] = mv_ref[s, :, hh, :].astype(BF16)

    h = h_ref[...]
    nb_ref[...] = _rms(h, gx_ref[...]).astype(BF16)
    q = _dot(nb_ref[...], wq_ref[...])
    scale = dh ** -0.5
    pairs = [(s, hh, slice(s * sl, (s + 1) * sl), slice(hh * dh, (hh + 1) * dh))
             for s in range(nseq) for hh in range(heads)]
    scs = [_dot_nt(q[rows, hs].astype(BF16), kh_s[s, hh]) * scale for s, hh, rows, hs in pairs]
    es = [jnp.exp(sc - jnp.max(sc, axis=-1, keepdims=True)) for sc in scs]
    prs = [e / jnp.sum(e, axis=-1, keepdims=True) for e in es]
    for (s, hh, rows, hs), pr in zip(pairs, prs):
        ob_ref[rows, hs] = _dot(pr.astype(BF16), vh_s[s, hh]).astype(BF16)
    h = h + _dot(ob_ref[...], wo_ref[...])
    nb_ref[...] = _rms(h, g2_ref[...]).astype(BF16)
    h = h + 0.5 * _swiglu(nb_ref, wg_ref, wu_ref, wd_ref, act_ref)
    if final:
        h = _rms(h, gf_ref[...])
    out_ref[...] = h


def _attn_ffn(h2d, mk, mv, l, lw, gf, tm, seq_len, final):
    t, d = h2d.shape
    d_ff = _arr(lw["wg2"]).shape[-1]
    m, heads, dh = mk.shape[2:]
    sl = min(seq_len, tm)
    nseq = tm // sl
    tps = seq_len // sl
    tile = pl.BlockSpec((tm, d), lambda i: (i, 0))
    mem = pl.BlockSpec((None, nseq, m, heads, dh), lambda i: (l, i // tps, 0, 0, 0))
    weights = [lw["gx"], lw["wmq"], lw["wmo"], lw["g2"], lw["wg2"], lw["wu2"], lw["wd2"], gf]
    return pl.pallas_call(
        functools.partial(_attn_ffn_kernel, final=final, sl=sl, tps=tps),
        grid=(t // tm,),
        in_specs=[tile, mem, mem] + [_res(w) for w in weights],
        out_specs=tile,
        out_shape=jax.ShapeDtypeStruct((t, d), F32),
        scratch_shapes=[pltpu.VMEM((tm, d), BF16), pltpu.VMEM((tm, d_ff), BF16), pltpu.VMEM((tm, heads * dh), BF16),
                        pltpu.VMEM((nseq, heads, m, dh), BF16), pltpu.VMEM((nseq, heads, m, dh), BF16)],
        compiler_params=pltpu.CompilerParams(dimension_semantics=("arbitrary",),
                                             vmem_limit_bytes=VMEM_LIMIT_BYTES),
        name="attn_ffn",
    )(h2d, mk, mv, *(_arr(w) for w in weights))


def _mem_kv_kernel(mem_ref, g_ref, wk_ref, wv_ref, k_ref, v_ref):
    mb = _rms(mem_ref[...], g_ref[...]).astype(BF16)
    k = _dot(mb, wk_ref[...])
    v = _dot(mb, wv_ref[...])
    heads, dh = k_ref.shape[1], k_ref.shape[2]
    for hh in range(heads):
        k_ref[:, hh, :] = k[:, hh * dh:(hh + 1) * dh]
        v_ref[:, hh, :] = v[:, hh * dh:(hh + 1) * dh]


def _mem_kv(mem, g, wk, wv, heads):
    b, m, d = mem.shape
    depth, _, mw = wk.shape
    per_layer = lambda *shape: pl.BlockSpec((None,) + shape, lambda l, i: (l,) + (0,) * len(shape))
    out = pl.BlockSpec((None, None, m, heads, mw // heads), lambda l, i: (l, i, 0, 0, 0))
    return pl.pallas_call(
        _mem_kv_kernel,
        grid=(depth, b),
        in_specs=[pl.BlockSpec((None, m, d), lambda l, i: (i, 0, 0)), per_layer(1, d), per_layer(d, mw),
                  per_layer(d, mw)],
        out_specs=[out, out],
        out_shape=[jax.ShapeDtypeStruct((depth, b, m, heads, mw // heads), F32)] * 2,
        compiler_params=pltpu.CompilerParams(dimension_semantics=("parallel", "parallel")),
        name="mem_kv",
    )(mem, g.reshape(depth, 1, d), wk, wv)


def _layer_weights(l, c_sgu, big, ffn1_norm, mix_norm, sgu_norm, sgu_w, sgu_b, pool_w, pool_scale, gdn_conv_w,
                   gdn_a_log, gdn_dt_bias, gdn_out_norm, xattn_norm, ffn2_norm):
    sgu_width = sgu_norm.shape[1]
    pool_width = pool_scale.shape[1]
    gdn_width = gdn_out_norm.shape[1] * GDN_HEADS
    row = lambda vec: vec.reshape(1, -1).astype(F32)
    groups = len(POOL_WINDOWS)
    gdim = pool_width // groups
    poolw = jnp.zeros((pool_width, pool_width), F32)
    for gi in range(groups):
        poolw = poolw.at[gi * gdim:(gi + 1) * gdim, gi * gdim:(gi + 1) * gdim].set(pool_w[l, gi])
    hd = sgu_width // SGU_HEADS
    seg = jnp.arange(sgu_width) // hd
    avg = jnp.where(seg[:, None] == seg[None, :], 1.0 / hd, 0.0).astype(BF16)
    sgw = jnp.concatenate([sgu_w[l, hh, :c_sgu, :c_sgu] for hh in range(SGU_HEADS)], axis=1)
    sgb = jnp.repeat(sgu_b[l, :, :c_sgu].T, hd, axis=1)
    lane_pad = lambda vec: jnp.pad(vec, (GDN_HEADS, LANES - 2 * GDN_HEADS)).reshape(1, LANES)
    lw = {name: (arr, l) for name, arr in big.items()}
    lw.update(
        splits=_in_splits(sgu_width, pool_width, gdn_width),
        g1=row(ffn1_norm[l]), gmix=row(mix_norm[l]),
        sgw=sgw, sgb=sgb, sgn=row(sgu_norm[l]), avg=avg,
        poolw=poolw.astype(BF16), pools=row(pool_scale[l]),
        convw=gdn_conv_w[l], alog=lane_pad(gdn_a_log[l]), dtb=lane_pad(gdn_dt_bias[l]),
        onorm=row(jnp.tile(gdn_out_norm[l], GDN_HEADS)),
        gx=row(xattn_norm[l]), g2=row(ffn2_norm[l]),
    )
    return lw


def _in_splits(sgu_width, pool_width, gdn_width):
    o_pool = 2 * sgu_width
    o_qkv = o_pool + pool_width
    o_z = o_qkv + 3 * gdn_width
    return o_pool, o_qkv, o_z, o_z + gdn_width + LANES


def _tiles(batch, seq):
    total = batch * seq
    return min(FFN_IN_ROWS, total), min(MIXER_ROWS, seq), min(ATTN_FFN_ROWS, total)


def _layer(x, mk, mv, l, pool0, conv0, ssm0, lw, gf, final, pos0):
    b, seq, d = x.shape
    tm, ts, tm_attn = _tiles(b, seq)
    h, zg, ug, vn, yb, q, k, v, pool_new, conv_new = _ffn_in(x.reshape(b * seq, d), pool0, conv0, lw, tm, seq, pos0)
    shp = lambda arr: arr.reshape(b, seq, arr.shape[-1])
    c_sgu = lw["sgb"].shape[0]
    hshp = lambda arr: arr.reshape(arr.shape[0], b, seq, arr.shape[-1])
    h2, ssm_new = _mixer(shp(h), shp(ug), shp(vn), shp(yb), hshp(q), hshp(k), hshp(v), shp(zg), ssm0, lw, ts, c_sgu,
                         MIXER_SEQS)
    out = _attn_ffn(h2.reshape(b * seq, d), mk, mv, l, lw, gf, tm_attn, seq, final)
    return (shp(out), shp(vn), pool_new[:, POOL_PAD - POOL_HIST:], conv_new[:, CONV_PAD - (GDN_CONV - 1):], ssm_new)


def kernel(x_prompt, x_sample, mem_prompt, cache_mem_k, cache_mem_v, state_pool, state_conv, state_ssm, ffn1_norm, ffn1_w_gate, ffn1_w_up, ffn1_w_down, mix_norm, w_in, sgu_norm, sgu_w, sgu_b, pool_w, pool_scale, gdn_conv_w, gdn_a_log, gdn_dt_bias, gdn_out_norm, w_out, xattn_norm, mem_norm, w_mq, w_mk, w_mv, w_mo, ffn2_norm, ffn2_w_gate, ffn2_w_up, ffn2_w_down, final_norm):
    depth = w_in.shape[0]
    bp, lp, d = x_prompt.shape
    bs, ls, _ = x_sample.shape
    pool_width = state_pool.shape[-1]
    qkv_width = state_conv.shape[-1]
    dk = state_ssm.shape[-1]
    n_pad = _in_splits(sgu_norm.shape[1], pool_width, gdn_out_norm.shape[1] * GDN_HEADS)[3]
    big = dict(wg1=_cast_bf16(ffn1_w_gate), wu1=_cast_bf16(ffn1_w_up), wd1=_cast_bf16(ffn1_w_down),
               win=_cast_bf16(w_in, cols=n_pad), wout=_cast_bf16(w_out), wmq=_cast_bf16(w_mq),
               wmo=_cast_bf16(w_mo), wg2=_cast_bf16(ffn2_w_gate), wu2=_cast_bf16(ffn2_w_up),
               wd2=_cast_bf16(ffn2_w_down))
    small = (ffn1_norm, mix_norm, sgu_norm, sgu_w, sgu_b, pool_w, pool_scale, gdn_conv_w, gdn_a_log, gdn_dt_bias,
             gdn_out_norm, xattn_norm, ffn2_norm)
    gf = final_norm.reshape(1, d)
    prompt_mk, prompt_mv = _mem_kv(mem_prompt, mem_norm, _cast_bf16(w_mk), _cast_bf16(w_mv), cache_mem_k.shape[3])
    pool0_p = jnp.zeros((bp, POOL_PAD, pool_width), F32)
    conv0_p = jnp.zeros((bp, CONV_PAD, qkv_width), F32)
    ssm0_p = jnp.zeros((bp, GDN_HEADS, dk, dk), F32)
    xp, xs = x_prompt, x_sample
    outs_p = [[] for _ in range(3)]
    outs_s = [[] for _ in range(4)]
    for l in range(depth):
        final = l == depth - 1
        lw_p = _layer_weights(l, min(128, lp), big, *small)
        lw_s = lw_p if min(128, ls) == min(128, lp) else _layer_weights(l, min(128, ls), big, *small)
        xp, _, pp, pc, ps = _layer(xp, prompt_mk, prompt_mv, l, pool0_p, conv0_p, ssm0_p, lw_p, gf, final, 0)
        pool0_s = jnp.pad(state_pool[l], ((0, 0), (POOL_PAD - POOL_HIST, 0), (0, 0)))
        conv0_s = jnp.pad(state_conv[l], ((0, 0), (CONV_PAD - (GDN_CONV - 1), 0), (0, 0)))
        xs, sv, sp, sc, ss = _layer(xs, cache_mem_k, cache_mem_v, l, pool0_s, conv0_s, state_ssm[l], lw_s, gf, final,
                                    PAST_LEN)
        for lst, val in zip(outs_p, (pp, pc, ps)):
            lst.append(val)
        for lst, val in zip(outs_s, (sp, sc, ss, sv)):
            lst.append(val)
    return (xp, xs, *(jnp.stack(v) for v in outs_p), prompt_mk, prompt_mv, *(jnp.stack(v) for v in outs_s))
```

```python
import functools
import math

import jax
import jax.numpy as jnp
from jax import lax
from jax.experimental import pallas as pl
from jax.experimental.pallas import tpu as pltpu

F32 = jnp.float32
BF16 = jnp.bfloat16

EPS = 1e-6
L2_EPS = 1e-6
LANES = 128
MXU_COLS = 256
VMEM_LIMIT_BYTES = 58 * 1024 * 1024

SGU_HEADS = 4
SGU_BLOCK = 64
POOL_WINDOWS = (2, 4, 8, 16)
POOL_HIST = max(POOL_WINDOWS) - 1
POOL_PAD = 16
GDN_HEADS = 4
GDN_CONV = 4
CONV_PAD = 8
CONV_PHASES = 4
GDN_CHUNK = 64
MEM_HEADS = 4
PAST_LEN = 2048
NEG_BIG = -1e30
MIXER_SEQS = 2
FFN_IN_ROWS = 512
ATTN_FFN_ROWS = 1024
MIXER_ROWS = 256


def _dot(a, b):
    return jnp.dot(a, b, preferred_element_type=F32)


def _dot_nt(a, b):
    return lax.dot_general(a, b, (((1,), (1,)), ((), ())), preferred_element_type=F32)


def _rms(x, g):
    return x * lax.rsqrt(jnp.mean(x * x, axis=-1, keepdims=True) + EPS) * g


def _sigmoid(x):
    return 1.0 / (1.0 + jnp.exp(-x))


def _gelu_tanh(x):
    return 0.5 * x * (1.0 + jnp.tanh(math.sqrt(2.0 / math.pi) * (x + 0.044715 * (x * x * x))))


def _softplus(x):
    return jnp.maximum(x, 0.0) + jnp.log1p(jnp.exp(-jnp.abs(x)))


def _swiglu(nb_ref, wg_ref, wu_ref, wd_ref, act_ref):
    d_ff = wg_ref.shape[1]
    for c in range(d_ff // MXU_COLS):
        sl = slice(c * MXU_COLS, (c + 1) * MXU_COLS)
        g = _dot(nb_ref[...], wg_ref[:, sl])
        u = _dot(nb_ref[...], wu_ref[:, sl])
        act_ref[:, sl] = (g * _sigmoid(g) * u).astype(BF16)
    return _dot(act_ref[...], wd_ref[...])


def _ffn_in_kernel(x_ref, pool0_ref, conv0_ref, g1_ref, wg_ref, wu_ref, wd_ref, gm_ref, win_ref,
                   sgn_ref, avg_ref, poolw_ref, pools_ref, convw_ref,
                   h_ref, zg_ref, ug_ref, vn_ref, yb_ref, q_ref, k_ref, v_ref, poolst_ref, convst_ref,
                   nb_ref, act_ref, uv_s, pbuf, cbuf, *, splits, sl, tps, pos0):
    i = pl.program_id(0)
    tm = x_ref.shape[0]
    nseq = tm // sl
    sgu_w = sgn_ref.shape[1]
    pool_w = pools_ref.shape[1]
    dk = q_ref.shape[2]
    gdn_w = GDN_HEADS * dk
    ncb = 3 * GDN_HEADS
    o_pool, o_qkv, o_z, n_pad = splits
    iv = i

    @pl.when(i % tps == 0)
    def _():
        pbuf[:, 0:POOL_PAD, :] = pool0_ref[...]
        for cb in range(ncb):
            cbuf[:, cb, 0:CONV_PAD, :] = conv0_ref[:, :, cb * dk:(cb + 1) * dk]

    def matmul_stage():
        x = x_ref[...]
        nb_ref[...] = _rms(x, g1_ref[...]).astype(BF16)
        h = x + 0.5 * _swiglu(nb_ref, wg_ref, wu_ref, wd_ref, act_ref)
        h_ref[...] = h
        nb_ref[...] = _rms(h, gm_ref[...]).astype(BF16)
        uv_s[...] = _dot(nb_ref[...], win_ref[:, 0:o_pool])
        zg_ref[...] = _dot(nb_ref[...], win_ref[:, o_z:n_pad])
        p = _dot(nb_ref[...], win_ref[:, o_pool:o_qkv])
        qkv = _dot(nb_ref[...], win_ref[:, o_qkv:o_z])
        for s in range(nseq):
            pbuf[s, POOL_PAD:POOL_PAD + sl, :] = p[s * sl:(s + 1) * sl]
            for cb in range(ncb):
                cbuf[s, cb, CONV_PAD:CONV_PAD + sl, :] = qkv[s * sl:(s + 1) * sl, cb * dk:(cb + 1) * dk]

    def vector_stage():
        uv = _gelu_tanh(uv_s[...])
        ug_ref[...] = uv[:, 0:sgu_w]
        v = uv[:, sgu_w:]
        avg = avg_ref[...]

        def seg_mean(t):
            hi = t.astype(BF16)
            lo = (t - hi.astype(F32)).astype(BF16)
            return _dot(hi, avg) + _dot(lo, avg)

        vc = v - seg_mean(v)
        vn_ref[...] = vc * lax.rsqrt(seg_mean(vc * vc) + EPS) * sgn_ref[...]

        gdim = pool_w // len(POOL_WINDOWS)
        lane = lax.broadcasted_iota(jnp.int32, (sl, LANES), 1)
        pos = pos0 + (iv % tps) * sl + lax.broadcasted_iota(jnp.int32, (sl, LANES), 0)
        dparts = []
        for s in range(nseq):
            dblk = []
            for blk in range(pool_w // LANES):
                ls = slice(blk * LANES, (blk + 1) * LANES)
                wins = [POOL_WINDOWS[(blk * LANES + l0) // gdim] for l0 in range(0, LANES, gdim)]
                wl = jnp.full((sl, LANES), wins[-1], jnp.int32)
                for gi in range(len(wins) - 2, -1, -1):
                    wl = jnp.where(lane < (gi + 1) * gdim, wins[gi], wl)
                pb = pbuf[s, POOL_PAD:POOL_PAD + sl, ls]
                acc = pb
                prev = 1
                for w in sorted(set(wins)):
                    part = None
                    for jj in range(prev, w):
                        sh = pbuf[s, POOL_PAD - jj:POOL_PAD - jj + sl, ls]
                        part = sh if part is None else part + sh
                    if part is not None:
                        acc = acc + (part if w == min(wins) else jnp.where(wl >= w, part, 0.0))
                    prev = w
                cnt = jnp.minimum(wl, pos + 1).astype(F32)
                dblk.append(acc / cnt - pb)
            dparts.append(jnp.concatenate(dblk, axis=1))
            tail = pbuf[s, sl:sl + POOL_PAD, :]
            poolst_ref[s] = tail
            pbuf[s, 0:POOL_PAD, :] = tail
        dlt = (dparts[0] if nseq == 1 else jnp.concatenate(dparts, axis=0)).astype(BF16)
        yb_ref[...] = (_dot(dlt, poolw_ref[...]) * pools_ref[...]).astype(BF16)

        qscale = dk ** -0.5
        nrow = sl // CONV_PHASES
        for cb in range(ncb):
            cs = slice(cb * dk, (cb + 1) * dk)
            which, hh = divmod(cb, GDN_HEADS)
            for s in range(nseq):
                for r in range(CONV_PHASES):
                    y = None
                    for t in range(GDN_CONV):
                        tap = (cbuf[s, cb, pl.ds(CONV_PAD + r - t, nrow, stride=CONV_PHASES), :]
                               * convw_ref[GDN_CONV - 1 - t:GDN_CONV - t, cs])
                        y = tap if y is None else y + tap
                    y = y * _sigmoid(y)
                    out_rows = pl.ds(s * sl + r, nrow, stride=CONV_PHASES)
                    if which == 0:
                        q_ref[hh, out_rows, :] = y * (lax.rsqrt(jnp.sum(y * y, axis=-1, keepdims=True) + L2_EPS)
                                                      * qscale)
                    elif which == 1:
                        k_ref[hh, out_rows, :] = y * lax.rsqrt(jnp.sum(y * y, axis=-1, keepdims=True) + L2_EPS)
                    else:
                        v_ref[hh, out_rows, :] = y
        for s in range(nseq):
            for cb in range(ncb):
                ctail = cbuf[s, cb, sl:sl + CONV_PAD, :]
                convst_ref[s, :, cb * dk:(cb + 1) * dk] = ctail
                cbuf[s, cb, 0:CONV_PAD, :] = ctail

    matmul_stage()
    vector_stage()


def _res(w):
    if isinstance(w, tuple):
        arr, l = w
        nd = arr.ndim - 1
        return pl.BlockSpec((None,) + arr.shape[1:], lambda *_: (l,) + (0,) * nd, pipeline_mode=pl.Buffered(1))
    nd = w.ndim
    return pl.BlockSpec(w.shape, lambda *_: (0,) * nd, pipeline_mode=pl.Buffered(1))


def _arr(w):
    return w[0] if isinstance(w, tuple) else w


def _cast_kernel(x_ref, o_ref):
    o_ref[...] = x_ref[...].astype(BF16)


def _cast_bf16(w, block_rows=256):
    depth, rows, c = w.shape
    br = min(block_rows, rows)
    blk = pl.BlockSpec((None, br, c), lambda l, i: (l, i, 0))
    return pl.pallas_call(
        _cast_kernel,
        grid=(depth, rows // br),
        in_specs=[blk],
        out_specs=blk,
        out_shape=jax.ShapeDtypeStruct((depth, rows, c), BF16),
        compiler_params=pltpu.CompilerParams(dimension_semantics=("parallel", "parallel")),
        name="cast_bf16",
    )(w)


def _cast_transposed_kernel(x_ref, o_ref, *, valid):
    cols = pl.program_id(1) * x_ref.shape[0] + lax.broadcasted_iota(jnp.int32, x_ref.shape, 0)
    o_ref[...] = jnp.where(cols < valid, x_ref[...], 0.0).T.astype(BF16)


def _cast_bf16_transposed(wt, cols):
    depth, c, rows = wt.shape
    return pl.pallas_call(
        functools.partial(_cast_transposed_kernel, valid=c),
        grid=(depth, cols // LANES),
        in_specs=[pl.BlockSpec((None, LANES, rows), lambda l, i: (l, i, 0))],
        out_specs=pl.BlockSpec((None, rows, LANES), lambda l, i: (l, 0, i)),
        out_shape=jax.ShapeDtypeStruct((depth, rows, cols), BF16),
        compiler_params=pltpu.CompilerParams(dimension_semantics=("parallel", "parallel")),
        name="cast_bf16_t",
    )(wt)


def _ffn_in(x2d, pool0, conv0, lw, tm, seq_len, pos0):
    t, d = x2d.shape
    d_ff = _arr(lw["wg1"]).shape[-1]
    o_pool, o_qkv, o_z, n_pad = lw["splits"]
    sgu_w, pool_w, gdn_w = lw["sgn"].shape[1], lw["pools"].shape[1], lw["onorm"].shape[1]
    sl = min(seq_len, tm)
    nseq = tm // sl
    tps = seq_len // sl
    row = lambda width: pl.BlockSpec((tm, width), lambda i: (i, 0))
    hist = lambda pad, width: pl.BlockSpec((nseq, pad, width), lambda i: (i // tps, 0, 0))
    res = _res
    weights = [lw["g1"], lw["wg1"], lw["wu1"], lw["wd1"], lw["gmix"], lw["win"],
               lw["sgn"], lw["avg"], lw["poolw"], lw["pools"], lw["convw"]]
    outs = [(d, F32), (n_pad - o_z, F32), (sgu_w, F32), (sgu_w, F32), (pool_w, BF16)]
    dk = gdn_w // GDN_HEADS
    head_rows = pl.BlockSpec((GDN_HEADS, tm, dk), lambda i: (0, i, 0))
    nbatch = t // seq_len
    return pl.pallas_call(
        functools.partial(_ffn_in_kernel, splits=lw["splits"], sl=sl, tps=tps, pos0=pos0),
        grid=(t // tm,),
        in_specs=[row(d), hist(POOL_PAD, pool_w), hist(CONV_PAD, 3 * gdn_w)] + [res(w) for w in weights],
        out_specs=[row(w) for w, _ in outs] + [head_rows] * 3 + [hist(POOL_PAD, pool_w), hist(CONV_PAD, 3 * gdn_w)],
        out_shape=[jax.ShapeDtypeStruct((t, w), dt) for w, dt in outs]
                  + [jax.ShapeDtypeStruct((GDN_HEADS, t, dk), F32)] * 3
                  + [jax.ShapeDtypeStruct((nbatch, POOL_PAD, pool_w), F32),
                     jax.ShapeDtypeStruct((nbatch, CONV_PAD, 3 * gdn_w), F32)],
        scratch_shapes=[pltpu.VMEM((tm, d), BF16), pltpu.VMEM((tm, d_ff), BF16),
                        pltpu.VMEM((tm, o_pool), F32),
                        pltpu.VMEM((nseq, POOL_PAD + sl, pool_w), F32),
                        pltpu.VMEM((nseq, 3 * GDN_HEADS, CONV_PAD + sl, dk), F32)],
        compiler_params=pltpu.CompilerParams(dimension_semantics=("arbitrary",),
                                             vmem_limit_bytes=VMEM_LIMIT_BYTES),
        name="ffn_in",
    )(x2d, pool0, conv0, *(_arr(w) for w in weights))


def _mixer_kernel(ug_ref, vn_ref, yb_ref, q_ref, k_ref, v_ref, zg_ref, h_ref, ssm0_ref,
                  sgw_ref, sgb_ref, alog_ref, dtb_ref, onorm_ref, wout_f32_ref,
                  h2_ref, ssm_ref,
                  u_s, wq_s, ql_s, y_s, s_s, wout_ref, *, ts, c_sgu):
    j = pl.program_id(1)
    nj = pl.num_programs(1)
    ns = vn_ref.shape[0]
    sgu_w = vn_ref.shape[2]
    pool_w = yb_ref.shape[2]
    gdn_w = onorm_ref.shape[1]
    dk = gdn_w // GDN_HEADS
    cc = GDN_CHUNK
    nch = ts // cc
    pre_w = sgu_w + pool_w
    seqs = range(ns)

    @pl.when(jnp.logical_and(pl.program_id(0) == 0, j == 0))
    def _():
        wout_ref[...] = wout_f32_ref[...].astype(BF16)

    @pl.when(j == 0)
    def _():
        s_s[...] = ssm0_ref[...]

    hd = sgu_w // SGU_HEADS
    wi = lax.broadcasted_iota(jnp.int32, (c_sgu, SGU_HEADS * c_sgu), 0)
    wj = lax.broadcasted_iota(jnp.int32, (c_sgu, SGU_HEADS * c_sgu), 1) % c_sgu
    wmask = jnp.where(wi // SGU_BLOCK >= wj // SGU_BLOCK, sgw_ref[...], 0.0).astype(BF16)
    lane_head = lax.broadcasted_iota(jnp.int32, (c_sgu, sgu_w), 1) // hd
    for e in seqs:
        for c in range(ts // c_sgu):
            rows = slice(c * c_sgu, (c + 1) * c_sgu)
            vch = vn_ref[e, rows, :]
            vstack = jnp.concatenate([jnp.where(lane_head == hh, vch, 0.0) for hh in range(SGU_HEADS)],
                                     axis=0).astype(BF16)
            s = _dot(wmask, vstack) + sgb_ref[...]
            y_s[e * ts + c * c_sgu:e * ts + (c + 1) * c_sgu, 0:sgu_w] = (ug_ref[e, rows, :] * s).astype(BF16)
        y_s[e * ts:(e + 1) * ts, sgu_w:pre_w] = yb_ref[e]

    ti = lax.broadcasted_iota(jnp.int32, (2 * ts, ts), 0)
    tj = lax.broadcasted_iota(jnp.int32, (2 * ts, ts), 1)
    tr = jnp.where(ti < ts, ti, ti - ts)
    same_chunk = tr // cc == tj // cc
    summat = jnp.where(same_chunk, jnp.where(ti < ts, jnp.where(tj <= tr, 1.0, 0.0), 1.0), 0.0).astype(BF16)
    beta, gc, egc, erg, etot = [], [], [], [], []
    for e in seqs:
        gates = zg_ref[e, :, gdn_w:gdn_w + LANES]
        beta.append(_sigmoid(gates))
        g = -jnp.exp(alog_ref[...]) * _softplus(gates + dtb_ref[...])
        g_hi = g.astype(BF16)
        g_r = g - g_hi.astype(F32)
        g_mid = g_r.astype(BF16)
        g_lo = (g_r - g_mid.astype(F32)).astype(BF16)
        sums = _dot(summat, g_hi) + _dot(summat, g_mid) + _dot(summat, g_lo)
        gc.append(sums[0:ts])
        tot = sums[ts:2 * ts]
        egc.append(jnp.exp(gc[e]))
        erg.append(jnp.exp(tot - gc[e]))
        etot.append(jnp.exp(tot))

    ri = lax.broadcasted_iota(jnp.int32, (cc, cc), 0)
    ci = lax.broadcasted_iota(jnp.int32, (cc, cc), 1)
    incl = ri >= ci
    strict = ri > ci
    chains = [(e, n, hh) for n in range(nch) for e in seqs for hh in range(GDN_HEADS)]
    gct = {(e, n): gc[e][n * cc:(n + 1) * cc].T for e in seqs for n in range(nch)}

    nmats = []
    for e, n, hh in chains:
        rows = slice(n * cc, (n + 1) * cc)
        hs = slice(hh * dk, (hh + 1) * dk)
        gl = GDN_HEADS + hh
        k = k_ref[hh, e, rows, :]
        kbq = jnp.concatenate([k * beta[e][rows, hh:hh + 1], q_ref[hh, e, rows, :]], axis=0).astype(BF16)
        prod = _dot_nt(kbq, k.astype(BF16))
        decay = jnp.exp(jnp.where(incl, gc[e][rows, gl:gl + 1] - gct[e, n][gl:gl + 1, :], NEG_BIG))
        nmats.append(jnp.where(strict, -(prod[0:cc] * decay), 0.0))
        ql_s[e, n, hh, 0:cc, :] = (prod[cc:2 * cc] * decay).astype(BF16)
    tps = list(nmats)
    pows = list(nmats)
    for _ in range(5):
        pows = [_dot(m.astype(BF16), m.astype(BF16)) for m in pows]
        tps = [t + m + _dot(t.astype(BF16), m.astype(BF16)) for t, m in zip(tps, pows)]
    for (e, n, hh), tp in zip(chains, tps):
        rows = slice(n * cc, (n + 1) * cc)
        hs = slice(hh * dk, (hh + 1) * dk)
        gl = GDN_HEADS + hh
        k = k_ref[hh, e, rows, :]
        b = beta[e][rows, hh:hh + 1]
        eg = egc[e][rows, gl:gl + 1]
        rhs = jnp.concatenate([v_ref[hh, e, rows, :] * b, k * (b * eg)], axis=1)
        sol = rhs + _dot(tp.astype(BF16), rhs.astype(BF16))
        u_s[e, rows, hs] = sol[:, 0:dk]
        wq_s[e, n, hh, 0:cc, :] = sol[:, dk:2 * dk].astype(BF16)
        wq_s[e, n, hh, cc:2 * cc, :] = (q_ref[hh, e, rows, :] * eg).astype(BF16)
        ql_s[e, n, hh, cc:cc + dk, :] = (k * erg[e][rows, gl:gl + 1]).T.astype(BF16)

    fill = []
    for kb in range(pre_w // MXU_COLS):
        for nb in range(h_ref.shape[2] // MXU_COLS):
            fill.append((slice(kb * MXU_COLS, (kb + 1) * MXU_COLS), slice(nb * MXU_COLS, (nb + 1) * MXU_COLS), kb == 0))
    gaps = 2 * nch

    def run_fill(gap):
        for ks, cs, first in fill[gap * len(fill) // gaps:(gap + 1) * len(fill) // gaps]:
            part = _dot(y_s[:, ks], wout_ref[ks, cs])
            for e in seqs:
                base = h_ref[e, :, cs] if first else h2_ref[e, :, cs]
                h2_ref[e, :, cs] = base + part[e * ts:(e + 1) * ts]

    heads = [(e, hh) for e in seqs for hh in range(GDN_HEADS)]
    for n in range(nch):
        rows = slice(n * cc, (n + 1) * cc)
        sts = [s_s[e, hh] for e, hh in heads]
        r1 = [_dot(wq_s[e, n, hh], st.astype(BF16)) for (e, hh), st in zip(heads, sts)]
        run_fill(2 * n)
        vnew = [u_s[e, rows, hh * dk:(hh + 1) * dk] - r[0:cc] for (e, hh), r in zip(heads, r1)]
        r2 = [_dot(ql_s[e, n, hh], vn.astype(BF16)) for (e, hh), vn in zip(heads, vnew)]
        run_fill(2 * n + 1)
        for (e, hh), st, ra, rb in zip(heads, sts, r1, r2):
            gl = GDN_HEADS + hh
            hs = slice(hh * dk, (hh + 1) * dk)
            glcol = etot[e][rows, gl:gl + 1]
            s_s[e, hh] = st * jnp.concatenate([glcol] * (dk // cc), axis=0) + rb[cc:cc + dk]
            o = ra[cc:2 * cc] + rb[0:cc]
            z = zg_ref[e, rows, hs]
            on = o * lax.rsqrt(jnp.mean(o * o, axis=-1, keepdims=True) + EPS) * onorm_ref[:, hs]
            y_s[e * ts + n * cc:e * ts + (n + 1) * cc, pre_w + hh * dk:pre_w + (hh + 1) * dk] = (
                on * (z * _sigmoid(z))).astype(BF16)

    @pl.when(j == nj - 1)
    def _():
        ssm_ref[...] = s_s[...]

    part = _dot(y_s[:, pre_w:], wout_ref[pre_w:, :])
    for e in seqs:
        h2_ref[e] = h2_ref[e] + part[e * ts:(e + 1) * ts]


def _mixer(h, ug, vn, yb, q, k, v, zg, ssm0, lw, ts, c_sgu, ns):
    b, l, d = h.shape
    sgu_w, pool_w, dk = vn.shape[2], yb.shape[2], q.shape[3]
    gdn_w = GDN_HEADS * dk
    nch = ts // GDN_CHUNK
    mix_w = sgu_w + pool_w + gdn_w
    tile = lambda width: pl.BlockSpec((ns, ts, width), lambda i, j: (i, j, 0))
    per_b = lambda *shape: pl.BlockSpec((ns,) + shape, lambda i, j: (i,) + (0,) * len(shape))
    heads_tile = pl.BlockSpec((GDN_HEADS, ns, ts, dk), lambda i, j: (0, i, j, 0))
    res = _res
    weights = [lw["sgw"], lw["sgb"], lw["alog"], lw["dtb"], lw["onorm"], lw["wout"]]
    return pl.pallas_call(
        functools.partial(_mixer_kernel, ts=ts, c_sgu=c_sgu),
        grid=(b // ns, l // ts),
        in_specs=[tile(sgu_w), tile(sgu_w), tile(pool_w), heads_tile, heads_tile, heads_tile,
                  tile(zg.shape[2]), tile(d), per_b(GDN_HEADS, dk, dk)] + [res(w) for w in weights],
        out_specs=[tile(d), per_b(GDN_HEADS, dk, dk)],
        out_shape=[jax.ShapeDtypeStruct((b, l, d), F32), jax.ShapeDtypeStruct((b, GDN_HEADS, dk, dk), F32)],
        scratch_shapes=[
            pltpu.VMEM((ns, ts, gdn_w), F32),
            pltpu.VMEM((ns, nch, GDN_HEADS, 2 * GDN_CHUNK, dk), BF16),
            pltpu.VMEM((ns, nch, GDN_HEADS, GDN_CHUNK + dk, GDN_CHUNK), BF16),
            pltpu.VMEM((ns * ts, mix_w), BF16),
            pltpu.VMEM((ns, GDN_HEADS, dk, dk), F32),
            pltpu.VMEM((mix_w, d), BF16),
        ],
        compiler_params=pltpu.CompilerParams(dimension_semantics=("arbitrary", "arbitrary"),
                                             vmem_limit_bytes=VMEM_LIMIT_BYTES),
        name="mixer",
    )(ug, vn, yb, q, k, v, zg, h, ssm0, *(_arr(w) for w in weights))


def _attn_ffn_kernel(h_ref, mk_ref, mv_ref, gx_ref, wq_f32_ref, wo_f32_ref, g2_ref, wg_ref, wu_ref, wd_ref, gf_ref,
                     out_ref, nb_ref, act_ref, ob_ref, kh_s, vh_s, wq_ref, wo_ref, *, final, sl, tps):
    nseq = h_ref.shape[0] // sl
    heads, dh = mk_ref.shape[2], mk_ref.shape[3]

    @pl.when(pl.program_id(0) == 0)
    def _():
        wq_ref[...] = wq_f32_ref[...].astype(BF16)
        wo_ref[...] = wo_f32_ref[...].astype(BF16)

    @pl.when(pl.program_id(0) % tps == 0)
    def _():
        for s in range(nseq):
            for hh in range(heads):
                kh_s[s, hh] = mk_ref[s, :, hh, :].astype(BF16)
                vh_s[s, hh] = mv_ref[s, :, hh, :].astype(BF16)

    h = h_ref[...]
    nb_ref[...] = _rms(h, gx_ref[...]).astype(BF16)
    q = _dot(nb_ref[...], wq_ref[...])
    scale = dh ** -0.5
    pairs = [(s, hh, slice(s * sl, (s + 1) * sl), slice(hh * dh, (hh + 1) * dh))
             for s in range(nseq) for hh in range(heads)]
    scs = [_dot_nt(q[rows, hs].astype(BF16), kh_s[s, hh]) * scale for s, hh, rows, hs in pairs]
    es = [jnp.exp(sc - jnp.max(sc, axis=-1, keepdims=True)) for sc in scs]
    prs = [e / jnp.sum(e, axis=-1, keepdims=True) for e in es]
    for (s, hh, rows, hs), pr in zip(pairs, prs):
        ob_ref[rows, hs] = _dot(pr.astype(BF16), vh_s[s, hh]).astype(BF16)
    h = h + _dot(ob_ref[...], wo_ref[...])
    nb_ref[...] = _rms(h, g2_ref[...]).astype(BF16)
    h = h + 0.5 * _swiglu(nb_ref, wg_ref, wu_ref, wd_ref, act_ref)
    if final:
        h = _rms(h, gf_ref[...])
    out_ref[...] = h


def _attn_ffn(h2d, mk, mv, l, lw, gf, tm, seq_len, final):
    t, d = h2d.shape
    d_ff = _arr(lw["wg2"]).shape[-1]
    m, heads, dh = mk.shape[2:]
    sl = min(seq_len, tm)
    nseq = tm // sl
    tps = seq_len // sl
    tile = pl.BlockSpec((tm, d), lambda i: (i, 0))
    mem = pl.BlockSpec((None, nseq, m, heads, dh), lambda i: (l, i // tps, 0, 0, 0))
    weights = [lw["gx"], lw["wmq"], lw["wmo"], lw["g2"], lw["wg2"], lw["wu2"], lw["wd2"], gf]
    return pl.pallas_call(
        functools.partial(_attn_ffn_kernel, final=final, sl=sl, tps=tps),
        grid=(t // tm,),
        in_specs=[tile, mem, mem] + [_res(w) for w in weights],
        out_specs=tile,
        out_shape=jax.ShapeDtypeStruct((t, d), F32),
        scratch_shapes=[pltpu.VMEM((tm, d), BF16), pltpu.VMEM((tm, d_ff), BF16), pltpu.VMEM((tm, heads * dh), BF16),
                        pltpu.VMEM((nseq, heads, m, dh), BF16), pltpu.VMEM((nseq, heads, m, dh), BF16),
                        pltpu.VMEM((d, heads * dh), BF16), pltpu.VMEM((heads * dh, d), BF16)],
        compiler_params=pltpu.CompilerParams(dimension_semantics=("arbitrary",),
                                             vmem_limit_bytes=VMEM_LIMIT_BYTES),
        name="attn_ffn",
    )(h2d, mk, mv, *(_arr(w) for w in weights))


def _mem_kv_kernel(mem_ref, g_ref, wk_ref, wv_ref, k_ref, v_ref):
    mb = _rms(mem_ref[...], g_ref[...]).astype(BF16)
    k = _dot(mb, wk_ref[...].astype(BF16))
    v = _dot(mb, wv_ref[...].astype(BF16))
    heads, dh = k_ref.shape[1], k_ref.shape[2]
    for hh in range(heads):
        k_ref[:, hh, :] = k[:, hh * dh:(hh + 1) * dh]
        v_ref[:, hh, :] = v[:, hh * dh:(hh + 1) * dh]


def _mem_kv(mem, g, wk, wv, heads):
    b, m, d = mem.shape
    depth, _, mw = wk.shape
    per_layer = lambda *shape: pl.BlockSpec((None,) + shape, lambda l, i: (l,) + (0,) * len(shape))
    out = pl.BlockSpec((None, None, m, heads, mw // heads), lambda l, i: (l, i, 0, 0, 0))
    return pl.pallas_call(
        _mem_kv_kernel,
        grid=(depth, b),
        in_specs=[pl.BlockSpec((None, m, d), lambda l, i: (i, 0, 0)), per_layer(1, d), per_layer(d, mw),
                  per_layer(d, mw)],
        out_specs=[out, out],
        out_shape=[jax.ShapeDtypeStruct((depth, b, m, heads, mw // heads), F32)] * 2,
        compiler_params=pltpu.CompilerParams(dimension_semantics=("parallel", "parallel")),
        name="mem_kv",
    )(mem, g.reshape(depth, 1, d), wk, wv)


def _layer_weights(l, c_sgu, big, ffn1_norm, mix_norm, sgu_norm, sgu_w, sgu_b, pool_w, pool_scale, gdn_conv_w,
                   gdn_a_log, gdn_dt_bias, gdn_out_norm, xattn_norm, ffn2_norm):
    sgu_width = sgu_norm.shape[1]
    pool_width = pool_scale.shape[1]
    gdn_width = gdn_out_norm.shape[1] * GDN_HEADS
    row = lambda vec: vec.reshape(1, -1).astype(F32)
    groups = len(POOL_WINDOWS)
    gdim = pool_width // groups
    poolw = jnp.zeros((pool_width, pool_width), F32)
    for gi in range(groups):
        poolw = poolw.at[gi * gdim:(gi + 1) * gdim, gi * gdim:(gi + 1) * gdim].set(pool_w[l, gi])
    hd = sgu_width // SGU_HEADS
    seg = jnp.arange(sgu_width) // hd
    avg = jnp.where(seg[:, None] == seg[None, :], 1.0 / hd, 0.0).astype(BF16)
    sgw = jnp.concatenate([sgu_w[l, hh, :c_sgu, :c_sgu] for hh in range(SGU_HEADS)], axis=1)
    sgb = jnp.repeat(sgu_b[l, :, :c_sgu].T, hd, axis=1)
    lane_pad = lambda vec: jnp.pad(vec, (GDN_HEADS, LANES - 2 * GDN_HEADS)).reshape(1, LANES)
    lw = {name: (arr, l) for name, arr in big.items()}
    lw.update(
        splits=_in_splits(sgu_width, pool_width, gdn_width),
        g1=row(ffn1_norm[l]), gmix=row(mix_norm[l]),
        sgw=sgw, sgb=sgb, sgn=row(sgu_norm[l]), avg=avg,
        poolw=poolw.astype(BF16), pools=row(pool_scale[l]),
        convw=gdn_conv_w[l], alog=lane_pad(gdn_a_log[l]), dtb=lane_pad(gdn_dt_bias[l]),
        onorm=row(jnp.tile(gdn_out_norm[l], GDN_HEADS)),
        gx=row(xattn_norm[l]), g2=row(ffn2_norm[l]),
    )
    return lw


def _in_splits(sgu_width, pool_width, gdn_width):
    o_pool = 2 * sgu_width
    o_qkv = o_pool + pool_width
    o_z = o_qkv + 3 * gdn_width
    return o_pool, o_qkv, o_z, o_z + gdn_width + LANES


def _tiles(batch, seq):
    total = batch * seq
    return min(FFN_IN_ROWS, total), min(MIXER_ROWS, seq), min(ATTN_FFN_ROWS, total)


def _layer(x, mk, mv, l, pool0, conv0, ssm0, lw, gf, final, pos0):
    b, seq, d = x.shape
    tm, ts, tm_attn = _tiles(b, seq)
    h, zg, ug, vn, yb, q, k, v, pool_new, conv_new = _ffn_in(x.reshape(b * seq, d), pool0, conv0, lw, tm, seq, pos0)
    shp = lambda arr: arr.reshape(b, seq, arr.shape[-1])
    c_sgu = lw["sgb"].shape[0]
    hshp = lambda arr: arr.reshape(arr.shape[0], b, seq, arr.shape[-1])
    h2, ssm_new = _mixer(shp(h), shp(ug), shp(vn), shp(yb), hshp(q), hshp(k), hshp(v), shp(zg), ssm0, lw, ts, c_sgu,
                         MIXER_SEQS)
    out = _attn_ffn(h2.reshape(b * seq, d), mk, mv, l, lw, gf, tm_attn, seq, final)
    return (shp(out), shp(vn), pool_new[:, POOL_PAD - POOL_HIST:], conv_new[:, CONV_PAD - (GDN_CONV - 1):], ssm_new)


def kernel(x_prompt, x_sample, mem_prompt, cache_mem_k, cache_mem_v, state_pool, state_conv, state_ssm, ffn1_norm, ffn1_w_gate, ffn1_w_up, ffn1_w_down, mix_norm, w_in, sgu_norm, sgu_w, sgu_b, pool_w, pool_scale, gdn_conv_w, gdn_a_log, gdn_dt_bias, gdn_out_norm, w_out, xattn_norm, mem_norm, w_mq, w_mk, w_mv, w_mo, ffn2_norm, ffn2_w_gate, ffn2_w_up, ffn2_w_down, final_norm):
    depth = w_in.shape[0]
    bp, lp, d = x_prompt.shape
    bs, ls, _ = x_sample.shape
    pool_width = state_pool.shape[-1]
    qkv_width = state_conv.shape[-1]
    dk = state_ssm.shape[-1]
    n_pad = _in_splits(sgu_norm.shape[1], pool_width, gdn_out_norm.shape[1] * GDN_HEADS)[3]
    big = dict(wg1=_cast_bf16(ffn1_w_gate), wu1=_cast_bf16(ffn1_w_up), wd1=_cast_bf16(ffn1_w_down),
               win=_cast_bf16_transposed(jnp.swapaxes(w_in, 1, 2), n_pad), wout=w_out, wmq=w_mq, wmo=w_mo,
               wg2=_cast_bf16(ffn2_w_gate), wu2=_cast_bf16(ffn2_w_up), wd2=_cast_bf16(ffn2_w_down))
    small = (ffn1_norm, mix_norm, sgu_norm, sgu_w, sgu_b, pool_w, pool_scale, gdn_conv_w, gdn_a_log, gdn_dt_bias,
             gdn_out_norm, xattn_norm, ffn2_norm)
    gf = final_norm.reshape(1, d)
    prompt_mk, prompt_mv = _mem_kv(mem_prompt, mem_norm, w_mk, w_mv, cache_mem_k.shape[3])
    pool0_p = jnp.zeros((bp, POOL_PAD, pool_width), F32)
    conv0_p = jnp.zeros((bp, CONV_PAD, qkv_width), F32)
    ssm0_p = jnp.zeros((bp, GDN_HEADS, dk, dk), F32)
    xp, xs = x_prompt, x_sample
    outs_p = [[] for _ in range(3)]
    outs_s = [[] for _ in range(4)]
    for l in range(depth):
        final = l == depth - 1
        lw_p = _layer_weights(l, min(128, lp), big, *small)
        lw_s = lw_p if min(128, ls) == min(128, lp) else _layer_weights(l, min(128, ls), big, *small)
        xp, _, pp, pc, ps = _layer(xp, prompt_mk, prompt_mv, l, pool0_p, conv0_p, ssm0_p, lw_p, gf, final, 0)
        pool0_s = jnp.pad(state_pool[l], ((0, 0), (POOL_PAD - POOL_HIST, 0), (0, 0)))
        conv0_s = jnp.pad(state_conv[l], ((0, 0), (CONV_PAD - (GDN_CONV - 1), 0), (0, 0)))
        xs, sv, sp, sc, ss = _layer(xs, cache_mem_k, cache_mem_v, l, pool0_s, conv0_s, state_ssm[l], lw_s, gf, final,
                                    PAST_LEN)
        for lst, val in zip(outs_p, (pp, pc, ps)):
            lst.append(val)
        for lst, val in zip(outs_s, (sp, sc, ss, sv)):
            lst.append(val)
    return (xp, xs, *(jnp.stack(v) for v in outs_p), prompt_mk, prompt_mv, *(jnp.stack(v) for v in outs_s))
```

```python
import functools
import math

import jax
import jax.numpy as jnp
from jax import lax
from jax.experimental import pallas as pl
from jax.experimental.pallas import tpu as pltpu

F32 = jnp.float32
BF16 = jnp.bfloat16

EPS = 1e-6
L2_EPS = 1e-6
LANES = 128
MXU_COLS = 256
VMEM_LIMIT_BYTES = 58 * 1024 * 1024

SGU_HEADS = 4
SGU_BLOCK = 64
POOL_WINDOWS = (2, 4, 8, 16)
POOL_HIST = max(POOL_WINDOWS) - 1
POOL_PAD = 16
GDN_HEADS = 4
GDN_CONV = 4
CONV_PAD = 8
CONV_PHASES = 4
GDN_CHUNK = 64
MEM_HEADS = 4
PAST_LEN = 2048
NEG_BIG = -1e30
MIXER_SEQS = 2
FFN_IN_ROWS = 512
ATTN_FFN_ROWS = 1024
MIXER_ROWS = 256
CAST_BLOCK_BYTES = 3 * 1024 * 1024
CAST_T_COLS = 512


def _dot(a, b):
    return jnp.dot(a, b, preferred_element_type=F32)


def _dot_nt(a, b):
    return lax.dot_general(a, b, (((1,), (1,)), ((), ())), preferred_element_type=F32)


def _rms(x, g):
    return x * lax.rsqrt(jnp.mean(x * x, axis=-1, keepdims=True) + EPS) * g


def _sigmoid(x):
    return 1.0 / (1.0 + jnp.exp(-x))


def _gelu_tanh(x):
    return 0.5 * x * (1.0 + jnp.tanh(math.sqrt(2.0 / math.pi) * (x + 0.044715 * (x * x * x))))


def _softplus(x):
    return jnp.maximum(x, 0.0) + jnp.log1p(jnp.exp(-jnp.abs(x)))


def _swiglu(nb_ref, wg_ref, wu_ref, wd_ref, act_ref):
    d_ff = wg_ref.shape[1]
    for c in range(d_ff // MXU_COLS):
        sl = slice(c * MXU_COLS, (c + 1) * MXU_COLS)
        g = _dot(nb_ref[...], wg_ref[:, sl])
        u = _dot(nb_ref[...], wu_ref[:, sl])
        act_ref[:, sl] = (g * _sigmoid(g) * u).astype(BF16)
    return _dot(act_ref[...], wd_ref[...])


def _ffn_in_kernel(x_ref, pool0_ref, conv0_ref, g1_ref, wg_ref, wu_ref, wd_ref, gm_ref, win_ref,
                   sgn_ref, avg_ref, poolw_ref, pools_ref, convw_ref,
                   h_ref, zg_ref, ug_ref, vn_ref, yb_ref, q_ref, k_ref, v_ref, poolst_ref, convst_ref,
                   nb_ref, act_ref, uv_s, pbuf, cbuf, *, splits, sl, tps, pos0):
    i = pl.program_id(0)
    tm = x_ref.shape[0]
    nseq = tm // sl
    sgu_w = sgn_ref.shape[1]
    pool_w = pools_ref.shape[1]
    dk = q_ref.shape[2]
    gdn_w = GDN_HEADS * dk
    ncb = 3 * GDN_HEADS
    o_pool, o_qkv, o_z, n_pad = splits
    iv = i

    @pl.when(i % tps == 0)
    def _():
        pbuf[:, 0:POOL_PAD, :] = pool0_ref[...]
        for cb in range(ncb):
            cbuf[:, cb, 0:CONV_PAD, :] = conv0_ref[:, :, cb * dk:(cb + 1) * dk]

    def matmul_stage():
        x = x_ref[...]
        nb_ref[...] = _rms(x, g1_ref[...]).astype(BF16)
        h = x + 0.5 * _swiglu(nb_ref, wg_ref, wu_ref, wd_ref, act_ref)
        h_ref[...] = h
        nb_ref[...] = _rms(h, gm_ref[...]).astype(BF16)
        uv_s[...] = _dot(nb_ref[...], win_ref[:, 0:o_pool])
        zg_ref[...] = _dot(nb_ref[...], win_ref[:, o_z:n_pad])
        p = _dot(nb_ref[...], win_ref[:, o_pool:o_qkv])
        qkv = _dot(nb_ref[...], win_ref[:, o_qkv:o_z])
        for s in range(nseq):
            pbuf[s, POOL_PAD:POOL_PAD + sl, :] = p[s * sl:(s + 1) * sl]
            for cb in range(ncb):
                cbuf[s, cb, CONV_PAD:CONV_PAD + sl, :] = qkv[s * sl:(s + 1) * sl, cb * dk:(cb + 1) * dk]

    def vector_stage():
        uv = _gelu_tanh(uv_s[...])
        ug_ref[...] = uv[:, 0:sgu_w]
        v = uv[:, sgu_w:]
        avg = avg_ref[...]

        def seg_mean(t):
            hi = t.astype(BF16)
            lo = (t - hi.astype(F32)).astype(BF16)
            return _dot(hi, avg) + _dot(lo, avg)

        vc = v - seg_mean(v)
        vn_ref[...] = vc * lax.rsqrt(seg_mean(vc * vc) + EPS) * sgn_ref[...]

        gdim = pool_w // len(POOL_WINDOWS)
        lane = lax.broadcasted_iota(jnp.int32, (sl, LANES), 1)
        pos = pos0 + (iv % tps) * sl + lax.broadcasted_iota(jnp.int32, (sl, LANES), 0)
        dparts = []
        for s in range(nseq):
            dblk = []
            for blk in range(pool_w // LANES):
                ls = slice(blk * LANES, (blk + 1) * LANES)
                wins = [POOL_WINDOWS[(blk * LANES + l0) // gdim] for l0 in range(0, LANES, gdim)]
                wl = jnp.full((sl, LANES), wins[-1], jnp.int32)
                for gi in range(len(wins) - 2, -1, -1):
                    wl = jnp.where(lane < (gi + 1) * gdim, wins[gi], wl)
                pb = pbuf[s, POOL_PAD:POOL_PAD + sl, ls]
                acc = pb
                prev = 1
                for w in sorted(set(wins)):
                    part = None
                    for jj in range(prev, w):
                        sh = pbuf[s, POOL_PAD - jj:POOL_PAD - jj + sl, ls]
                        part = sh if part is None else part + sh
                    if part is not None:
                        acc = acc + (part if w == min(wins) else jnp.where(wl >= w, part, 0.0))
                    prev = w
                cnt = jnp.minimum(wl, pos + 1).astype(F32)
                dblk.append(acc / cnt - pb)
            dparts.append(jnp.concatenate(dblk, axis=1))
            tail = pbuf[s, sl:sl + POOL_PAD, :]
            poolst_ref[s] = tail
            pbuf[s, 0:POOL_PAD, :] = tail
        dlt = (dparts[0] if nseq == 1 else jnp.concatenate(dparts, axis=0)).astype(BF16)
        yb_ref[...] = (_dot(dlt, poolw_ref[...]) * pools_ref[...]).astype(BF16)

        qscale = dk ** -0.5
        nrow = sl // CONV_PHASES
        for cb in range(ncb):
            cs = slice(cb * dk, (cb + 1) * dk)
            which, hh = divmod(cb, GDN_HEADS)
            for s in range(nseq):
                for r in range(CONV_PHASES):
                    y = None
                    for t in range(GDN_CONV):
                        tap = (cbuf[s, cb, pl.ds(CONV_PAD + r - t, nrow, stride=CONV_PHASES), :]
                               * convw_ref[GDN_CONV - 1 - t:GDN_CONV - t, cs])
                        y = tap if y is None else y + tap
                    y = y * _sigmoid(y)
                    out_rows = pl.ds(s * sl + r, nrow, stride=CONV_PHASES)
                    if which == 0:
                        q_ref[hh, out_rows, :] = y * (lax.rsqrt(jnp.sum(y * y, axis=-1, keepdims=True) + L2_EPS)
                                                      * qscale)
                    elif which == 1:
                        k_ref[hh, out_rows, :] = y * lax.rsqrt(jnp.sum(y * y, axis=-1, keepdims=True) + L2_EPS)
                    else:
                        v_ref[hh, out_rows, :] = y
        for s in range(nseq):
            for cb in range(ncb):
                ctail = cbuf[s, cb, sl:sl + CONV_PAD, :]
                convst_ref[s, :, cb * dk:(cb + 1) * dk] = ctail
                cbuf[s, cb, 0:CONV_PAD, :] = ctail

    matmul_stage()
    vector_stage()


def _res(w):
    if isinstance(w, tuple):
        arr, l = w
        nd = arr.ndim - 1
        return pl.BlockSpec((None,) + arr.shape[1:], lambda *_: (l,) + (0,) * nd, pipeline_mode=pl.Buffered(1))
    nd = w.ndim
    return pl.BlockSpec(w.shape, lambda *_: (0,) * nd, pipeline_mode=pl.Buffered(1))


def _arr(w):
    return w[0] if isinstance(w, tuple) else w


def _cast_kernel(x_ref, o_ref):
    o_ref[...] = x_ref[...].astype(BF16)


def _cast_bf16(w):
    depth, rows, c = w.shape
    br = max(r for r in range(8, rows + 1, 8) if rows % r == 0 and (r * c * 4 <= CAST_BLOCK_BYTES or r == 8))
    blk = pl.BlockSpec((None, br, c), lambda l, i: (l, i, 0))
    return pl.pallas_call(
        _cast_kernel,
        grid=(depth, rows // br),
        in_specs=[blk],
        out_specs=blk,
        out_shape=jax.ShapeDtypeStruct((depth, rows, c), BF16),
        compiler_params=pltpu.CompilerParams(dimension_semantics=("parallel", "parallel")),
        name="cast_bf16",
    )(w)


def _cast_transposed_kernel(x_ref, o_ref, *, valid):
    cols = pl.program_id(1) * x_ref.shape[0] + lax.broadcasted_iota(jnp.int32, x_ref.shape, 0)
    o_ref[...] = jnp.where(cols < valid, x_ref[...], 0.0).T.astype(BF16)


def _cast_bf16_transposed(wt, cols):
    depth, c, rows = wt.shape
    bc = CAST_T_COLS
    return pl.pallas_call(
        functools.partial(_cast_transposed_kernel, valid=c),
        grid=(depth, pl.cdiv(cols, bc)),
        in_specs=[pl.BlockSpec((None, bc, rows), lambda l, i: (l, i, 0))],
        out_specs=pl.BlockSpec((None, rows, bc), lambda l, i: (l, 0, i)),
        out_shape=jax.ShapeDtypeStruct((depth, rows, cols), BF16),
        compiler_params=pltpu.CompilerParams(dimension_semantics=("parallel", "parallel")),
        name="cast_bf16_t",
    )(wt)


def _ffn_in(x2d, pool0, conv0, lw, tm, seq_len, pos0):
    t, d = x2d.shape
    d_ff = _arr(lw["wg1"]).shape[-1]
    o_pool, o_qkv, o_z, n_pad = lw["splits"]
    sgu_w, pool_w, gdn_w = lw["sgn"].shape[1], lw["pools"].shape[1], lw["onorm"].shape[1]
    sl = min(seq_len, tm)
    nseq = tm // sl
    tps = seq_len // sl
    row = lambda width: pl.BlockSpec((tm, width), lambda i: (i, 0))
    hist = lambda pad, width: pl.BlockSpec((nseq, pad, width), lambda i: (i // tps, 0, 0))
    res = _res
    weights = [lw["g1"], lw["wg1"], lw["wu1"], lw["wd1"], lw["gmix"], lw["win"],
               lw["sgn"], lw["avg"], lw["poolw"], lw["pools"], lw["convw"]]
    outs = [(d, F32), (n_pad - o_z, F32), (sgu_w, F32), (sgu_w, F32), (pool_w, BF16)]
    dk = gdn_w // GDN_HEADS
    head_rows = pl.BlockSpec((GDN_HEADS, tm, dk), lambda i: (0, i, 0))
    nbatch = t // seq_len
    return pl.pallas_call(
        functools.partial(_ffn_in_kernel, splits=lw["splits"], sl=sl, tps=tps, pos0=pos0),
        grid=(t // tm,),
        in_specs=[row(d), hist(POOL_PAD, pool_w), hist(CONV_PAD, 3 * gdn_w)] + [res(w) for w in weights],
        out_specs=[row(w) for w, _ in outs] + [head_rows] * 3 + [hist(POOL_PAD, pool_w), hist(CONV_PAD, 3 * gdn_w)],
        out_shape=[jax.ShapeDtypeStruct((t, w), dt) for w, dt in outs]
                  + [jax.ShapeDtypeStruct((GDN_HEADS, t, dk), F32)] * 3
                  + [jax.ShapeDtypeStruct((nbatch, POOL_PAD, pool_w), F32),
                     jax.ShapeDtypeStruct((nbatch, CONV_PAD, 3 * gdn_w), F32)],
        scratch_shapes=[pltpu.VMEM((tm, d), BF16), pltpu.VMEM((tm, d_ff), BF16),
                        pltpu.VMEM((tm, o_pool), F32),
                        pltpu.VMEM((nseq, POOL_PAD + sl, pool_w), F32),
                        pltpu.VMEM((nseq, 3 * GDN_HEADS, CONV_PAD + sl, dk), F32)],
        compiler_params=pltpu.CompilerParams(dimension_semantics=("arbitrary",),
                                             vmem_limit_bytes=VMEM_LIMIT_BYTES),
        name="ffn_in",
    )(x2d, pool0, conv0, *(_arr(w) for w in weights))


def _mixer_kernel(ug_ref, vn_ref, yb_ref, q_ref, k_ref, v_ref, zg_ref, h_ref, ssm0_ref,
                  sgw_ref, sgb_ref, alog_ref, dtb_ref, onorm_ref, wout_f32_ref,
                  h2_ref, ssm_ref,
                  u_s, wq_s, ql_s, y_s, s_s, wout_ref, *, ts, c_sgu):
    j = pl.program_id(1)
    nj = pl.num_programs(1)
    ns = vn_ref.shape[0]
    sgu_w = vn_ref.shape[2]
    pool_w = yb_ref.shape[2]
    gdn_w = onorm_ref.shape[1]
    dk = gdn_w // GDN_HEADS
    cc = GDN_CHUNK
    nch = ts // cc
    pre_w = sgu_w + pool_w
    seqs = range(ns)

    @pl.when(jnp.logical_and(pl.program_id(0) == 0, j == 0))
    def _():
        wout_ref[...] = wout_f32_ref[...].astype(BF16)

    @pl.when(j == 0)
    def _():
        s_s[...] = ssm0_ref[...]

    hd = sgu_w // SGU_HEADS
    wi = lax.broadcasted_iota(jnp.int32, (c_sgu, SGU_HEADS * c_sgu), 0)
    wj = lax.broadcasted_iota(jnp.int32, (c_sgu, SGU_HEADS * c_sgu), 1) % c_sgu
    wmask = jnp.where(wi // SGU_BLOCK >= wj // SGU_BLOCK, sgw_ref[...], 0.0).astype(BF16)
    lane_head = lax.broadcasted_iota(jnp.int32, (c_sgu, sgu_w), 1) // hd
    for e in seqs:
        for c in range(ts // c_sgu):
            rows = slice(c * c_sgu, (c + 1) * c_sgu)
            vch = vn_ref[e, rows, :]
            vstack = jnp.concatenate([jnp.where(lane_head == hh, vch, 0.0) for hh in range(SGU_HEADS)],
                                     axis=0).astype(BF16)
            s = _dot(wmask, vstack) + sgb_ref[...]
            y_s[e * ts + c * c_sgu:e * ts + (c + 1) * c_sgu, 0:sgu_w] = (ug_ref[e, rows, :] * s).astype(BF16)
        y_s[e * ts:(e + 1) * ts, sgu_w:pre_w] = yb_ref[e]

    ti = lax.broadcasted_iota(jnp.int32, (2 * ts, ts), 0)
    tj = lax.broadcasted_iota(jnp.int32, (2 * ts, ts), 1)
    tr = jnp.where(ti < ts, ti, ti - ts)
    same_chunk = tr // cc == tj // cc
    summat = jnp.where(same_chunk, jnp.where(ti < ts, jnp.where(tj <= tr, 1.0, 0.0), 1.0), 0.0).astype(BF16)
    beta, gc, egc, erg, etot = [], [], [], [], []
    for e in seqs:
        gates = zg_ref[e, :, gdn_w:gdn_w + LANES]
        beta.append(_sigmoid(gates))
        g = -jnp.exp(alog_ref[...]) * _softplus(gates + dtb_ref[...])
        g_hi = g.astype(BF16)
        g_r = g - g_hi.astype(F32)
        g_mid = g_r.astype(BF16)
        g_lo = (g_r - g_mid.astype(F32)).astype(BF16)
        sums = _dot(summat, g_hi) + _dot(summat, g_mid) + _dot(summat, g_lo)
        gc.append(sums[0:ts])
        tot = sums[ts:2 * ts]
        egc.append(jnp.exp(gc[e]))
        erg.append(jnp.exp(tot - gc[e]))
        etot.append(jnp.exp(tot))

    ri = lax.broadcasted_iota(jnp.int32, (cc, cc), 0)
    ci = lax.broadcasted_iota(jnp.int32, (cc, cc), 1)
    incl = ri >= ci
    strict = ri > ci
    chains = [(e, n, hh) for n in range(nch) for e in seqs for hh in range(GDN_HEADS)]
    gct = {(e, n): gc[e][n * cc:(n + 1) * cc].T for e in seqs for n in range(nch)}

    nmats = []
    for e, n, hh in chains:
        rows = slice(n * cc, (n + 1) * cc)
        hs = slice(hh * dk, (hh + 1) * dk)
        gl = GDN_HEADS + hh
        k = k_ref[hh, e, rows, :]
        kbq = jnp.concatenate([k * beta[e][rows, hh:hh + 1], q_ref[hh, e, rows, :]], axis=0).astype(BF16)
        prod = _dot_nt(kbq, k.astype(BF16))
        decay = jnp.exp(jnp.where(incl, gc[e][rows, gl:gl + 1] - gct[e, n][gl:gl + 1, :], NEG_BIG))
        nmats.append(jnp.where(strict, -(prod[0:cc] * decay), 0.0))
        ql_s[e, n, hh, 0:cc, :] = (prod[cc:2 * cc] * decay).astype(BF16)
    tps = list(nmats)
    pows = list(nmats)
    for _ in range(5):
        pows = [_dot(m.astype(BF16), m.astype(BF16)) for m in pows]
        tps = [t + m + _dot(t.astype(BF16), m.astype(BF16)) for t, m in zip(tps, pows)]
    for (e, n, hh), tp in zip(chains, tps):
        rows = slice(n * cc, (n + 1) * cc)
        hs = slice(hh * dk, (hh + 1) * dk)
        gl = GDN_HEADS + hh
        k = k_ref[hh, e, rows, :]
        b = beta[e][rows, hh:hh + 1]
        eg = egc[e][rows, gl:gl + 1]
        rhs = jnp.concatenate([v_ref[hh, e, rows, :] * b, k * (b * eg)], axis=1)
        sol = rhs + _dot(tp.astype(BF16), rhs.astype(BF16))
        u_s[e, rows, hs] = sol[:, 0:dk]
        wq_s[e, n, hh, 0:cc, :] = sol[:, dk:2 * dk].astype(BF16)
        wq_s[e, n, hh, cc:2 * cc, :] = (q_ref[hh, e, rows, :] * eg).astype(BF16)
        ql_s[e, n, hh, cc:cc + dk, :] = (k * erg[e][rows, gl:gl + 1]).T.astype(BF16)

    fill = []
    for kb in range(pre_w // MXU_COLS):
        for nb in range(h_ref.shape[2] // MXU_COLS):
            fill.append((slice(kb * MXU_COLS, (kb + 1) * MXU_COLS), slice(nb * MXU_COLS, (nb + 1) * MXU_COLS), kb == 0))
    gaps = 2 * nch

    def run_fill(gap):
        for ks, cs, first in fill[gap * len(fill) // gaps:(gap + 1) * len(fill) // gaps]:
            part = _dot(y_s[:, ks], wout_ref[ks, cs])
            for e in seqs:
                base = h_ref[e, :, cs] if first else h2_ref[e, :, cs]
                h2_ref[e, :, cs] = base + part[e * ts:(e + 1) * ts]

    heads = [(e, hh) for e in seqs for hh in range(GDN_HEADS)]
    for n in range(nch):
        rows = slice(n * cc, (n + 1) * cc)
        sts = [s_s[e, hh] for e, hh in heads]
        r1 = [_dot(wq_s[e, n, hh], st.astype(BF16)) for (e, hh), st in zip(heads, sts)]
        run_fill(2 * n)
        vnew = [u_s[e, rows, hh * dk:(hh + 1) * dk] - r[0:cc] for (e, hh), r in zip(heads, r1)]
        r2 = [_dot(ql_s[e, n, hh], vn.astype(BF16)) for (e, hh), vn in zip(heads, vnew)]
        run_fill(2 * n + 1)
        for (e, hh), st, ra, rb in zip(heads, sts, r1, r2):
            gl = GDN_HEADS + hh
            hs = slice(hh * dk, (hh + 1) * dk)
            glcol = etot[e][rows, gl:gl + 1]
            s_s[e, hh] = st * jnp.concatenate([glcol] * (dk // cc), axis=0) + rb[cc:cc + dk]
            o = ra[cc:2 * cc] + rb[0:cc]
            z = zg_ref[e, rows, hs]
            on = o * lax.rsqrt(jnp.mean(o * o, axis=-1, keepdims=True) + EPS) * onorm_ref[:, hs]
            y_s[e * ts + n * cc:e * ts + (n + 1) * cc, pre_w + hh * dk:pre_w + (hh + 1) * dk] = (
                on * (z * _sigmoid(z))).astype(BF16)

    @pl.when(j == nj - 1)
    def _():
        ssm_ref[...] = s_s[...]

    part = _dot(y_s[:, pre_w:], wout_ref[pre_w:, :])
    for e in seqs:
        h2_ref[e] = h2_ref[e] + part[e * ts:(e + 1) * ts]


def _mixer(h, ug, vn, yb, q, k, v, zg, ssm0, lw, ts, c_sgu, ns):
    b, l, d = h.shape
    sgu_w, pool_w, dk = vn.shape[2], yb.shape[2], q.shape[3]
    gdn_w = GDN_HEADS * dk
    nch = ts // GDN_CHUNK
    mix_w = sgu_w + pool_w + gdn_w
    tile = lambda width: pl.BlockSpec((ns, ts, width), lambda i, j: (i, j, 0))
    per_b = lambda *shape: pl.BlockSpec((ns,) + shape, lambda i, j: (i,) + (0,) * len(shape))
    heads_tile = pl.BlockSpec((GDN_HEADS, ns, ts, dk), lambda i, j: (0, i, j, 0))
    res = _res
    weights = [lw["sgw"], lw["sgb"], lw["alog"], lw["dtb"], lw["onorm"], lw["wout"]]
    return pl.pallas_call(
        functools.partial(_mixer_kernel, ts=ts, c_sgu=c_sgu),
        grid=(b // ns, l // ts),
        in_specs=[tile(sgu_w), tile(sgu_w), tile(pool_w), heads_tile, heads_tile, heads_tile,
                  tile(zg.shape[2]), tile(d), per_b(GDN_HEADS, dk, dk)] + [res(w) for w in weights],
        out_specs=[tile(d), per_b(GDN_HEADS, dk, dk)],
        out_shape=[jax.ShapeDtypeStruct((b, l, d), F32), jax.ShapeDtypeStruct((b, GDN_HEADS, dk, dk), F32)],
        scratch_shapes=[
            pltpu.VMEM((ns, ts, gdn_w), F32),
            pltpu.VMEM((ns, nch, GDN_HEADS, 2 * GDN_CHUNK, dk), BF16),
            pltpu.VMEM((ns, nch, GDN_HEADS, GDN_CHUNK + dk, GDN_CHUNK), BF16),
            pltpu.VMEM((ns * ts, mix_w), BF16),
            pltpu.VMEM((ns, GDN_HEADS, dk, dk), F32),
            pltpu.VMEM((mix_w, d), BF16),
        ],
        compiler_params=pltpu.CompilerParams(dimension_semantics=("arbitrary", "arbitrary"),
                                             vmem_limit_bytes=VMEM_LIMIT_BYTES),
        name="mixer",
    )(ug, vn, yb, q, k, v, zg, h, ssm0, *(_arr(w) for w in weights))


def _attn_ffn_kernel(h_ref, mk_ref, mv_ref, gx_ref, wq_f32_ref, wo_f32_ref, g2_ref, wg_ref, wu_ref, wd_ref, gf_ref,
                     out_ref, nb_ref, act_ref, ob_ref, kh_s, vh_s, wq_ref, wo_ref, *, final, sl, tps):
    nseq = h_ref.shape[0] // sl
    heads, dh = mk_ref.shape[2], mk_ref.shape[3]

    @pl.when(pl.program_id(0) == 0)
    def _():
        wq_ref[...] = wq_f32_ref[...].astype(BF16)
        wo_ref[...] = wo_f32_ref[...].astype(BF16)

    @pl.when(pl.program_id(0) % tps == 0)
    def _():
        for s in range(nseq):
            for hh in range(heads):
                kh_s[s, hh] = mk_ref[s, :, hh, :].astype(BF16)
                vh_s[s, hh] = mv_ref[s, :, hh, :].astype(BF16)

    h = h_ref[...]
    nb_ref[...] = _rms(h, gx_ref[...]).astype(BF16)
    q = _dot(nb_ref[...], wq_ref[...])
    scale = dh ** -0.5
    pairs = [(s, hh, slice(s * sl, (s + 1) * sl), slice(hh * dh, (hh + 1) * dh))
             for s in range(nseq) for hh in range(heads)]
    scs = [_dot_nt(q[rows, hs].astype(BF16), kh_s[s, hh]) * scale for s, hh, rows, hs in pairs]
    es = [jnp.exp(sc - jnp.max(sc, axis=-1, keepdims=True)) for sc in scs]
    prs = [e / jnp.sum(e, axis=-1, keepdims=True) for e in es]
    for (s, hh, rows, hs), pr in zip(pairs, prs):
        ob_ref[rows, hs] = _dot(pr.astype(BF16), vh_s[s, hh]).astype(BF16)
    h = h + _dot(ob_ref[...], wo_ref[...])
    nb_ref[...] = _rms(h, g2_ref[...]).astype(BF16)
    h = h + 0.5 * _swiglu(nb_ref, wg_ref, wu_ref, wd_ref, act_ref)
    if final:
        h = _rms(h, gf_ref[...])
    out_ref[...] = h


def _attn_ffn(h2d, mk, mv, l, lw, gf, tm, seq_len, final):
    t, d = h2d.shape
    d_ff = _arr(lw["wg2"]).shape[-1]
    m, heads, dh = mk.shape[2:]
    sl = min(seq_len, tm)
    nseq = tm // sl
    tps = seq_len // sl
    tile = pl.BlockSpec((tm, d), lambda i: (i, 0))
    mem = pl.BlockSpec((None, nseq, m, heads, dh), lambda i: (l, i // tps, 0, 0, 0))
    weights = [lw["gx"], lw["wmq"], lw["wmo"], lw["g2"], lw["wg2"], lw["wu2"], lw["wd2"], gf]
    return pl.pallas_call(
        functools.partial(_attn_ffn_kernel, final=final, sl=sl, tps=tps),
        grid=(t // tm,),
        in_specs=[tile, mem, mem] + [_res(w) for w in weights],
        out_specs=tile,
        out_shape=jax.ShapeDtypeStruct((t, d), F32),
        scratch_shapes=[pltpu.VMEM((tm, d), BF16), pltpu.VMEM((tm, d_ff), BF16), pltpu.VMEM((tm, heads * dh), BF16),
                        pltpu.VMEM((nseq, heads, m, dh), BF16), pltpu.VMEM((nseq, heads, m, dh), BF16),
                        pltpu.VMEM((d, heads * dh), BF16), pltpu.VMEM((heads * dh, d), BF16)],
        compiler_params=pltpu.CompilerParams(dimension_semantics=("arbitrary",),
                                             vmem_limit_bytes=VMEM_LIMIT_BYTES),
        name="attn_ffn",
    )(h2d, mk, mv, *(_arr(w) for w in weights))


def _mem_kv_kernel(mem_ref, g_ref, wk_ref, wv_ref, k_ref, v_ref):
    mb = _rms(mem_ref[...], g_ref[...]).astype(BF16)
    k = _dot(mb, wk_ref[...].astype(BF16))
    v = _dot(mb, wv_ref[...].astype(BF16))
    heads, dh = k_ref.shape[1], k_ref.shape[2]
    for hh in range(heads):
        k_ref[:, hh, :] = k[:, hh * dh:(hh + 1) * dh]
        v_ref[:, hh, :] = v[:, hh * dh:(hh + 1) * dh]


def _mem_kv(mem, g, wk, wv, heads):
    b, m, d = mem.shape
    depth, _, mw = wk.shape
    per_layer = lambda *shape: pl.BlockSpec((None,) + shape, lambda l, i: (l,) + (0,) * len(shape))
    out = pl.BlockSpec((None, None, m, heads, mw // heads), lambda l, i: (l, i, 0, 0, 0))
    return pl.pallas_call(
        _mem_kv_kernel,
        grid=(depth, b),
        in_specs=[pl.BlockSpec((None, m, d), lambda l, i: (i, 0, 0)), per_layer(1, d), per_layer(d, mw),
                  per_layer(d, mw)],
        out_specs=[out, out],
        out_shape=[jax.ShapeDtypeStruct((depth, b, m, heads, mw // heads), F32)] * 2,
        compiler_params=pltpu.CompilerParams(dimension_semantics=("parallel", "parallel")),
        name="mem_kv",
    )(mem, g.reshape(depth, 1, d), wk, wv)


def _layer_weights(l, c_sgu, big, ffn1_norm, mix_norm, sgu_norm, sgu_w, sgu_b, pool_w, pool_scale, gdn_conv_w,
                   gdn_a_log, gdn_dt_bias, gdn_out_norm, xattn_norm, ffn2_norm):
    sgu_width = sgu_norm.shape[1]
    pool_width = pool_scale.shape[1]
    gdn_width = gdn_out_norm.shape[1] * GDN_HEADS
    row = lambda vec: vec.reshape(1, -1).astype(F32)
    groups = len(POOL_WINDOWS)
    gdim = pool_width // groups
    poolw = jnp.zeros((pool_width, pool_width), F32)
    for gi in range(groups):
        poolw = poolw.at[gi * gdim:(gi + 1) * gdim, gi * gdim:(gi + 1) * gdim].set(pool_w[l, gi])
    hd = sgu_width // SGU_HEADS
    seg = jnp.arange(sgu_width) // hd
    avg = jnp.where(seg[:, None] == seg[None, :], 1.0 / hd, 0.0).astype(BF16)
    sgw = jnp.concatenate([sgu_w[l, hh, :c_sgu, :c_sgu] for hh in range(SGU_HEADS)], axis=1)
    sgb = jnp.repeat(sgu_b[l, :, :c_sgu].T, hd, axis=1)
    lane_pad = lambda vec: jnp.pad(vec, (GDN_HEADS, LANES - 2 * GDN_HEADS)).reshape(1, LANES)
    lw = {name: (arr, l) for name, arr in big.items()}
    lw.update(
        splits=_in_splits(sgu_width, pool_width, gdn_width),
        g1=row(ffn1_norm[l]), gmix=row(mix_norm[l]),
        sgw=sgw, sgb=sgb, sgn=row(sgu_norm[l]), avg=avg,
        poolw=poolw.astype(BF16), pools=row(pool_scale[l]),
        convw=gdn_conv_w[l], alog=lane_pad(gdn_a_log[l]), dtb=lane_pad(gdn_dt_bias[l]),
        onorm=row(jnp.tile(gdn_out_norm[l], GDN_HEADS)),
        gx=row(xattn_norm[l]), g2=row(ffn2_norm[l]),
    )
    return lw


def _in_splits(sgu_width, pool_width, gdn_width):
    o_pool = 2 * sgu_width
    o_qkv = o_pool + pool_width
    o_z = o_qkv + 3 * gdn_width
    return o_pool, o_qkv, o_z, o_z + gdn_width + LANES


def _tiles(batch, seq):
    total = batch * seq
    return min(FFN_IN_ROWS, total), min(MIXER_ROWS, seq), min(ATTN_FFN_ROWS, total)


def _layer(x, mk, mv, l, pool0, conv0, ssm0, lw, gf, final, pos0):
    b, seq, d = x.shape
    tm, ts, tm_attn = _tiles(b, seq)
    h, zg, ug, vn, yb, q, k, v, pool_new, conv_new = _ffn_in(x.reshape(b * seq, d), pool0, conv0, lw, tm, seq, pos0)
    shp = lambda arr: arr.reshape(b, seq, arr.shape[-1])
    c_sgu = lw["sgb"].shape[0]
    hshp = lambda arr: arr.reshape(arr.shape[0], b, seq, arr.shape[-1])
    h2, ssm_new = _mixer(shp(h), shp(ug), shp(vn), shp(yb), hshp(q), hshp(k), hshp(v), shp(zg), ssm0, lw, ts, c_sgu,
                         MIXER_SEQS)
    out = _attn_ffn(h2.reshape(b * seq, d), mk, mv, l, lw, gf, tm_attn, seq, final)
    return (shp(out), shp(vn), pool_new[:, POOL_PAD - POOL_HIST:], conv_new[:, CONV_PAD - (GDN_CONV - 1):], ssm_new)


def kernel(x_prompt, x_sample, mem_prompt, cache_mem_k, cache_mem_v, state_pool, state_conv, state_ssm, ffn1_norm, ffn1_w_gate, ffn1_w_up, ffn1_w_down, mix_norm, w_in, sgu_norm, sgu_w, sgu_b, pool_w, pool_scale, gdn_conv_w, gdn_a_log, gdn_dt_bias, gdn_out_norm, w_out, xattn_norm, mem_norm, w_mq, w_mk, w_mv, w_mo, ffn2_norm, ffn2_w_gate, ffn2_w_up, ffn2_w_down, final_norm):
    depth = w_in.shape[0]
    bp, lp, d = x_prompt.shape
    bs, ls, _ = x_sample.shape
    pool_width = state_pool.shape[-1]
    qkv_width = state_conv.shape[-1]
    dk = state_ssm.shape[-1]
    n_pad = _in_splits(sgu_norm.shape[1], pool_width, gdn_out_norm.shape[1] * GDN_HEADS)[3]
    big = dict(wg1=_cast_bf16(ffn1_w_gate), wu1=_cast_bf16(ffn1_w_up), wd1=_cast_bf16(ffn1_w_down),
               win=_cast_bf16_transposed(jnp.swapaxes(w_in, 1, 2), n_pad), wout=w_out, wmq=w_mq, wmo=w_mo,
               wg2=_cast_bf16(ffn2_w_gate), wu2=_cast_bf16(ffn2_w_up), wd2=_cast_bf16(ffn2_w_down))
    small = (ffn1_norm, mix_norm, sgu_norm, sgu_w, sgu_b, pool_w, pool_scale, gdn_conv_w, gdn_a_log, gdn_dt_bias,
             gdn_out_norm, xattn_norm, ffn2_norm)
    gf = final_norm.reshape(1, d)
    prompt_mk, prompt_mv = _mem_kv(mem_prompt, mem_norm, w_mk, w_mv, cache_mem_k.shape[3])
    pool0_p = jnp.zeros((bp, POOL_PAD, pool_width), F32)
    conv0_p = jnp.zeros((bp, CONV_PAD, qkv_width), F32)
    ssm0_p = jnp.zeros((bp, GDN_HEADS, dk, dk), F32)
    xp, xs = x_prompt, x_sample
    outs_p = [[] for _ in range(3)]
    outs_s = [[] for _ in range(4)]
    for l in range(depth):
        final = l == depth - 1
        lw_p = _layer_weights(l, min(128, lp), big, *small)
        lw_s = lw_p if min(128, ls) == min(128, lp) else _layer_weights(l, min(128, ls), big, *small)
        xp, _, pp, pc, ps = _layer(xp, prompt_mk, prompt_mv, l, pool0_p, conv0_p, ssm0_p, lw_p, gf, final, 0)
        pool0_s = jnp.pad(state_pool[l], ((0, 0), (POOL_PAD - POOL_HIST, 0), (0, 0)))
        conv0_s = jnp.pad(state_conv[l], ((0, 0), (CONV_PAD - (GDN_CONV - 1), 0), (0, 0)))
        xs, sv, sp, sc, ss = _layer(xs, cache_mem_k, cache_mem_v, l, pool0_s, conv0_s, state_ssm[l], lw_s, gf, final,
                                    PAST_LEN)
        for lst, val in zip(outs_p, (pp, pc, ps)):
            lst.append(val)
        for lst, val in zip(outs_s, (sp, sc, ss, sv)):
            lst.append(val)
    return (xp, xs, *(jnp.stack(v) for v in outs_p), prompt_mk, prompt_mv, *(jnp.stack(v) for v in outs_s))
```

```python
import functools
import math

import jax
import jax.numpy as jnp
from jax import lax
from jax.experimental import pallas as pl
from jax.experimental.pallas import tpu as pltpu

F32 = jnp.float32
BF16 = jnp.bfloat16

EPS = 1e-6
L2_EPS = 1e-6
LANES = 128
MXU_COLS = 256
VMEM_LIMIT_BYTES = 58 * 1024 * 1024

SGU_HEADS = 4
SGU_BLOCK = 64
POOL_WINDOWS = (2, 4, 8, 16)
POOL_HIST = max(POOL_WINDOWS) - 1
POOL_PAD = 16
GDN_HEADS = 4
GDN_CONV = 4
CONV_PAD = 8
CONV_PHASES = 4
GDN_CHUNK = 64
MEM_HEADS = 4
PAST_LEN = 2048
NEG_BIG = -1e30
MIXER_SEQS = 2
FFN_IN_ROWS = 512
ATTN_FFN_ROWS = 1024
MIXER_ROWS = 256
CAST_BLOCK_BYTES = 3 * 1024 * 1024
CAST_T_COLS = 512


def _dot(a, b):
    return jnp.dot(a, b, preferred_element_type=F32)


def _dot_nt(a, b):
    return lax.dot_general(a, b, (((1,), (1,)), ((), ())), preferred_element_type=F32)


def _rms(x, g):
    return x * lax.rsqrt(jnp.mean(x * x, axis=-1, keepdims=True) + EPS) * g


def _sigmoid(x):
    return 1.0 / (1.0 + jnp.exp(-x))


def _gelu_tanh(x):
    return 0.5 * x * (1.0 + jnp.tanh(math.sqrt(2.0 / math.pi) * (x + 0.044715 * (x * x * x))))


def _softplus(x):
    return jnp.maximum(x, 0.0) + jnp.log1p(jnp.exp(-jnp.abs(x)))


def _swiglu(nb_ref, wg_ref, wu_ref, wd_ref, act_ref):
    d_ff = wg_ref.shape[1]
    for c in range(d_ff // MXU_COLS):
        sl = slice(c * MXU_COLS, (c + 1) * MXU_COLS)
        g = _dot(nb_ref[...], wg_ref[:, sl])
        u = _dot(nb_ref[...], wu_ref[:, sl])
        act_ref[:, sl] = (g * _sigmoid(g) * u).astype(BF16)
    return _dot(act_ref[...], wd_ref[...])


def _ffn_in_kernel(x_ref, pool0_ref, conv0_ref, g1_ref, wg_ref, wu_ref, wd_ref, gm_ref, win_ref,
                   sgn_ref, avg_ref, poolw_ref, pools_ref, convw_ref,
                   h_ref, zg_ref, ug_ref, vn_ref, yb_ref, q_ref, k_ref, v_ref, poolst_ref, convst_ref,
                   nb_ref, act_ref, uv_s, pbuf, cbuf, *, splits, sl, tps, pos0):
    i = pl.program_id(0)
    tm = x_ref.shape[0]
    nseq = tm // sl
    sgu_w = sgn_ref.shape[1]
    pool_w = pools_ref.shape[1]
    dk = q_ref.shape[2]
    gdn_w = GDN_HEADS * dk
    ncb = 3 * GDN_HEADS
    o_pool, o_qkv, o_z, n_pad = splits
    iv = i

    @pl.when(i % tps == 0)
    def _():
        pbuf[:, 0:POOL_PAD, :] = pool0_ref[...]
        for cb in range(ncb):
            cbuf[:, cb, 0:CONV_PAD, :] = conv0_ref[:, :, cb * dk:(cb + 1) * dk]

    def matmul_stage():
        x = x_ref[...]
        nb_ref[...] = _rms(x, g1_ref[...]).astype(BF16)
        h = x + 0.5 * _swiglu(nb_ref, wg_ref, wu_ref, wd_ref, act_ref)
        h_ref[...] = h
        nb_ref[...] = _rms(h, gm_ref[...]).astype(BF16)
        uv_s[...] = _dot(nb_ref[...], win_ref[:, 0:o_pool])
        zg_ref[...] = _dot(nb_ref[...], win_ref[:, o_z:n_pad])
        p = _dot(nb_ref[...], win_ref[:, o_pool:o_qkv])
        qkv = _dot(nb_ref[...], win_ref[:, o_qkv:o_z])
        for s in range(nseq):
            pbuf[s, POOL_PAD:POOL_PAD + sl, :] = p[s * sl:(s + 1) * sl]
            for cb in range(ncb):
                cbuf[s, cb, CONV_PAD:CONV_PAD + sl, :] = qkv[s * sl:(s + 1) * sl, cb * dk:(cb + 1) * dk]

    def vector_stage():
        uv = _gelu_tanh(uv_s[...])
        ug_ref[...] = uv[:, 0:sgu_w]
        v = uv[:, sgu_w:]
        avg = avg_ref[...]

        def seg_mean(t):
            hi = t.astype(BF16)
            lo = (t - hi.astype(F32)).astype(BF16)
            return _dot(hi, avg) + _dot(lo, avg)

        vc = v - seg_mean(v)
        vn_ref[...] = vc * lax.rsqrt(seg_mean(vc * vc) + EPS) * sgn_ref[...]

        gdim = pool_w // len(POOL_WINDOWS)
        lane = lax.broadcasted_iota(jnp.int32, (sl, LANES), 1)
        pos = pos0 + (iv % tps) * sl + lax.broadcasted_iota(jnp.int32, (sl, LANES), 0)
        dparts = []
        for s in range(nseq):
            dblk = []
            for blk in range(pool_w // LANES):
                ls = slice(blk * LANES, (blk + 1) * LANES)
                wins = [POOL_WINDOWS[(blk * LANES + l0) // gdim] for l0 in range(0, LANES, gdim)]
                wl = jnp.full((sl, LANES), wins[-1], jnp.int32)
                for gi in range(len(wins) - 2, -1, -1):
                    wl = jnp.where(lane < (gi + 1) * gdim, wins[gi], wl)
                pb = pbuf[s, POOL_PAD:POOL_PAD + sl, ls]
                acc = pb
                prev = 1
                for w in sorted(set(wins)):
                    part = None
                    for jj in range(prev, w):
                        sh = pbuf[s, POOL_PAD - jj:POOL_PAD - jj + sl, ls]
                        part = sh if part is None else part + sh
                    if part is not None:
                        acc = acc + (part if w == min(wins) else jnp.where(wl >= w, part, 0.0))
                    prev = w
                cnt = jnp.minimum(wl, pos + 1).astype(F32)
                dblk.append(acc / cnt - pb)
            dparts.append(jnp.concatenate(dblk, axis=1))
            tail = pbuf[s, sl:sl + POOL_PAD, :]
            poolst_ref[s] = tail
            pbuf[s, 0:POOL_PAD, :] = tail
        dlt = (dparts[0] if nseq == 1 else jnp.concatenate(dparts, axis=0)).astype(BF16)
        yb_ref[...] = (_dot(dlt, poolw_ref[...]) * pools_ref[...]).astype(BF16)

        qscale = dk ** -0.5
        nrow = sl // CONV_PHASES
        for cb in range(ncb):
            cs = slice(cb * dk, (cb + 1) * dk)
            which, hh = divmod(cb, GDN_HEADS)
            for s in range(nseq):
                for r in range(CONV_PHASES):
                    y = None
                    for t in range(GDN_CONV):
                        tap = (cbuf[s, cb, pl.ds(CONV_PAD + r - t, nrow, stride=CONV_PHASES), :]
                               * convw_ref[GDN_CONV - 1 - t:GDN_CONV - t, cs])
                        y = tap if y is None else y + tap
                    y = y * _sigmoid(y)
                    out_rows = pl.ds(s * sl + r, nrow, stride=CONV_PHASES)
                    if which == 0:
                        q_ref[hh, out_rows, :] = y * (lax.rsqrt(jnp.sum(y * y, axis=-1, keepdims=True) + L2_EPS)
                                                      * qscale)
                    elif which == 1:
                        k_ref[hh, out_rows, :] = y * lax.rsqrt(jnp.sum(y * y, axis=-1, keepdims=True) + L2_EPS)
                    else:
                        v_ref[hh, out_rows, :] = y
        for s in range(nseq):
            for cb in range(ncb):
                ctail = cbuf[s, cb, sl:sl + CONV_PAD, :]
                convst_ref[s, :, cb * dk:(cb + 1) * dk] = ctail
                cbuf[s, cb, 0:CONV_PAD, :] = ctail

    matmul_stage()
    vector_stage()


def _res(w):
    if isinstance(w, tuple):
        arr, l = w
        nd = arr.ndim - 1
        return pl.BlockSpec((None,) + arr.shape[1:], lambda *_: (l,) + (0,) * nd, pipeline_mode=pl.Buffered(1))
    nd = w.ndim
    return pl.BlockSpec(w.shape, lambda *_: (0,) * nd, pipeline_mode=pl.Buffered(1))


def _arr(w):
    return w[0] if isinstance(w, tuple) else w


def _cast_kernel(x_ref, o_ref):
    o_ref[...] = x_ref[...].astype(BF16)


def _cast_bf16(w):
    depth, rows, c = w.shape
    br = max(r for r in range(8, rows + 1, 8) if rows % r == 0 and (r * c * 4 <= CAST_BLOCK_BYTES or r == 8))
    blk = pl.BlockSpec((None, br, c), lambda l, i: (l, i, 0))
    return pl.pallas_call(
        _cast_kernel,
        grid=(depth, rows // br),
        in_specs=[blk],
        out_specs=blk,
        out_shape=jax.ShapeDtypeStruct((depth, rows, c), BF16),
        compiler_params=pltpu.CompilerParams(dimension_semantics=("parallel", "parallel")),
        name="cast_bf16",
    )(w)


def _cast_transposed_kernel(x_ref, o_ref, *, valid):
    cols = pl.program_id(1) * x_ref.shape[0] + lax.broadcasted_iota(jnp.int32, x_ref.shape, 0)
    o_ref[...] = jnp.where(cols < valid, x_ref[...], 0.0).T.astype(BF16)


def _cast_bf16_transposed(wt, cols):
    depth, c, rows = wt.shape
    bc = CAST_T_COLS
    return pl.pallas_call(
        functools.partial(_cast_transposed_kernel, valid=c),
        grid=(depth, pl.cdiv(cols, bc)),
        in_specs=[pl.BlockSpec((None, bc, rows), lambda l, i: (l, i, 0))],
        out_specs=pl.BlockSpec((None, rows, bc), lambda l, i: (l, 0, i)),
        out_shape=jax.ShapeDtypeStruct((depth, rows, cols), BF16),
        compiler_params=pltpu.CompilerParams(dimension_semantics=("parallel", "parallel")),
        name="cast_bf16_t",
    )(wt)


def _ffn_in(x2d, pool0, conv0, lw, tm, seq_len, pos0):
    t, d = x2d.shape
    d_ff = _arr(lw["wg1"]).shape[-1]
    o_pool, o_qkv, o_z, n_pad = lw["splits"]
    sgu_w, pool_w, gdn_w = lw["sgn"].shape[1], lw["pools"].shape[1], lw["onorm"].shape[1]
    sl = min(seq_len, tm)
    nseq = tm // sl
    tps = seq_len // sl
    row = lambda width: pl.BlockSpec((tm, width), lambda i: (i, 0))
    hist = lambda pad, width: pl.BlockSpec((nseq, pad, width), lambda i: (i // tps, 0, 0))
    res = _res
    weights = [lw["g1"], lw["wg1"], lw["wu1"], lw["wd1"], lw["gmix"], lw["win"],
               lw["sgn"], lw["avg"], lw["poolw"], lw["pools"], lw["convw"]]
    outs = [(d, F32), (n_pad - o_z, F32), (sgu_w, F32), (sgu_w, F32), (pool_w, BF16)]
    dk = gdn_w // GDN_HEADS
    head_rows = pl.BlockSpec((GDN_HEADS, tm, dk), lambda i: (0, i, 0))
    nbatch = t // seq_len
    return pl.pallas_call(
        functools.partial(_ffn_in_kernel, splits=lw["splits"], sl=sl, tps=tps, pos0=pos0),
        grid=(t // tm,),
        in_specs=[row(d), hist(POOL_PAD, pool_w), hist(CONV_PAD, 3 * gdn_w)] + [res(w) for w in weights],
        out_specs=[row(w) for w, _ in outs] + [head_rows] * 3 + [hist(POOL_PAD, pool_w), hist(CONV_PAD, 3 * gdn_w)],
        out_shape=[jax.ShapeDtypeStruct((t, w), dt) for w, dt in outs]
                  + [jax.ShapeDtypeStruct((GDN_HEADS, t, dk), F32)] * 3
                  + [jax.ShapeDtypeStruct((nbatch, POOL_PAD, pool_w), F32),
                     jax.ShapeDtypeStruct((nbatch, CONV_PAD, 3 * gdn_w), F32)],
        scratch_shapes=[pltpu.VMEM((tm, d), BF16), pltpu.VMEM((tm, d_ff), BF16),
                        pltpu.VMEM((tm, o_pool), F32),
                        pltpu.VMEM((nseq, POOL_PAD + sl, pool_w), F32),
                        pltpu.VMEM((nseq, 3 * GDN_HEADS, CONV_PAD + sl, dk), F32)],
        compiler_params=pltpu.CompilerParams(dimension_semantics=("arbitrary",),
                                             vmem_limit_bytes=VMEM_LIMIT_BYTES),
        name="ffn_in",
    )(x2d, pool0, conv0, *(_arr(w) for w in weights))


def _mixer_kernel(ug_ref, vn_ref, yb_ref, q_ref, k_ref, v_ref, zg_ref, h_ref, ssm0_ref,
                  sgw_ref, sgb_ref, alog_ref, dtb_ref, onorm_ref, wout_f32_ref,
                  h2_ref, ssm_ref,
                  u_s, wq_s, ql_s, y_s, s_s, wout_ref, *, ts, c_sgu):
    j = pl.program_id(1)
    nj = pl.num_programs(1)
    ns = vn_ref.shape[0]
    sgu_w = vn_ref.shape[2]
    pool_w = yb_ref.shape[2]
    gdn_w = onorm_ref.shape[1]
    dk = gdn_w // GDN_HEADS
    cc = GDN_CHUNK
    nch = ts // cc
    pre_w = sgu_w + pool_w
    seqs = range(ns)

    @pl.when(jnp.logical_and(pl.program_id(0) == 0, j == 0))
    def _():
        wout_ref[...] = wout_f32_ref[...].astype(BF16)

    @pl.when(j == 0)
    def _():
        s_s[...] = ssm0_ref[...]

    hd = sgu_w // SGU_HEADS
    wi = lax.broadcasted_iota(jnp.int32, (c_sgu, SGU_HEADS * c_sgu), 0)
    wj = lax.broadcasted_iota(jnp.int32, (c_sgu, SGU_HEADS * c_sgu), 1) % c_sgu
    wmask = jnp.where(wi // SGU_BLOCK >= wj // SGU_BLOCK, sgw_ref[...], 0.0).astype(BF16)
    lane_head = lax.broadcasted_iota(jnp.int32, (c_sgu, sgu_w), 1) // hd
    for e in seqs:
        for c in range(ts // c_sgu):
            rows = slice(c * c_sgu, (c + 1) * c_sgu)
            vch = vn_ref[e, rows, :]
            vstack = jnp.concatenate([jnp.where(lane_head == hh, vch, 0.0) for hh in range(SGU_HEADS)],
                                     axis=0).astype(BF16)
            s = _dot(wmask, vstack) + sgb_ref[...]
            y_s[e * ts + c * c_sgu:e * ts + (c + 1) * c_sgu, 0:sgu_w] = (ug_ref[e, rows, :] * s).astype(BF16)
        y_s[e * ts:(e + 1) * ts, sgu_w:pre_w] = yb_ref[e]

    ti = lax.broadcasted_iota(jnp.int32, (2 * ts, ts), 0)
    tj = lax.broadcasted_iota(jnp.int32, (2 * ts, ts), 1)
    tr = jnp.where(ti < ts, ti, ti - ts)
    same_chunk = tr // cc == tj // cc
    summat = jnp.where(same_chunk, jnp.where(ti < ts, jnp.where(tj <= tr, 1.0, 0.0), 1.0), 0.0).astype(BF16)
    beta, gc, egc, erg, etot = [], [], [], [], []
    for e in seqs:
        gates = zg_ref[e, :, gdn_w:gdn_w + LANES]
        beta.append(_sigmoid(gates))
        g = -jnp.exp(alog_ref[...]) * _softplus(gates + dtb_ref[...])
        g_hi = g.astype(BF16)
        g_r = g - g_hi.astype(F32)
        g_mid = g_r.astype(BF16)
        g_lo = (g_r - g_mid.astype(F32)).astype(BF16)
        sums = _dot(summat, g_hi) + _dot(summat, g_mid) + _dot(summat, g_lo)
        gc.append(sums[0:ts])
        tot = sums[ts:2 * ts]
        egc.append(jnp.exp(gc[e]))
        erg.append(jnp.exp(tot - gc[e]))
        etot.append(jnp.exp(tot))

    ri = lax.broadcasted_iota(jnp.int32, (cc, cc), 0)
    ci = lax.broadcasted_iota(jnp.int32, (cc, cc), 1)
    incl = ri >= ci
    strict = ri > ci
    chains = [(e, n, hh) for n in range(nch) for e in seqs for hh in range(GDN_HEADS)]
    gct = {(e, n): gc[e][n * cc:(n + 1) * cc].T for e in seqs for n in range(nch)}

    nmats = []
    for e, n, hh in chains:
        rows = slice(n * cc, (n + 1) * cc)
        hs = slice(hh * dk, (hh + 1) * dk)
        gl = GDN_HEADS + hh
        k = k_ref[hh, e, rows, :]
        kbq = jnp.concatenate([k * beta[e][rows, hh:hh + 1], q_ref[hh, e, rows, :]], axis=0).astype(BF16)
        prod = _dot_nt(kbq, k.astype(BF16))
        decay = jnp.exp(jnp.where(incl, gc[e][rows, gl:gl + 1] - gct[e, n][gl:gl + 1, :], NEG_BIG))
        nmats.append(jnp.where(strict, -(prod[0:cc] * decay), 0.0))
        ql_s[e, n, hh, 0:cc, :] = (prod[cc:2 * cc] * decay).astype(BF16)
    same_block = lambda s: ri // s == ci // s
    eye = jnp.where(ri == ci, 1.0, 0.0)
    base = 8
    n1 = [jnp.where(same_block(base), m, 0.0) for m in nmats]
    n2 = [_dot(m.astype(BF16), m.astype(BF16)) for m in n1]
    tps = [a + b + _dot(a.astype(BF16), b.astype(BF16)) for a, b in zip(n1, n2)]
    n4 = [_dot(m.astype(BF16), m.astype(BF16)) for m in n2]
    dinv = [t + b + _dot(t.astype(BF16), b.astype(BF16)) + eye for t, b in zip(tps, n4)]
    s = base
    while s < cc:
        lower_left = jnp.logical_and(same_block(2 * s), jnp.logical_not(same_block(s)))
        cd = [_dot(jnp.where(lower_left, m, 0.0).astype(BF16), d.astype(BF16)) for m, d in zip(nmats, dinv)]
        dinv = [d + _dot(d.astype(BF16), c.astype(BF16)) for d, c in zip(dinv, cd)]
        s *= 2
    tps = [d - eye for d in dinv]
    for (e, n, hh), tp in zip(chains, tps):
        rows = slice(n * cc, (n + 1) * cc)
        hs = slice(hh * dk, (hh + 1) * dk)
        gl = GDN_HEADS + hh
        k = k_ref[hh, e, rows, :]
        b = beta[e][rows, hh:hh + 1]
        eg = egc[e][rows, gl:gl + 1]
        rhs = jnp.concatenate([v_ref[hh, e, rows, :] * b, k * (b * eg)], axis=1)
        sol = rhs + _dot(tp.astype(BF16), rhs.astype(BF16))
        u_s[e, rows, hs] = sol[:, 0:dk]
        wq_s[e, n, hh, 0:cc, :] = sol[:, dk:2 * dk].astype(BF16)
        wq_s[e, n, hh, cc:2 * cc, :] = (q_ref[hh, e, rows, :] * eg).astype(BF16)
        ql_s[e, n, hh, cc:cc + dk, :] = (k * erg[e][rows, gl:gl + 1]).T.astype(BF16)

    fill = []
    for kb in range(pre_w // MXU_COLS):
        for nb in range(h_ref.shape[2] // MXU_COLS):
            fill.append((slice(kb * MXU_COLS, (kb + 1) * MXU_COLS), slice(nb * MXU_COLS, (nb + 1) * MXU_COLS), kb == 0))
    gaps = 2 * nch

    def run_fill(gap):
        for ks, cs, first in fill[gap * len(fill) // gaps:(gap + 1) * len(fill) // gaps]:
            part = _dot(y_s[:, ks], wout_ref[ks, cs])
            for e in seqs:
                base = h_ref[e, :, cs] if first else h2_ref[e, :, cs]
                h2_ref[e, :, cs] = base + part[e * ts:(e + 1) * ts]

    heads = [(e, hh) for e in seqs for hh in range(GDN_HEADS)]
    for n in range(nch):
        rows = slice(n * cc, (n + 1) * cc)
        sts = [s_s[e, hh] for e, hh in heads]
        r1 = [_dot(wq_s[e, n, hh], st.astype(BF16)) for (e, hh), st in zip(heads, sts)]
        run_fill(2 * n)
        vnew = [u_s[e, rows, hh * dk:(hh + 1) * dk] - r[0:cc] for (e, hh), r in zip(heads, r1)]
        r2 = [_dot(ql_s[e, n, hh], vn.astype(BF16)) for (e, hh), vn in zip(heads, vnew)]
        run_fill(2 * n + 1)
        for (e, hh), st, ra, rb in zip(heads, sts, r1, r2):
            gl = GDN_HEADS + hh
            hs = slice(hh * dk, (hh + 1) * dk)
            glcol = etot[e][rows, gl:gl + 1]
            s_s[e, hh] = st * jnp.concatenate([glcol] * (dk // cc), axis=0) + rb[cc:cc + dk]
            o = ra[cc:2 * cc] + rb[0:cc]
            z = zg_ref[e, rows, hs]
            on = o * lax.rsqrt(jnp.mean(o * o, axis=-1, keepdims=True) + EPS) * onorm_ref[:, hs]
            y_s[e * ts + n * cc:e * ts + (n + 1) * cc, pre_w + hh * dk:pre_w + (hh + 1) * dk] = (
                on * (z * _sigmoid(z))).astype(BF16)

    @pl.when(j == nj - 1)
    def _():
        ssm_ref[...] = s_s[...]

    part = _dot(y_s[:, pre_w:], wout_ref[pre_w:, :])
    for e in seqs:
        h2_ref[e] = h2_ref[e] + part[e * ts:(e + 1) * ts]


def _mixer(h, ug, vn, yb, q, k, v, zg, ssm0, lw, ts, c_sgu, ns):
    b, l, d = h.shape
    sgu_w, pool_w, dk = vn.shape[2], yb.shape[2], q.shape[3]
    gdn_w = GDN_HEADS * dk
    nch = ts // GDN_CHUNK
    mix_w = sgu_w + pool_w + gdn_w
    tile = lambda width: pl.BlockSpec((ns, ts, width), lambda i, j: (i, j, 0))
    per_b = lambda *shape: pl.BlockSpec((ns,) + shape, lambda i, j: (i,) + (0,) * len(shape))
    heads_tile = pl.BlockSpec((GDN_HEADS, ns, ts, dk), lambda i, j: (0, i, j, 0))
    res = _res
    weights = [lw["sgw"], lw["sgb"], lw["alog"], lw["dtb"], lw["onorm"], lw["wout"]]
    return pl.pallas_call(
        functools.partial(_mixer_kernel, ts=ts, c_sgu=c_sgu),
        grid=(b // ns, l // ts),
        in_specs=[tile(sgu_w), tile(sgu_w), tile(pool_w), heads_tile, heads_tile, heads_tile,
                  tile(zg.shape[2]), tile(d), per_b(GDN_HEADS, dk, dk)] + [res(w) for w in weights],
        out_specs=[tile(d), per_b(GDN_HEADS, dk, dk)],
        out_shape=[jax.ShapeDtypeStruct((b, l, d), F32), jax.ShapeDtypeStruct((b, GDN_HEADS, dk, dk), F32)],
        scratch_shapes=[
            pltpu.VMEM((ns, ts, gdn_w), F32),
            pltpu.VMEM((ns, nch, GDN_HEADS, 2 * GDN_CHUNK, dk), BF16),
            pltpu.VMEM((ns, nch, GDN_HEADS, GDN_CHUNK + dk, GDN_CHUNK), BF16),
            pltpu.VMEM((ns * ts, mix_w), BF16),
            pltpu.VMEM((ns, GDN_HEADS, dk, dk), F32),
            pltpu.VMEM((mix_w, d), BF16),
        ],
        compiler_params=pltpu.CompilerParams(dimension_semantics=("arbitrary", "arbitrary"),
                                             vmem_limit_bytes=VMEM_LIMIT_BYTES),
        name="mixer",
    )(ug, vn, yb, q, k, v, zg, h, ssm0, *(_arr(w) for w in weights))


def _attn_ffn_kernel(h_ref, mk_ref, mv_ref, gx_ref, wq_f32_ref, wo_f32_ref, g2_ref, wg_ref, wu_ref, wd_ref, gf_ref,
                     out_ref, nb_ref, act_ref, ob_ref, kh_s, vh_s, wq_ref, wo_ref, *, final, sl, tps):
    nseq = h_ref.shape[0] // sl
    heads, dh = mk_ref.shape[2], mk_ref.shape[3]

    @pl.when(pl.program_id(0) == 0)
    def _():
        wq_ref[...] = wq_f32_ref[...].astype(BF16)
        wo_ref[...] = wo_f32_ref[...].astype(BF16)

    @pl.when(pl.program_id(0) % tps == 0)
    def _():
        for s in range(nseq):
            for hh in range(heads):
                kh_s[s, hh] = mk_ref[s, :, hh, :].astype(BF16)
                vh_s[s, hh] = mv_ref[s, :, hh, :].astype(BF16)

    h = h_ref[...]
    nb_ref[...] = _rms(h, gx_ref[...]).astype(BF16)
    q = _dot(nb_ref[...], wq_ref[...])
    scale = dh ** -0.5
    pairs = [(s, hh, slice(s * sl, (s + 1) * sl), slice(hh * dh, (hh + 1) * dh))
             for s in range(nseq) for hh in range(heads)]
    scs = [_dot_nt(q[rows, hs].astype(BF16), kh_s[s, hh]) * scale for s, hh, rows, hs in pairs]
    es = [jnp.exp(sc - jnp.max(sc, axis=-1, keepdims=True)) for sc in scs]
    prs = [e / jnp.sum(e, axis=-1, keepdims=True) for e in es]
    for (s, hh, rows, hs), pr in zip(pairs, prs):
        ob_ref[rows, hs] = _dot(pr.astype(BF16), vh_s[s, hh]).astype(BF16)
    h = h + _dot(ob_ref[...], wo_ref[...])
    nb_ref[...] = _rms(h, g2_ref[...]).astype(BF16)
    h = h + 0.5 * _swiglu(nb_ref, wg_ref, wu_ref, wd_ref, act_ref)
    if final:
        h = _rms(h, gf_ref[...])
    out_ref[...] = h


def _attn_ffn(h2d, mk, mv, l, lw, gf, tm, seq_len, final):
    t, d = h2d.shape
    d_ff = _arr(lw["wg2"]).shape[-1]
    m, heads, dh = mk.shape[2:]
    sl = min(seq_len, tm)
    nseq = tm // sl
    tps = seq_len // sl
    tile = pl.BlockSpec((tm, d), lambda i: (i, 0))
    mem = pl.BlockSpec((None, nseq, m, heads, dh), lambda i: (l, i // tps, 0, 0, 0))
    weights = [lw["gx"], lw["wmq"], lw["wmo"], lw["g2"], lw["wg2"], lw["wu2"], lw["wd2"], gf]
    return pl.pallas_call(
        functools.partial(_attn_ffn_kernel, final=final, sl=sl, tps=tps),
        grid=(t // tm,),
        in_specs=[tile, mem, mem] + [_res(w) for w in weights],
        out_specs=tile,
        out_shape=jax.ShapeDtypeStruct((t, d), F32),
        scratch_shapes=[pltpu.VMEM((tm, d), BF16), pltpu.VMEM((tm, d_ff), BF16), pltpu.VMEM((tm, heads * dh), BF16),
                        pltpu.VMEM((nseq, heads, m, dh), BF16), pltpu.VMEM((nseq, heads, m, dh), BF16),
                        pltpu.VMEM((d, heads * dh), BF16), pltpu.VMEM((heads * dh, d), BF16)],
        compiler_params=pltpu.CompilerParams(dimension_semantics=("arbitrary",),
                                             vmem_limit_bytes=VMEM_LIMIT_BYTES),
        name="attn_ffn",
    )(h2d, mk, mv, *(_arr(w) for w in weights))


def _mem_kv_kernel(mem_ref, g_ref, wk_ref, wv_ref, k_ref, v_ref):
    mb = _rms(mem_ref[...], g_ref[...]).astype(BF16)
    k = _dot(mb, wk_ref[...].astype(BF16))
    v = _dot(mb, wv_ref[...].astype(BF16))
    heads, dh = k_ref.shape[1], k_ref.shape[2]
    for hh in range(heads):
        k_ref[:, hh, :] = k[:, hh * dh:(hh + 1) * dh]
        v_ref[:, hh, :] = v[:, hh * dh:(hh + 1) * dh]


def _mem_kv(mem, g, wk, wv, heads):
    b, m, d = mem.shape
    depth, _, mw = wk.shape
    per_layer = lambda *shape: pl.BlockSpec((None,) + shape, lambda l, i: (l,) + (0,) * len(shape))
    out = pl.BlockSpec((None, None, m, heads, mw // heads), lambda l, i: (l, i, 0, 0, 0))
    return pl.pallas_call(
        _mem_kv_kernel,
        grid=(depth, b),
        in_specs=[pl.BlockSpec((None, m, d), lambda l, i: (i, 0, 0)), per_layer(1, d), per_layer(d, mw),
                  per_layer(d, mw)],
        out_specs=[out, out],
        out_shape=[jax.ShapeDtypeStruct((depth, b, m, heads, mw // heads), F32)] * 2,
        compiler_params=pltpu.CompilerParams(dimension_semantics=("parallel", "parallel")),
        name="mem_kv",
    )(mem, g.reshape(depth, 1, d), wk, wv)


def _layer_weights(l, c_sgu, big, ffn1_norm, mix_norm, sgu_norm, sgu_w, sgu_b, pool_w, pool_scale, gdn_conv_w,
                   gdn_a_log, gdn_dt_bias, gdn_out_norm, xattn_norm, ffn2_norm):
    sgu_width = sgu_norm.shape[1]
    pool_width = pool_scale.shape[1]
    gdn_width = gdn_out_norm.shape[1] * GDN_HEADS
    row = lambda vec: vec.reshape(1, -1).astype(F32)
    groups = len(POOL_WINDOWS)
    gdim = pool_width // groups
    poolw = jnp.zeros((pool_width, pool_width), F32)
    for gi in range(groups):
        poolw = poolw.at[gi * gdim:(gi + 1) * gdim, gi * gdim:(gi + 1) * gdim].set(pool_w[l, gi])
    hd = sgu_width // SGU_HEADS
    seg = jnp.arange(sgu_width) // hd
    avg = jnp.where(seg[:, None] == seg[None, :], 1.0 / hd, 0.0).astype(BF16)
    sgw = jnp.concatenate([sgu_w[l, hh, :c_sgu, :c_sgu] for hh in range(SGU_HEADS)], axis=1)
    sgb = jnp.repeat(sgu_b[l, :, :c_sgu].T, hd, axis=1)
    lane_pad = lambda vec: jnp.pad(vec, (GDN_HEADS, LANES - 2 * GDN_HEADS)).reshape(1, LANES)
    lw = {name: (arr, l) for name, arr in big.items()}
    lw.update(
        splits=_in_splits(sgu_width, pool_width, gdn_width),
        g1=row(ffn1_norm[l]), gmix=row(mix_norm[l]),
        sgw=sgw, sgb=sgb, sgn=row(sgu_norm[l]), avg=avg,
        poolw=poolw.astype(BF16), pools=row(pool_scale[l]),
        convw=gdn_conv_w[l], alog=lane_pad(gdn_a_log[l]), dtb=lane_pad(gdn_dt_bias[l]),
        onorm=row(jnp.tile(gdn_out_norm[l], GDN_HEADS)),
        gx=row(xattn_norm[l]), g2=row(ffn2_norm[l]),
    )
    return lw


def _in_splits(sgu_width, pool_width, gdn_width):
    o_pool = 2 * sgu_width
    o_qkv = o_pool + pool_width
    o_z = o_qkv + 3 * gdn_width
    return o_pool, o_qkv, o_z, o_z + gdn_width + LANES


def _tiles(batch, seq):
    total = batch * seq
    return min(FFN_IN_ROWS, total), min(MIXER_ROWS, seq), min(ATTN_FFN_ROWS, total)


def _layer(x, mk, mv, l, pool0, conv0, ssm0, lw, gf, final, pos0):
    b, seq, d = x.shape
    tm, ts, tm_attn = _tiles(b, seq)
    h, zg, ug, vn, yb, q, k, v, pool_new, conv_new = _ffn_in(x.reshape(b * seq, d), pool0, conv0, lw, tm, seq, pos0)
    shp = lambda arr: arr.reshape(b, seq, arr.shape[-1])
    c_sgu = lw["sgb"].shape[0]
    hshp = lambda arr: arr.reshape(arr.shape[0], b, seq, arr.shape[-1])
    h2, ssm_new = _mixer(shp(h), shp(ug), shp(vn), shp(yb), hshp(q), hshp(k), hshp(v), shp(zg), ssm0, lw, ts, c_sgu,
                         MIXER_SEQS)
    out = _attn_ffn(h2.reshape(b * seq, d), mk, mv, l, lw, gf, tm_attn, seq, final)
    return (shp(out), shp(vn), pool_new[:, POOL_PAD - POOL_HIST:], conv_new[:, CONV_PAD - (GDN_CONV - 1):], ssm_new)


def kernel(x_prompt, x_sample, mem_prompt, cache_mem_k, cache_mem_v, state_pool, state_conv, state_ssm, ffn1_norm, ffn1_w_gate, ffn1_w_up, ffn1_w_down, mix_norm, w_in, sgu_norm, sgu_w, sgu_b, pool_w, pool_scale, gdn_conv_w, gdn_a_log, gdn_dt_bias, gdn_out_norm, w_out, xattn_norm, mem_norm, w_mq, w_mk, w_mv, w_mo, ffn2_norm, ffn2_w_gate, ffn2_w_up, ffn2_w_down, final_norm):
    depth = w_in.shape[0]
    bp, lp, d = x_prompt.shape
    bs, ls, _ = x_sample.shape
    pool_width = state_pool.shape[-1]
    qkv_width = state_conv.shape[-1]
    dk = state_ssm.shape[-1]
    n_pad = _in_splits(sgu_norm.shape[1], pool_width, gdn_out_norm.shape[1] * GDN_HEADS)[3]
    big = dict(wg1=_cast_bf16(ffn1_w_gate), wu1=_cast_bf16(ffn1_w_up), wd1=_cast_bf16(ffn1_w_down),
               win=_cast_bf16_transposed(jnp.swapaxes(w_in, 1, 2), n_pad), wout=w_out, wmq=w_mq, wmo=w_mo,
               wg2=_cast_bf16(ffn2_w_gate), wu2=_cast_bf16(ffn2_w_up), wd2=_cast_bf16(ffn2_w_down))
    small = (ffn1_norm, mix_norm, sgu_norm, sgu_w, sgu_b, pool_w, pool_scale, gdn_conv_w, gdn_a_log, gdn_dt_bias,
             gdn_out_norm, xattn_norm, ffn2_norm)
    gf = final_norm.reshape(1, d)
    prompt_mk, prompt_mv = _mem_kv(mem_prompt, mem_norm, w_mk, w_mv, cache_mem_k.shape[3])
    pool0_p = jnp.zeros((bp, POOL_PAD, pool_width), F32)
    conv0_p = jnp.zeros((bp, CONV_PAD, qkv_width), F32)
    ssm0_p = jnp.zeros((bp, GDN_HEADS, dk, dk), F32)
    xp, xs = x_prompt, x_sample
    outs_p = [[] for _ in range(3)]
    outs_s = [[] for _ in range(4)]
    for l in range(depth):
        final = l == depth - 1
        lw_p = _layer_weights(l, min(128, lp), big, *small)
        lw_s = lw_p if min(128, ls) == min(128, lp) else _layer_weights(l, min(128, ls), big, *small)
        xp, _, pp, pc, ps = _layer(xp, prompt_mk, prompt_mv, l, pool0_p, conv0_p, ssm0_p, lw_p, gf, final, 0)
        pool0_s = jnp.pad(state_pool[l], ((0, 0), (POOL_PAD - POOL_HIST, 0), (0, 0)))
        conv0_s = jnp.pad(state_conv[l], ((0, 0), (CONV_PAD - (GDN_CONV - 1), 0), (0, 0)))
        xs, sv, sp, sc, ss = _layer(xs, cache_mem_k, cache_mem_v, l, pool0_s, conv0_s, state_ssm[l], lw_s, gf, final,
                                    PAST_LEN)
        for lst, val in zip(outs_p, (pp, pc, ps)):
            lst.append(val)
        for lst, val in zip(outs_s, (sp, sc, ss, sv)):
            lst.append(val)
    return (xp, xs, *(jnp.stack(v) for v in outs_p), prompt_mk, prompt_mv, *(jnp.stack(v) for v in outs_s))
```

```python
import functools
import math

import jax
import jax.numpy as jnp
from jax import lax
from jax.experimental import pallas as pl
from jax.experimental.pallas import tpu as pltpu

F32 = jnp.float32
BF16 = jnp.bfloat16

EPS = 1e-6
L2_EPS = 1e-6
LANES = 128
MXU_COLS = 256
VMEM_LIMIT_BYTES = 58 * 1024 * 1024

SGU_HEADS = 4
SGU_BLOCK = 64
POOL_WINDOWS = (2, 4, 8, 16)
POOL_HIST = max(POOL_WINDOWS) - 1
POOL_PAD = 16
GDN_HEADS = 4
GDN_CONV = 4
CONV_PAD = 8
CONV_PHASES = 4
GDN_CHUNK = 64
MEM_HEADS = 4
PAST_LEN = 2048
NEG_BIG = -1e30
MIXER_SEQS = 2
FFN_IN_ROWS = 512
ATTN_FFN_ROWS = 1024
MIXER_ROWS = 256
CAST_BLOCK_BYTES = 3 * 1024 * 1024
CAST_T_COLS = 512


def _dot(a, b):
    return jnp.dot(a, b, preferred_element_type=F32)


def _dot_nt(a, b):
    return lax.dot_general(a, b, (((1,), (1,)), ((), ())), preferred_element_type=F32)


def _rms(x, g):
    return x * lax.rsqrt(jnp.mean(x * x, axis=-1, keepdims=True) + EPS) * g


def _sigmoid(x):
    return 1.0 / (1.0 + jnp.exp(-x))


def _gelu_tanh(x):
    return 0.5 * x * (1.0 + jnp.tanh(math.sqrt(2.0 / math.pi) * (x + 0.044715 * (x * x * x))))


def _softplus(x):
    return jnp.maximum(x, 0.0) + jnp.log1p(jnp.exp(-jnp.abs(x)))


def _swiglu(nb_ref, wg_ref, wu_ref, wd_ref, act_ref):
    d_ff = wg_ref.shape[1]
    for c in range(d_ff // MXU_COLS):
        sl = slice(c * MXU_COLS, (c + 1) * MXU_COLS)
        g = _dot(nb_ref[...], wg_ref[:, sl])
        u = _dot(nb_ref[...], wu_ref[:, sl])
        act_ref[:, sl] = (g * _sigmoid(g) * u).astype(BF16)
    return _dot(act_ref[...], wd_ref[...])


def _ffn_in_kernel(x_ref, pool0_ref, conv0_ref, g1_ref, wg_ref, wu_ref, wd_ref, gm_ref, win_ref,
                   sgn_ref, avg_ref, poolw_ref, pools_ref, convw_ref,
                   h_ref, zg_ref, ug_ref, vn_ref, yb_ref, q_ref, k_ref, v_ref, poolst_ref, convst_ref,
                   nb_ref, act_ref, uv_s, pbuf, cbuf, *, splits, sl, tps, pos0):
    i = pl.program_id(0)
    tm = x_ref.shape[0]
    nseq = tm // sl
    sgu_w = sgn_ref.shape[1]
    pool_w = pools_ref.shape[1]
    dk = q_ref.shape[2]
    gdn_w = GDN_HEADS * dk
    ncb = 3 * GDN_HEADS
    o_pool, o_qkv, o_z, n_pad = splits
    iv = i

    @pl.when(i % tps == 0)
    def _():
        pbuf[:, 0:POOL_PAD, :] = pool0_ref[...]
        for cb in range(ncb):
            cbuf[:, cb, 0:CONV_PAD, :] = conv0_ref[:, :, cb * dk:(cb + 1) * dk]

    def matmul_stage():
        x = x_ref[...]
        nb_ref[...] = _rms(x, g1_ref[...]).astype(BF16)
        h = x + 0.5 * _swiglu(nb_ref, wg_ref, wu_ref, wd_ref, act_ref)
        h_ref[...] = h
        nb_ref[...] = _rms(h, gm_ref[...]).astype(BF16)
        uv_s[...] = _dot(nb_ref[...], win_ref[:, 0:o_pool])
        zg_ref[...] = _dot(nb_ref[...], win_ref[:, o_z:n_pad])
        p = _dot(nb_ref[...], win_ref[:, o_pool:o_qkv])
        qkv = _dot(nb_ref[...], win_ref[:, o_qkv:o_z])
        for s in range(nseq):
            pbuf[s, POOL_PAD:POOL_PAD + sl, :] = p[s * sl:(s + 1) * sl]
            for cb in range(ncb):
                cbuf[s, cb, CONV_PAD:CONV_PAD + sl, :] = qkv[s * sl:(s + 1) * sl, cb * dk:(cb + 1) * dk]

    def vector_stage():
        uv = _gelu_tanh(uv_s[...])
        ug_ref[...] = uv[:, 0:sgu_w]
        v = uv[:, sgu_w:]
        avg = avg_ref[...]

        def seg_mean(t):
            hi = t.astype(BF16)
            lo = (t - hi.astype(F32)).astype(BF16)
            return _dot(hi, avg) + _dot(lo, avg)

        vc = v - seg_mean(v)
        vn_ref[...] = vc * lax.rsqrt(seg_mean(vc * vc) + EPS) * sgn_ref[...]

        gdim = pool_w // len(POOL_WINDOWS)
        lane = lax.broadcasted_iota(jnp.int32, (sl, LANES), 1)
        pos = pos0 + (iv % tps) * sl + lax.broadcasted_iota(jnp.int32, (sl, LANES), 0)
        dparts = []
        for s in range(nseq):
            dblk = []
            for blk in range(pool_w // LANES):
                ls = slice(blk * LANES, (blk + 1) * LANES)
                wins = [POOL_WINDOWS[(blk * LANES + l0) // gdim] for l0 in range(0, LANES, gdim)]
                wl = jnp.full((sl, LANES), wins[-1], jnp.int32)
                for gi in range(len(wins) - 2, -1, -1):
                    wl = jnp.where(lane < (gi + 1) * gdim, wins[gi], wl)
                pb = pbuf[s, POOL_PAD:POOL_PAD + sl, ls]
                acc = pb
                prev = 1
                for w in sorted(set(wins)):
                    part = None
                    for jj in range(prev, w):
                        sh = pbuf[s, POOL_PAD - jj:POOL_PAD - jj + sl, ls]
                        part = sh if part is None else part + sh
                    if part is not None:
                        acc = acc + (part if w == min(wins) else jnp.where(wl >= w, part, 0.0))
                    prev = w
                cnt = jnp.minimum(wl, pos + 1).astype(F32)
                dblk.append(acc / cnt - pb)
            dparts.append(jnp.concatenate(dblk, axis=1))
            tail = pbuf[s, sl:sl + POOL_PAD, :]
            poolst_ref[s] = tail
            pbuf[s, 0:POOL_PAD, :] = tail
        dlt = (dparts[0] if nseq == 1 else jnp.concatenate(dparts, axis=0)).astype(BF16)
        yb_ref[...] = (_dot(dlt, poolw_ref[...]) * pools_ref[...]).astype(BF16)

        qscale = dk ** -0.5
        nrow = sl // CONV_PHASES
        for cb in range(ncb):
            cs = slice(cb * dk, (cb + 1) * dk)
            which, hh = divmod(cb, GDN_HEADS)
            for s in range(nseq):
                for r in range(CONV_PHASES):
                    y = None
                    for t in range(GDN_CONV):
                        tap = (cbuf[s, cb, pl.ds(CONV_PAD + r - t, nrow, stride=CONV_PHASES), :]
                               * convw_ref[GDN_CONV - 1 - t:GDN_CONV - t, cs])
                        y = tap if y is None else y + tap
                    y = y * _sigmoid(y)
                    out_rows = pl.ds(s * sl + r, nrow, stride=CONV_PHASES)
                    if which == 0:
                        q_ref[hh, out_rows, :] = y * (lax.rsqrt(jnp.sum(y * y, axis=-1, keepdims=True) + L2_EPS)
                                                      * qscale)
                    elif which == 1:
                        k_ref[hh, out_rows, :] = y * lax.rsqrt(jnp.sum(y * y, axis=-1, keepdims=True) + L2_EPS)
                    else:
                        v_ref[hh, out_rows, :] = y
        for s in range(nseq):
            for cb in range(ncb):
                ctail = cbuf[s, cb, sl:sl + CONV_PAD, :]
                convst_ref[s, :, cb * dk:(cb + 1) * dk] = ctail
                cbuf[s, cb, 0:CONV_PAD, :] = ctail

    matmul_stage()
    vector_stage()


def _res(w):
    if isinstance(w, tuple):
        arr, l = w
        nd = arr.ndim - 1
        return pl.BlockSpec((None,) + arr.shape[1:], lambda *_: (l,) + (0,) * nd, pipeline_mode=pl.Buffered(1))
    nd = w.ndim
    return pl.BlockSpec(w.shape, lambda *_: (0,) * nd, pipeline_mode=pl.Buffered(1))


def _arr(w):
    return w[0] if isinstance(w, tuple) else w


def _cast_kernel(x_ref, o_ref):
    o_ref[...] = x_ref[...].astype(BF16)


def _cast_bf16(w):
    depth, rows, c = w.shape
    br = max(r for r in range(8, rows + 1, 8) if rows % r == 0 and (r * c * 4 <= CAST_BLOCK_BYTES or r == 8))
    blk = pl.BlockSpec((None, br, c), lambda l, i: (l, i, 0))
    return pl.pallas_call(
        _cast_kernel,
        grid=(depth, rows // br),
        in_specs=[blk],
        out_specs=blk,
        out_shape=jax.ShapeDtypeStruct((depth, rows, c), BF16),
        compiler_params=pltpu.CompilerParams(dimension_semantics=("parallel", "parallel")),
        name="cast_bf16",
    )(w)


def _cast_transposed_kernel(x_ref, o_ref, *, valid):
    cols = pl.program_id(1) * x_ref.shape[0] + lax.broadcasted_iota(jnp.int32, x_ref.shape, 0)
    o_ref[...] = jnp.where(cols < valid, x_ref[...], 0.0).T.astype(BF16)


def _cast_bf16_transposed(wt, cols):
    depth, c, rows = wt.shape
    bc = CAST_T_COLS
    return pl.pallas_call(
        functools.partial(_cast_transposed_kernel, valid=c),
        grid=(depth, pl.cdiv(cols, bc)),
        in_specs=[pl.BlockSpec((None, bc, rows), lambda l, i: (l, i, 0))],
        out_specs=pl.BlockSpec((None, rows, bc), lambda l, i: (l, 0, i)),
        out_shape=jax.ShapeDtypeStruct((depth, rows, cols), BF16),
        compiler_params=pltpu.CompilerParams(dimension_semantics=("parallel", "parallel")),
        name="cast_bf16_t",
    )(wt)


def _ffn_in(x2d, pool0, conv0, lw, tm, seq_len, pos0):
    t, d = x2d.shape
    d_ff = _arr(lw["wg1"]).shape[-1]
    o_pool, o_qkv, o_z, n_pad = lw["splits"]
    sgu_w, pool_w, gdn_w = lw["sgn"].shape[1], lw["pools"].shape[1], lw["onorm"].shape[1]
    sl = min(seq_len, tm)
    nseq = tm // sl
    tps = seq_len // sl
    row = lambda width: pl.BlockSpec((tm, width), lambda i: (i, 0))
    hist = lambda pad, width: pl.BlockSpec((nseq, pad, width), lambda i: (i // tps, 0, 0))
    res = _res
    weights = [lw["g1"], lw["wg1"], lw["wu1"], lw["wd1"], lw["gmix"], lw["win"],
               lw["sgn"], lw["avg"], lw["poolw"], lw["pools"], lw["convw"]]
    outs = [(d, F32), (n_pad - o_z, F32), (sgu_w, F32), (sgu_w, F32), (pool_w, BF16)]
    dk = gdn_w // GDN_HEADS
    head_rows = pl.BlockSpec((GDN_HEADS, tm, dk), lambda i: (0, i, 0))
    nbatch = t // seq_len
    return pl.pallas_call(
        functools.partial(_ffn_in_kernel, splits=lw["splits"], sl=sl, tps=tps, pos0=pos0),
        grid=(t // tm,),
        in_specs=[row(d), hist(POOL_PAD, pool_w), hist(CONV_PAD, 3 * gdn_w)] + [res(w) for w in weights],
        out_specs=[row(w) for w, _ in outs] + [head_rows] * 3 + [hist(POOL_PAD, pool_w), hist(CONV_PAD, 3 * gdn_w)],
        out_shape=[jax.ShapeDtypeStruct((t, w), dt) for w, dt in outs]
                  + [jax.ShapeDtypeStruct((GDN_HEADS, t, dk), F32)] * 3
                  + [jax.ShapeDtypeStruct((nbatch, POOL_PAD, pool_w), F32),
                     jax.ShapeDtypeStruct((nbatch, CONV_PAD, 3 * gdn_w), F32)],
        scratch_shapes=[pltpu.VMEM((tm, d), BF16), pltpu.VMEM((tm, d_ff), BF16),
                        pltpu.VMEM((tm, o_pool), F32),
                        pltpu.VMEM((nseq, POOL_PAD + sl, pool_w), F32),
                        pltpu.VMEM((nseq, 3 * GDN_HEADS, CONV_PAD + sl, dk), F32)],
        compiler_params=pltpu.CompilerParams(dimension_semantics=("arbitrary",),
                                             vmem_limit_bytes=VMEM_LIMIT_BYTES),
        name="ffn_in",
    )(x2d, pool0, conv0, *(_arr(w) for w in weights))


def _mixer_kernel(ug_ref, vn_ref, yb_ref, q_ref, k_ref, v_ref, zg_ref, h_ref, ssm0_ref,
                  sgw_ref, sgb_ref, alog_ref, dtb_ref, onorm_ref, wout_f32_ref,
                  h2_ref, ssm_ref,
                  u_s, wq_s, ql_s, y_s, s_s, wout_ref, *, ts, c_sgu):
    j = pl.program_id(1)
    nj = pl.num_programs(1)
    ns = vn_ref.shape[0]
    sgu_w = vn_ref.shape[2]
    pool_w = yb_ref.shape[2]
    gdn_w = onorm_ref.shape[1]
    dk = gdn_w // GDN_HEADS
    cc = GDN_CHUNK
    nch = ts // cc
    pre_w = sgu_w + pool_w
    seqs = range(ns)

    @pl.when(jnp.logical_and(pl.program_id(0) == 0, j == 0))
    def _():
        wout_ref[...] = wout_f32_ref[...].astype(BF16)

    @pl.when(j == 0)
    def _():
        s_s[...] = ssm0_ref[...]

    hd = sgu_w // SGU_HEADS
    wi = lax.broadcasted_iota(jnp.int32, (c_sgu, SGU_HEADS * c_sgu), 0)
    wj = lax.broadcasted_iota(jnp.int32, (c_sgu, SGU_HEADS * c_sgu), 1) % c_sgu
    wmask = jnp.where(wi // SGU_BLOCK >= wj // SGU_BLOCK, sgw_ref[...], 0.0).astype(BF16)
    lane_head = lax.broadcasted_iota(jnp.int32, (c_sgu, sgu_w), 1) // hd
    for e in seqs:
        for c in range(ts // c_sgu):
            rows = slice(c * c_sgu, (c + 1) * c_sgu)
            vch = vn_ref[e, rows, :]
            vstack = jnp.concatenate([jnp.where(lane_head == hh, vch, 0.0) for hh in range(SGU_HEADS)],
                                     axis=0).astype(BF16)
            s = _dot(wmask, vstack) + sgb_ref[...]
            y_s[e * ts + c * c_sgu:e * ts + (c + 1) * c_sgu, 0:sgu_w] = (ug_ref[e, rows, :] * s).astype(BF16)
        y_s[e * ts:(e + 1) * ts, sgu_w:pre_w] = yb_ref[e]

    ti = lax.broadcasted_iota(jnp.int32, (2 * ts, ts), 0)
    tj = lax.broadcasted_iota(jnp.int32, (2 * ts, ts), 1)
    tr = jnp.where(ti < ts, ti, ti - ts)
    same_chunk = tr // cc == tj // cc
    summat = jnp.where(same_chunk, jnp.where(ti < ts, jnp.where(tj <= tr, 1.0, 0.0), 1.0), 0.0).astype(BF16)
    beta, gc, egc, erg, etot = [], [], [], [], []
    for e in seqs:
        gates = zg_ref[e, :, gdn_w:gdn_w + LANES]
        beta.append(_sigmoid(gates))
        g = -jnp.exp(alog_ref[...]) * _softplus(gates + dtb_ref[...])
        g_hi = g.astype(BF16)
        g_r = g - g_hi.astype(F32)
        g_mid = g_r.astype(BF16)
        g_lo = (g_r - g_mid.astype(F32)).astype(BF16)
        sums = _dot(summat, g_hi) + _dot(summat, g_mid) + _dot(summat, g_lo)
        gc.append(sums[0:ts])
        tot = sums[ts:2 * ts]
        egc.append(jnp.exp(gc[e]))
        erg.append(jnp.exp(tot - gc[e]))
        etot.append(jnp.exp(tot))

    ri = lax.broadcasted_iota(jnp.int32, (cc, cc), 0)
    ci = lax.broadcasted_iota(jnp.int32, (cc, cc), 1)
    incl = ri >= ci
    strict = ri > ci
    chains = [(e, n, hh) for n in range(nch) for e in seqs for hh in range(GDN_HEADS)]
    gct = {(e, n): gc[e][n * cc:(n + 1) * cc].T for e in seqs for n in range(nch)}

    nmats = []
    for e, n, hh in chains:
        rows = slice(n * cc, (n + 1) * cc)
        hs = slice(hh * dk, (hh + 1) * dk)
        gl = GDN_HEADS + hh
        k = k_ref[hh, e, rows, :]
        kbq = jnp.concatenate([k * beta[e][rows, hh:hh + 1], q_ref[hh, e, rows, :]], axis=0).astype(BF16)
        prod = _dot_nt(kbq, k.astype(BF16))
        decay = jnp.exp(jnp.where(incl, gc[e][rows, gl:gl + 1] - gct[e, n][gl:gl + 1, :], NEG_BIG))
        nmats.append(jnp.where(strict, -(prod[0:cc] * decay), 0.0))
        ql_s[e, n, hh, 0:cc, :] = (prod[cc:2 * cc] * decay).astype(BF16)
    same_block = lambda s: ri // s == ci // s
    eye = jnp.where(ri == ci, 1.0, 0.0)
    base = 8
    n1 = [jnp.where(same_block(base), m, 0.0) for m in nmats]
    n2 = [_dot(m.astype(BF16), m.astype(BF16)) for m in n1]
    tps = [a + b + _dot(a.astype(BF16), b.astype(BF16)) for a, b in zip(n1, n2)]
    n4 = [_dot(m.astype(BF16), m.astype(BF16)) for m in n2]
    dinv = [t + b + _dot(t.astype(BF16), b.astype(BF16)) + eye for t, b in zip(tps, n4)]
    s = base
    while s < cc:
        lower_left = jnp.logical_and(same_block(2 * s), jnp.logical_not(same_block(s)))
        cd = [_dot(jnp.where(lower_left, m, 0.0).astype(BF16), d.astype(BF16)) for m, d in zip(nmats, dinv)]
        dinv = [d + _dot(d.astype(BF16), c.astype(BF16)) for d, c in zip(dinv, cd)]
        s *= 2
    tps = [d - eye for d in dinv]
    for (e, n, hh), tp in zip(chains, tps):
        rows = slice(n * cc, (n + 1) * cc)
        hs = slice(hh * dk, (hh + 1) * dk)
        gl = GDN_HEADS + hh
        k = k_ref[hh, e, rows, :]
        b = beta[e][rows, hh:hh + 1]
        eg = egc[e][rows, gl:gl + 1]
        rhs = jnp.concatenate([v_ref[hh, e, rows, :] * b, k * (b * eg)], axis=1)
        sol = rhs + _dot(tp.astype(BF16), rhs.astype(BF16))
        u_s[e, rows, hs] = sol[:, 0:dk]
        wq_s[e, n, hh, 0:cc, :] = sol[:, dk:2 * dk].astype(BF16)
        wq_s[e, n, hh, cc:2 * cc, :] = (q_ref[hh, e, rows, :] * eg).astype(BF16)
        ql_s[e, n, hh, cc:cc + dk, :] = (k * erg[e][rows, gl:gl + 1]).T.astype(BF16)

    fill = []
    for kb in range(pre_w // MXU_COLS):
        for nb in range(h_ref.shape[2] // MXU_COLS):
            fill.append((slice(kb * MXU_COLS, (kb + 1) * MXU_COLS), slice(nb * MXU_COLS, (nb + 1) * MXU_COLS), kb == 0))
    gaps = 2 * nch

    def run_fill(gap):
        for ks, cs, first in fill[gap * len(fill) // gaps:(gap + 1) * len(fill) // gaps]:
            part = _dot(y_s[:, ks], wout_ref[ks, cs])
            for e in seqs:
                base = h_ref[e, :, cs] if first else h2_ref[e, :, cs]
                h2_ref[e, :, cs] = base + part[e * ts:(e + 1) * ts]

    heads = [(e, hh) for e in seqs for hh in range(GDN_HEADS)]
    for n in range(nch):
        rows = slice(n * cc, (n + 1) * cc)
        sts = [s_s[e, hh] for e, hh in heads]
        r1 = [_dot(wq_s[e, n, hh], st.astype(BF16)) for (e, hh), st in zip(heads, sts)]
        run_fill(2 * n)
        vnew = [u_s[e, rows, hh * dk:(hh + 1) * dk] - r[0:cc] for (e, hh), r in zip(heads, r1)]
        r2 = [_dot(ql_s[e, n, hh], vn.astype(BF16)) for (e, hh), vn in zip(heads, vnew)]
        run_fill(2 * n + 1)
        for (e, hh), st, ra, rb in zip(heads, sts, r1, r2):
            gl = GDN_HEADS + hh
            hs = slice(hh * dk, (hh + 1) * dk)
            glcol = etot[e][rows, gl:gl + 1]
            s_s[e, hh] = st * jnp.concatenate([glcol] * (dk // cc), axis=0) + rb[cc:cc + dk]
            o = ra[cc:2 * cc] + rb[0:cc]
            z = zg_ref[e, rows, hs]
            on = o * lax.rsqrt(jnp.mean(o * o, axis=-1, keepdims=True) + EPS) * onorm_ref[:, hs]
            y_s[e * ts + n * cc:e * ts + (n + 1) * cc, pre_w + hh * dk:pre_w + (hh + 1) * dk] = (
                on * (z * _sigmoid(z))).astype(BF16)

    @pl.when(j == nj - 1)
    def _():
        ssm_ref[...] = s_s[...]

    part = _dot(y_s[:, pre_w:], wout_ref[pre_w:, :])
    for e in seqs:
        h2_ref[e] = h2_ref[e] + part[e * ts:(e + 1) * ts]


def _mixer(h, ug, vn, yb, q, k, v, zg, ssm0, lw, ts, c_sgu, ns):
    b, l, d = h.shape
    sgu_w, pool_w, dk = vn.shape[2], yb.shape[2], q.shape[3]
    gdn_w = GDN_HEADS * dk
    nch = ts // GDN_CHUNK
    mix_w = sgu_w + pool_w + gdn_w
    tile = lambda width: pl.BlockSpec((ns, ts, width), lambda i, j: (i, j, 0))
    per_b = lambda *shape: pl.BlockSpec((ns,) + shape, lambda i, j: (i,) + (0,) * len(shape))
    heads_tile = pl.BlockSpec((GDN_HEADS, ns, ts, dk), lambda i, j: (0, i, j, 0))
    res = _res
    weights = [lw["sgw"], lw["sgb"], lw["alog"], lw["dtb"], lw["onorm"], lw["wout"]]
    return pl.pallas_call(
        functools.partial(_mixer_kernel, ts=ts, c_sgu=c_sgu),
        grid=(b // ns, l // ts),
        in_specs=[tile(sgu_w), tile(sgu_w), tile(pool_w), heads_tile, heads_tile, heads_tile,
                  tile(zg.shape[2]), tile(d), per_b(GDN_HEADS, dk, dk)] + [res(w) for w in weights],
        out_specs=[tile(d), per_b(GDN_HEADS, dk, dk)],
        out_shape=[jax.ShapeDtypeStruct((b, l, d), F32), jax.ShapeDtypeStruct((b, GDN_HEADS, dk, dk), F32)],
        scratch_shapes=[
            pltpu.VMEM((ns, ts, gdn_w), F32),
            pltpu.VMEM((ns, nch, GDN_HEADS, 2 * GDN_CHUNK, dk), BF16),
            pltpu.VMEM((ns, nch, GDN_HEADS, GDN_CHUNK + dk, GDN_CHUNK), BF16),
            pltpu.VMEM((ns * ts, mix_w), BF16),
            pltpu.VMEM((ns, GDN_HEADS, dk, dk), F32),
            pltpu.VMEM((mix_w, d), BF16),
        ],
        compiler_params=pltpu.CompilerParams(dimension_semantics=("arbitrary", "arbitrary"),
                                             vmem_limit_bytes=VMEM_LIMIT_BYTES),
        name="mixer",
    )(ug, vn, yb, q, k, v, zg, h, ssm0, *(_arr(w) for w in weights))


def _attn_ffn_kernel(h_ref, mk_ref, mv_ref, gx_ref, wq_f32_ref, wo_f32_ref, g2_ref, wg_ref, wu_ref, wd_ref, gf_ref,
                     out_ref, nb_ref, act_ref, ob_ref, kh_s, vh_s, wq_ref, wo_ref, *, final, sl, tps):
    nseq = h_ref.shape[0] // sl
    heads, dh = mk_ref.shape[2], mk_ref.shape[3]

    @pl.when(pl.program_id(0) == 0)
    def _():
        wq_ref[...] = wq_f32_ref[...].astype(BF16)
        wo_ref[...] = wo_f32_ref[...].astype(BF16)

    @pl.when(pl.program_id(0) % tps == 0)
    def _():
        for s in range(nseq):
            for hh in range(heads):
                kh_s[s, hh] = mk_ref[s, :, hh, :].astype(BF16)
                vh_s[s, hh] = mv_ref[s, :, hh, :].astype(BF16)

    h = h_ref[...]
    nb_ref[...] = _rms(h, gx_ref[...]).astype(BF16)
    q = _dot(nb_ref[...], wq_ref[...])
    scale = dh ** -0.5
    pairs = [(s, hh, slice(s * sl, (s + 1) * sl), slice(hh * dh, (hh + 1) * dh))
             for s in range(nseq) for hh in range(heads)]
    scs = [_dot_nt(q[rows, hs].astype(BF16), kh_s[s, hh]) * scale for s, hh, rows, hs in pairs]
    es = [jnp.exp(sc - jnp.max(sc, axis=-1, keepdims=True)) for sc in scs]
    dens = [jnp.sum(e, axis=-1, keepdims=True) for e in es]
    for (s, hh, rows, hs), e, den in zip(pairs, es, dens):
        ob_ref[rows, hs] = (_dot(e.astype(BF16), vh_s[s, hh]) / den).astype(BF16)
    h = h + _dot(ob_ref[...], wo_ref[...])
    nb_ref[...] = _rms(h, g2_ref[...]).astype(BF16)
    h = h + 0.5 * _swiglu(nb_ref, wg_ref, wu_ref, wd_ref, act_ref)
    if final:
        h = _rms(h, gf_ref[...])
    out_ref[...] = h


def _attn_ffn(h2d, mk, mv, l, lw, gf, tm, seq_len, final):
    t, d = h2d.shape
    d_ff = _arr(lw["wg2"]).shape[-1]
    m, heads, dh = mk.shape[2:]
    sl = min(seq_len, tm)
    nseq = tm // sl
    tps = seq_len // sl
    tile = pl.BlockSpec((tm, d), lambda i: (i, 0))
    mem = pl.BlockSpec((None, nseq, m, heads, dh), lambda i: (l, i // tps, 0, 0, 0))
    weights = [lw["gx"], lw["wmq"], lw["wmo"], lw["g2"], lw["wg2"], lw["wu2"], lw["wd2"], gf]
    return pl.pallas_call(
        functools.partial(_attn_ffn_kernel, final=final, sl=sl, tps=tps),
        grid=(t // tm,),
        in_specs=[tile, mem, mem] + [_res(w) for w in weights],
        out_specs=tile,
        out_shape=jax.ShapeDtypeStruct((t, d), F32),
        scratch_shapes=[pltpu.VMEM((tm, d), BF16), pltpu.VMEM((tm, d_ff), BF16), pltpu.VMEM((tm, heads * dh), BF16),
                        pltpu.VMEM((nseq, heads, m, dh), BF16), pltpu.VMEM((nseq, heads, m, dh), BF16),
                        pltpu.VMEM((d, heads * dh), BF16), pltpu.VMEM((heads * dh, d), BF16)],
        compiler_params=pltpu.CompilerParams(dimension_semantics=("arbitrary",),
                                             vmem_limit_bytes=VMEM_LIMIT_BYTES),
        name="attn_ffn",
    )(h2d, mk, mv, *(_arr(w) for w in weights))


def _mem_kv_kernel(mem_ref, g_ref, wk_ref, wv_ref, k_ref, v_ref):
    mb = _rms(mem_ref[...], g_ref[...]).astype(BF16)
    k = _dot(mb, wk_ref[...].astype(BF16))
    v = _dot(mb, wv_ref[...].astype(BF16))
    heads, dh = k_ref.shape[1], k_ref.shape[2]
    for hh in range(heads):
        k_ref[:, hh, :] = k[:, hh * dh:(hh + 1) * dh]
        v_ref[:, hh, :] = v[:, hh * dh:(hh + 1) * dh]


def _mem_kv(mem, g, wk, wv, heads):
    b, m, d = mem.shape
    depth, _, mw = wk.shape
    per_layer = lambda *shape: pl.BlockSpec((None,) + shape, lambda l, i: (l,) + (0,) * len(shape))
    out = pl.BlockSpec((None, None, m, heads, mw // heads), lambda l, i: (l, i, 0, 0, 0))
    return pl.pallas_call(
        _mem_kv_kernel,
        grid=(depth, b),
        in_specs=[pl.BlockSpec((None, m, d), lambda l, i: (i, 0, 0)), per_layer(1, d), per_layer(d, mw),
                  per_layer(d, mw)],
        out_specs=[out, out],
        out_shape=[jax.ShapeDtypeStruct((depth, b, m, heads, mw // heads), F32)] * 2,
        compiler_params=pltpu.CompilerParams(dimension_semantics=("parallel", "parallel")),
        name="mem_kv",
    )(mem, g.reshape(depth, 1, d), wk, wv)


def _layer_weights(l, c_sgu, big, ffn1_norm, mix_norm, sgu_norm, sgu_w, sgu_b, pool_w, pool_scale, gdn_conv_w,
                   gdn_a_log, gdn_dt_bias, gdn_out_norm, xattn_norm, ffn2_norm):
    sgu_width = sgu_norm.shape[1]
    pool_width = pool_scale.shape[1]
    gdn_width = gdn_out_norm.shape[1] * GDN_HEADS
    row = lambda vec: vec.reshape(1, -1).astype(F32)
    groups = len(POOL_WINDOWS)
    gdim = pool_width // groups
    poolw = jnp.zeros((pool_width, pool_width), F32)
    for gi in range(groups):
        poolw = poolw.at[gi * gdim:(gi + 1) * gdim, gi * gdim:(gi + 1) * gdim].set(pool_w[l, gi])
    hd = sgu_width // SGU_HEADS
    seg = jnp.arange(sgu_width) // hd
    avg = jnp.where(seg[:, None] == seg[None, :], 1.0 / hd, 0.0).astype(BF16)
    sgw = jnp.concatenate([sgu_w[l, hh, :c_sgu, :c_sgu] for hh in range(SGU_HEADS)], axis=1)
    sgb = jnp.repeat(sgu_b[l, :, :c_sgu].T, hd, axis=1)
    lane_pad = lambda vec: jnp.pad(vec, (GDN_HEADS, LANES - 2 * GDN_HEADS)).reshape(1, LANES)
    lw = {name: (arr, l) for name, arr in big.items()}
    lw.update(
        splits=_in_splits(sgu_width, pool_width, gdn_width),
        g1=row(ffn1_norm[l]), gmix=row(mix_norm[l]),
        sgw=sgw, sgb=sgb, sgn=row(sgu_norm[l]), avg=avg,
        poolw=poolw.astype(BF16), pools=row(pool_scale[l]),
        convw=gdn_conv_w[l], alog=lane_pad(gdn_a_log[l]), dtb=lane_pad(gdn_dt_bias[l]),
        onorm=row(jnp.tile(gdn_out_norm[l], GDN_HEADS)),
        gx=row(xattn_norm[l]), g2=row(ffn2_norm[l]),
    )
    return lw


def _in_splits(sgu_width, pool_width, gdn_width):
    o_pool = 2 * sgu_width
    o_qkv = o_pool + pool_width
    o_z = o_qkv + 3 * gdn_width
    return o_pool, o_qkv, o_z, o_z + gdn_width + LANES


def _tiles(batch, seq):
    total = batch * seq
    return min(FFN_IN_ROWS, total), min(MIXER_ROWS, seq), min(ATTN_FFN_ROWS, total)


def _layer(x, mk, mv, l, pool0, conv0, ssm0, lw, gf, final, pos0):
    b, seq, d = x.shape
    tm, ts, tm_attn = _tiles(b, seq)
    h, zg, ug, vn, yb, q, k, v, pool_new, conv_new = _ffn_in(x.reshape(b * seq, d), pool0, conv0, lw, tm, seq, pos0)
    shp = lambda arr: arr.reshape(b, seq, arr.shape[-1])
    c_sgu = lw["sgb"].shape[0]
    hshp = lambda arr: arr.reshape(arr.shape[0], b, seq, arr.shape[-1])
    h2, ssm_new = _mixer(shp(h), shp(ug), shp(vn), shp(yb), hshp(q), hshp(k), hshp(v), shp(zg), ssm0, lw, ts, c_sgu,
                         MIXER_SEQS)
    out = _attn_ffn(h2.reshape(b * seq, d), mk, mv, l, lw, gf, tm_attn, seq, final)
    return (shp(out), shp(vn), pool_new[:, POOL_PAD - POOL_HIST:], conv_new[:, CONV_PAD - (GDN_CONV - 1):], ssm_new)


def kernel(x_prompt, x_sample, mem_prompt, cache_mem_k, cache_mem_v, state_pool, state_conv, state_ssm, ffn1_norm, ffn1_w_gate, ffn1_w_up, ffn1_w_down, mix_norm, w_in, sgu_norm, sgu_w, sgu_b, pool_w, pool_scale, gdn_conv_w, gdn_a_log, gdn_dt_bias, gdn_out_norm, w_out, xattn_norm, mem_norm, w_mq, w_mk, w_mv, w_mo, ffn2_norm, ffn2_w_gate, ffn2_w_up, ffn2_w_down, final_norm):
    depth = w_in.shape[0]
    bp, lp, d = x_prompt.shape
    bs, ls, _ = x_sample.shape
    pool_width = state_pool.shape[-1]
    qkv_width = state_conv.shape[-1]
    dk = state_ssm.shape[-1]
    n_pad = _in_splits(sgu_norm.shape[1], pool_width, gdn_out_norm.shape[1] * GDN_HEADS)[3]
    big = dict(wg1=_cast_bf16(ffn1_w_gate), wu1=_cast_bf16(ffn1_w_up), wd1=_cast_bf16(ffn1_w_down),
               win=_cast_bf16_transposed(jnp.swapaxes(w_in, 1, 2), n_pad), wout=w_out, wmq=w_mq, wmo=w_mo,
               wg2=_cast_bf16(ffn2_w_gate), wu2=_cast_bf16(ffn2_w_up), wd2=_cast_bf16(ffn2_w_down))
    small = (ffn1_norm, mix_norm, sgu_norm, sgu_w, sgu_b, pool_w, pool_scale, gdn_conv_w, gdn_a_log, gdn_dt_bias,
             gdn_out_norm, xattn_norm, ffn2_norm)
    gf = final_norm.reshape(1, d)
    prompt_mk, prompt_mv = _mem_kv(mem_prompt, mem_norm, w_mk, w_mv, cache_mem_k.shape[3])
    pool0_p = jnp.zeros((bp, POOL_PAD, pool_width), F32)
    conv0_p = jnp.zeros((bp, CONV_PAD, qkv_width), F32)
    ssm0_p = jnp.zeros((bp, GDN_HEADS, dk, dk), F32)
    xp, xs = x_prompt, x_sample
    outs_p = [[] for _ in range(3)]
    outs_s = [[] for _ in range(4)]
    for l in range(depth):
        final = l == depth - 1
        lw_p = _layer_weights(l, min(128, lp), big, *small)
        lw_s = lw_p if min(128, ls) == min(128, lp) else _layer_weights(l, min(128, ls), big, *small)
        xp, _, pp, pc, ps = _layer(xp, prompt_mk, prompt_mv, l, pool0_p, conv0_p, ssm0_p, lw_p, gf, final, 0)
        pool0_s = jnp.pad(state_pool[l], ((0, 0), (POOL_PAD - POOL_HIST, 0), (0, 0)))
        conv0_s = jnp.pad(state_conv[l], ((0, 0), (CONV_PAD - (GDN_CONV - 1), 0), (0, 0)))
        xs, sv, sp, sc, ss = _layer(xs, cache_mem_k, cache_mem_v, l, pool0_s, conv0_s, state_ssm[l], lw_s, gf, final,
                                    PAST_LEN)
        for lst, val in zip(outs_p, (pp, pc, ps)):
            lst.append(val)
        for lst, val in zip(outs_s, (sp, sc, ss, sv)):
            lst.append(val)
    return (xp, xs, *(jnp.stack(v) for v in outs_p), prompt_mk, prompt_mv, *(jnp.stack(v) for v in outs_s))
```
